```python
import math
import jax
import jax.numpy as jnp
from jax import lax
import numpy as np

D_MODEL = 1024
BATCH = 32
SEQ = 2048
DEPTH = 4

CTX_LEN = 256
GRID_W = 64

GDN_HEADS = 6
GDN_DK = 64
GDN_DV = 64
GDN_W = GDN_HEADS * GDN_DV
GDN_QKV = GDN_HEADS * (2 * GDN_DK + GDN_DV)
CONV_K = 5
CHUNK = 64
ATT_HEADS = 6
ATT_KV_HEADS = 2
ATT_DH = 64
ATT_W = ATT_HEADS * ATT_DH
ATT_GROUP = ATT_HEADS // ATT_KV_HEADS
Q_BLOCK = 128
ROPE_THETA = 10000.0
ROPE_PAIRS = ATT_DH // 4
S5_GROUPS = 16
S5_GH = 16
S5_P = 64
S5_W = S5_GROUPS * S5_GH
D_MIX = GDN_W + ATT_W + S5_W
D_FF = 2816
N_MOD = 9
ALPHA = (2.0 * DEPTH) ** 0.25
BETA_INIT = (8.0 * DEPTH) ** -0.25
EPS = 1e-6

OFF_GDN_Z = GDN_QKV
OFF_GDN_B = OFF_GDN_Z + GDN_W
OFF_GDN_A = OFF_GDN_B + 2 * GDN_HEADS
OFF_ATT_Q = OFF_GDN_A + 2 * GDN_HEADS
OFF_ATT_K = OFF_ATT_Q + ATT_W
OFF_ATT_V = OFF_ATT_K + ATT_KV_HEADS * ATT_DH
OFF_S5 = OFF_ATT_V + ATT_KV_HEADS * ATT_DH
IN_COLS = OFF_S5 + S5_W
IN_CUTS = (OFF_GDN_Z, OFF_GDN_B, OFF_GDN_A, OFF_ATT_Q, OFF_ATT_K, OFF_ATT_V, OFF_S5)

kernel_name = 'hybrid_gdn_gqa_s5_dit_layer'


def _layer_norm(x, g, b):
    xf = x.astype(jnp.float32)
    mu = jnp.mean(xf, axis=-1, keepdims=True)
    xc = xf - mu
    var = jnp.mean(xc * xc, axis=-1, keepdims=True)
    return (xc * lax.rsqrt(var + EPS) * g.astype(jnp.float32) + b.astype(jnp.float32)).astype(x.dtype)


def _rms_norm(x, w):
    xf = x.astype(jnp.float32)
    y = xf * lax.rsqrt(jnp.mean(xf * xf, axis=-1, keepdims=True) + EPS) * w.astype(jnp.float32)
    return y.astype(x.dtype)


def _l2_normalize(x):
    return x * lax.rsqrt(jnp.sum(x * x, axis=-1, keepdims=True) + EPS)


def _swiglu(h, w1, w3, w2):
    return (jax.nn.silu(h @ w1) * (h @ w3)) @ w2


def _post_norm_residual(x, y, gate, res_w, g, b):
    return _layer_norm(ALPHA * x + res_w * gate * y, g, b)


def _dwconv_centred(x, w):
    pad = CONV_K // 2
    return lax.conv_general_dilated(
        x, w[:, None, :].astype(x.dtype), window_strides=(1,), padding=[(pad, pad)],
        dimension_numbers=('NWC', 'WIO', 'NWC'), feature_group_count=x.shape[-1])


def _gated_delta_chunked(q, k, v, g, beta, s0):
    Bn, H, T, dk = q.shape
    dv = v.shape[-1]
    n = T // CHUNK
    q = (q * dk ** -0.5).reshape(Bn, H, n, CHUNK, dk)
    k = k.reshape(Bn, H, n, CHUNK, dk)
    v = v.reshape(Bn, H, n, CHUNK, dv)
    g = jnp.cumsum(g.reshape(Bn, H, n, CHUNK), axis=-1)
    beta = beta.reshape(Bn, H, n, CHUNK)[..., None]
    incl = jnp.tril(jnp.ones((CHUNK, CHUNK), dtype=bool))
    strict = jnp.tril(jnp.ones((CHUNK, CHUNK), dtype=bool), -1)
    diff = g[..., :, None] - g[..., None, :]
    decay = jnp.where(incl, jnp.exp(jnp.where(incl, diff, 0.0)), 0.0)
    kb = k * beta
    lower = jnp.where(strict, jnp.einsum('bhnid,bhnjd->bhnij', kb, k) * decay, 0.0)
    eye = jnp.eye(CHUNK, dtype=q.dtype)
    rhs = jnp.concatenate([v * beta, kb * jnp.exp(g)[..., None]], axis=-1)
    sol = lax.linalg.triangular_solve(eye + lower, rhs, left_side=True, lower=True, unit_diagonal=True)
    u, w = sol[..., :dv], sol[..., dv:]
    intra = jnp.where(incl, jnp.einsum('bhnid,bhnjd->bhnij', q, k) * decay, 0.0)

    def step(state, inp):
        q_i, k_i, u_i, w_i, g_i, a_i = inp
        v_new = u_i - jnp.einsum('bhcd,bhde->bhce', w_i, state)
        o_i = (jnp.einsum('bhcd,bhde->bhce', q_i * jnp.exp(g_i)[..., None], state)
               + jnp.einsum('bhij,bhje->bhie', a_i, v_new))
        g_last = g_i[..., -1:]
        state = (state * jnp.exp(g_last)[..., None]
                 + jnp.einsum('bhcd,bhce->bhde', k_i * jnp.exp(g_last - g_i)[..., None], v_new))
        return state, o_i

    xs = tuple(jnp.moveaxis(t, 2, 0) for t in (q, k, u, w, g, intra))
    s_final, o = lax.scan(step, s0, xs)
    return jnp.moveaxis(o, 0, 2).reshape(Bn, H, T, dv), s_final


def _gdn_inputs(qkv, b, a, conv_w, a_log, dt_bias):
    Bn, T, _ = qkv.shape
    qkv = jax.nn.silu(_dwconv_centred(qkv, conv_w))
    q, k, v = jnp.split(qkv, [GDN_HEADS * GDN_DK, 2 * GDN_HEADS * GDN_DK], axis=-1)

    def to_heads(t, d):
        return t.reshape(Bn, T, GDN_HEADS, d).transpose(0, 2, 1, 3).astype(jnp.float32)

    def dir_heads(t):
        return t.reshape(Bn, T, 2, GDN_HEADS).transpose(2, 0, 3, 1).astype(jnp.float32)

    q = _l2_normalize(to_heads(q, GDN_DK))
    k = _l2_normalize(to_heads(k, GDN_DK))
    v = to_heads(v, GDN_DV)
    beta = jax.nn.sigmoid(dir_heads(b))
    g = (-jnp.exp(a_log.astype(jnp.float32))[:, None, :, None]
         * jax.nn.softplus(dir_heads(a) + dt_bias.astype(jnp.float32)[:, None, :, None]))
    return q, k, v, g, beta


def _gdn_gated_out(o, z, norm_w):
    Bn, H, T, dv = o.shape
    o = _rms_norm(o.transpose(0, 2, 1, 3), norm_w)
    o = o * jax.nn.silu(z.reshape(Bn, T, H, dv).astype(jnp.float32))
    return o.reshape(Bn, T, H * dv).astype(z.dtype)


def _gdn_group(qkv, z, b, a, qkv_c, z_c, b_c, a_c, conv_w, a_log, dt_bias, norm_w, with_ctx_out):
    q, k, v, g, beta = _gdn_inputs(qkv, b, a, conv_w, a_log, dt_bias)
    qc, kc, vc, gc, betac = _gdn_inputs(qkv_c, b_c, a_c, conv_w, a_log, dt_bias)
    s0 = jnp.zeros((q.shape[0], GDN_HEADS, GDN_DK, GDN_DV), jnp.float32)

    def flip(t):
        return jnp.flip(t, axis=2)

    oc_f, sc_f = _gated_delta_chunked(qc, kc, vc, gc[0], betac[0], s0)
    o_f, _ = _gated_delta_chunked(q, k, v, g[0], beta[0], sc_f)
    oc_b, sc_b = _gated_delta_chunked(flip(qc), flip(kc), flip(vc), flip(gc[1]), flip(betac[1]), s0)
    o_b, _ = _gated_delta_chunked(flip(q), flip(k), flip(v), flip(g[1]), flip(beta[1]), sc_b)
    out = _gdn_gated_out(o_f + flip(o_b), z, norm_w)
    out_c = _gdn_gated_out(oc_f + flip(oc_b), z_c, norm_w) if with_ctx_out else None
    return out, out_c


def _axial_rope_tables(rows):
    row = jnp.repeat(jnp.arange(rows), GRID_W)
    col = jnp.tile(jnp.arange(GRID_W), rows)
    inv_freq = ROPE_THETA ** (-jnp.arange(ROPE_PAIRS, dtype=jnp.float32) / ROPE_PAIRS)
    ang = jnp.stack([row, col], axis=-1).astype(jnp.float32)[..., None] * inv_freq
    return jnp.cos(ang), jnp.sin(ang)


def _rope_2d(x, cos, sin):
    shp = x.shape
    xr = x.reshape(*shp[:-1], 2, 2, ROPE_PAIRS).astype(jnp.float32)
    x1, x2 = xr[..., 0, :], xr[..., 1, :]
    bshape = (shp[1],) + (1,) * (x.ndim - 3) + (2, ROPE_PAIRS)
    c, s = cos.reshape(bshape), sin.reshape(bshape)
    out = jnp.stack([x1 * c - x2 * s, x2 * c + x1 * s], axis=-2)
    return out.reshape(shp).astype(x.dtype)


def _attend(qb, keys, vals):
    s = jnp.einsum('bqhgd,bkhd->bhgqk', qb, keys).astype(jnp.float32) * (ATT_DH ** -0.5)
    p = jax.nn.softmax(s, axis=-1).astype(vals.dtype)
    return jnp.einsum('bhgqk,bkhd->bqhgd', p, vals)


def _attention_group(q, k, v, q_c, k_c, v_c, q_norm_w, k_norm_w, cos, sin, with_ctx_out):
    Bn, T, _ = q.shape
    Tc = q_c.shape[1]
    q = _rope_2d(_rms_norm(q.reshape(Bn, T, ATT_KV_HEADS, ATT_GROUP, ATT_DH), q_norm_w), cos, sin)
    k = _rope_2d(_rms_norm(k.reshape(Bn, T, ATT_KV_HEADS, ATT_DH), k_norm_w), cos, sin)
    v = v.reshape(Bn, T, ATT_KV_HEADS, ATT_DH)
    qc = _rms_norm(q_c.reshape(Bn, Tc, ATT_KV_HEADS, ATT_GROUP, ATT_DH), q_norm_w)
    kc = _rms_norm(k_c.reshape(Bn, Tc, ATT_KV_HEADS, ATT_DH), k_norm_w)
    vc = v_c.reshape(Bn, Tc, ATT_KV_HEADS, ATT_DH)
    keys = jnp.concatenate([kc, k], axis=1)
    vals = jnp.concatenate([vc, v], axis=1)
    nb = T // Q_BLOCK
    q_blocks = q.reshape(Bn, nb, Q_BLOCK, ATT_KV_HEADS, ATT_GROUP, ATT_DH).transpose(1, 0, 2, 3, 4, 5)
    o = lax.map(lambda qb: _attend(qb, keys, vals), q_blocks)
    o = o.transpose(1, 0, 2, 3, 4, 5).reshape(Bn, T, ATT_W)
    o_c = _attend(qc, kc, vc).reshape(Bn, Tc, ATT_W) if with_ctx_out else None
    return o, o_c


def _s5_discretize(lam_re, lam_im, log_dt, b_re, b_im):
    lam = lax.complex(lam_re.astype(jnp.float32), lam_im.astype(jnp.float32))
    dt = jnp.exp(log_dt.astype(jnp.float32))[:, None]
    lam_bar = jnp.exp(lam * dt)
    b = lax.complex(b_re.astype(jnp.float32), b_im.astype(jnp.float32))
    b_bar = ((lam_bar - 1.0) / lam)[..., None] * b
    return lam_bar, b_bar


def _s5_scan(u, lam_bar, b_bar, h0):
    bu = jnp.einsum('gph,btgh->tbgp', b_bar, u.astype(jnp.complex64))
    if h0 is not None:
        bu = bu.at[0].add(lam_bar * h0)
    a = jnp.broadcast_to(lam_bar, (bu.shape[0], 1) + lam_bar.shape)

    def combine(e1, e2):
        a1, b1 = e1
        a2, b2 = e2
        return a2 * a1, a2 * b1 + b2

    _, h = lax.associative_scan(combine, (a, bu), axis=0)
    return h


def _s5_readout(h, c_re, c_im):
    return (jnp.einsum('ghp,tbgp->btgh', c_re.astype(jnp.float32), h.real)
            - jnp.einsum('ghp,tbgp->btgh', c_im.astype(jnp.float32), h.imag))


def _s5_group(u, u_c, lam_re, lam_im, log_dt, b_re, b_im, c_re, c_im, d_skip, glu_w, glu_b, with_ctx_out):
    def grp(t):
        return t.reshape(t.shape[0], t.shape[1], S5_GROUPS, S5_GH).astype(jnp.float32)

    ul, uc = grp(u), grp(u_c)
    dg = d_skip.astype(jnp.float32).reshape(S5_GROUPS, S5_GH)
    yl, yc = dg * ul, dg * uc
    for d in range(2):
        lam_bar, b_bar = _s5_discretize(lam_re[d], lam_im[d], log_dt[d], b_re[d], b_im[d])
        if d == 1:
            fl = lambda t: jnp.flip(t, axis=1)
        else:
            fl = lambda t: t
        hc = _s5_scan(fl(uc), lam_bar, b_bar, None)
        hl = _s5_scan(fl(ul), lam_bar, b_bar, hc[-1])
        yl = yl + fl(_s5_readout(hl, c_re[d], c_im[d]))
        if with_ctx_out:
            yc = yc + fl(_s5_readout(hc, c_re[d], c_im[d]))

    def glu(y):
        zz = jax.nn.gelu(y.reshape(y.shape[0], y.shape[1], S5_W))
        return (zz * jax.nn.sigmoid(zz @ glu_w.astype(jnp.float32) + glu_b.astype(jnp.float32))).astype(u.dtype)

    return glu(yl), (glu(yc) if with_ctx_out else None)


def _modulate(t, m, i):
    return t * (1.0 + m[:, 3 * i + 1]) + m[:, 3 * i]


def _trunk_layer(x, xc, mod, mod_c, cos, sin, ln_g, ln_b, ffn_w1, ffn_w3, ffn_w2, w_in, w_out,
                 gdn_conv_w, gdn_a_log, gdn_dt_bias, gdn_norm_w, q_norm_w, k_norm_w,
                 s5_lam_re, s5_lam_im, s5_log_dt, s5_b_re, s5_b_im, s5_c_re, s5_c_im, s5_d, glu_w, glu_b,
                 with_ctx_out):
    x = _post_norm_residual(x, _swiglu(_modulate(x, mod, 0), ffn_w1[0], ffn_w3[0], ffn_w2[0]),
                            mod[:, 2], 0.5, ln_g[0], ln_b[0])
    xc = _post_norm_residual(xc, _swiglu(_modulate(xc, mod_c, 0), ffn_w1[0], ffn_w3[0], ffn_w2[0]),
                             mod_c[:, 2], 0.5, ln_g[0], ln_b[0])
    h, hc = _modulate(x, mod, 1), _modulate(xc, mod_c, 1)
    g_qkv, g_z, g_b, g_a, a_q, a_k, a_v, s_u = jnp.split(h @ w_in, list(IN_CUTS), axis=-1)
    c_qkv, c_z, c_b, c_a, c_q, c_k, c_v, c_u = jnp.split(hc @ w_in, list(IN_CUTS), axis=-1)
    o_gdn, oc_gdn = _gdn_group(g_qkv, g_z, g_b, g_a, c_qkv, c_z, c_b, c_a,
                               gdn_conv_w, gdn_a_log, gdn_dt_bias, gdn_norm_w, with_ctx_out)
    o_att, oc_att = _attention_group(a_q, a_k, a_v, c_q, c_k, c_v, q_norm_w, k_norm_w, cos, sin, with_ctx_out)
    o_s5, oc_s5 = _s5_group(s_u, c_u, s5_lam_re, s5_lam_im, s5_log_dt, s5_b_re, s5_b_im,
                            s5_c_re, s5_c_im, s5_d, glu_w, glu_b, with_ctx_out)
    y = jnp.concatenate([o_gdn, o_att, o_s5], axis=-1) @ w_out
    x = _post_norm_residual(x, y, mod[:, 5], 1.0, ln_g[1], ln_b[1])
    x = _post_norm_residual(x, _swiglu(_modulate(x, mod, 2), ffn_w1[1], ffn_w3[1], ffn_w2[1]),
                            mod[:, 8], 0.5, ln_g[2], ln_b[2])
    if not with_ctx_out:
        return x, None
    yc = jnp.concatenate([oc_gdn, oc_att, oc_s5], axis=-1) @ w_out
    xc = _post_norm_residual(xc, yc, mod_c[:, 5], 1.0, ln_g[1], ln_b[1])
    xc = _post_norm_residual(xc, _swiglu(_modulate(xc, mod_c, 2), ffn_w1[1], ffn_w3[1], ffn_w2[1]),
                             mod_c[:, 8], 0.5, ln_g[2], ln_b[2])
    return x, xc


def setup_inputs(seed: int = 0) -> dict:
    key = jax.random.key(seed)
    ks = jax.random.split(key, 32)
    f32 = jnp.float32
    L, D = DEPTH, D_MODEL

    def nrm(k, shape, s):
        return jax.random.normal(k, shape, f32) * s

    def unif(k, shape, lo, hi):
        return jax.random.uniform(k, shape, f32, lo, hi)

    dt_g = jnp.exp(unif(ks[15], (L, 2, GDN_HEADS), math.log(1e-3), math.log(1e-1)))
    return {
        'x': nrm(ks[0], (BATCH, SEQ, D), 1.0),
        'c': nrm(ks[1], (BATCH, D), 1.0),
        'ctx': nrm(ks[2], (BATCH, CTX_LEN, D), 1.0),
        'c_ctx': nrm(ks[3], (D,), 1.0),
        'w_ada': nrm(ks[4], (L, D, N_MOD * D), 0.5 * D ** -0.5),
        'b_ada': nrm(ks[5], (L, N_MOD * D), 0.02),
        'ln_g': 1.0 + nrm(ks[6], (L, 3, D), 0.05),
        'ln_b': nrm(ks[7], (L, 3, D), 0.02),
        'ffn_w1': nrm(ks[8], (L, 2, D, D_FF), D ** -0.5),
        'ffn_w3': nrm(ks[9], (L, 2, D, D_FF), D ** -0.5),
        'ffn_w2': nrm(ks[10], (L, 2, D_FF, D), D_FF ** -0.5 * BETA_INIT),
        'w_in': nrm(ks[11], (L, D, IN_COLS), D ** -0.5),
        'w_out': nrm(ks[12], (L, D_MIX, D), D_MIX ** -0.5 * BETA_INIT),
        'gdn_conv_w': nrm(ks[13], (L, CONV_K, GDN_QKV), CONV_K ** -0.5),
        'gdn_a_log': jnp.log(unif(ks[14], (L, 2, GDN_HEADS), 1.0, 16.0)),
        'gdn_dt_bias': dt_g + jnp.log(-jnp.expm1(-dt_g)),
        'gdn_norm_w': 1.0 + nrm(ks[16], (L, GDN_DV), 0.05),
        'q_norm_w': 1.0 + nrm(ks[17], (L, ATT_DH), 0.05),
        'k_norm_w': 1.0 + nrm(ks[18], (L, ATT_DH), 0.05),
        's5_lam_re': -0.5 * (1.0 + nrm(ks[19], (L, 2, S5_GROUPS, S5_P), 0.01)),
        's5_lam_im': jnp.pi * jnp.arange(S5_P, dtype=f32) + nrm(ks[20], (L, 2, S5_GROUPS, S5_P), 0.01),
        's5_log_dt': unif(ks[21], (L, 2, S5_GROUPS), math.log(1e-3), math.log(1e-1)),
        's5_b_re': nrm(ks[22], (L, 2, S5_GROUPS, S5_P, S5_GH), (2.0 * S5_GH) ** -0.5),
        's5_b_im': nrm(ks[23], (L, 2, S5_GROUPS, S5_P, S5_GH), (2.0 * S5_GH) ** -0.5),
        's5_c_re': nrm(ks[24], (L, 2, S5_GROUPS, S5_GH, S5_P), S5_P ** -0.5),
        's5_c_im': nrm(ks[25], (L, 2, S5_GROUPS, S5_GH, S5_P), S5_P ** -0.5),
        's5_d': nrm(ks[26], (L, S5_W), 1.0),
        'glu_w': nrm(ks[27], (L, S5_W, S5_W), S5_W ** -0.5),
        'glu_b': nrm(ks[28], (L, S5_W), 0.02),
    }


def reference(x, c, ctx, c_ctx, w_ada, b_ada, ln_g, ln_b, ffn_w1, ffn_w3, ffn_w2, w_in, w_out,
              gdn_conv_w, gdn_a_log, gdn_dt_bias, gdn_norm_w, q_norm_w, k_norm_w,
              s5_lam_re, s5_lam_im, s5_log_dt, s5_b_re, s5_b_im, s5_c_re, s5_c_im, s5_d, glu_w, glu_b):
    Bn, T, D = x.shape
    ROWS = T // GRID_W
    cos, sin = _axial_rope_tables(ROWS)
    sc = jax.nn.silu(c)
    scc = jax.nn.silu(c_ctx)
    xc = ctx
    for layer in range(DEPTH):
        mod = (sc @ w_ada[layer] + b_ada[layer]).reshape(Bn, N_MOD, 1, D)
        mod_c = (scc @ w_ada[layer] + b_ada[layer]).reshape(1, N_MOD, 1, D)
        x, xc = _trunk_layer(
            x, xc, mod, mod_c, cos, sin, ln_g[layer], ln_b[layer],
            ffn_w1[layer], ffn_w3[layer], ffn_w2[layer], w_in[layer], w_out[layer],
            gdn_conv_w[layer], gdn_a_log[layer], gdn_dt_bias[layer], gdn_norm_w[layer],
            q_norm_w[layer], k_norm_w[layer],
            s5_lam_re[layer], s5_lam_im[layer], s5_log_dt[layer], s5_b_re[layer], s5_b_im[layer],
            s5_c_re[layer], s5_c_im[layer], s5_d[layer], glu_w[layer], glu_b[layer],
            with_ctx_out=layer < DEPTH - 1)
    return x
```

```python
import functools
import math

import numpy as np
import jax
import jax.numpy as jnp
from jax import lax
from jax.experimental import pallas as pl
from jax.experimental.pallas import tpu as pltpu

F32 = jnp.float32
BF16 = jnp.bfloat16

D_MODEL = 1024
GRID_W = 64
GDN_HEADS = 6
GDN_DK = 64
GDN_PAIRS = GDN_HEADS // 2
GDN_QKV = 3 * GDN_HEADS * GDN_DK
GDN_W = GDN_HEADS * GDN_DK
CONV_K = 5
CHUNK = 64
ATT_HEADS = 6
ATT_KV_HEADS = 2
ATT_GROUP = ATT_HEADS // ATT_KV_HEADS
ATT_DH = 64
ATT_W = ATT_HEADS * ATT_DH
ROPE_THETA = 10000.0
ROPE_PAIRS = ATT_DH // 4
S5_GROUPS = 16
S5_GH = 16
S5_P = 64
S5_W = S5_GROUPS * S5_GH
S5_CHUNK = 16
N_MOD = 9
DEPTH = 4
ALPHA = (2.0 * DEPTH) ** 0.25
EPS = 1e-6
LANES = 128
HALF = LANES // 2

C_QKV = 0
C_Z = C_QKV + GDN_QKV
C_BA = C_Z + GDN_W
C_AQ = C_BA + GDN_PAIRS * LANES
C_AK = C_AQ + ATT_W
C_AV = C_AK + LANES
C_SU = C_AV + LANES
C_END = C_SU + S5_W

VMEM_LIMIT = 56 * 1024 * 1024


def _cparams(sem):
    return pltpu.CompilerParams(dimension_semantics=sem, vmem_limit_bytes=VMEM_LIMIT)


def _bf(x):
    return x.astype(BF16)


def _dot(a, b):
    return jnp.dot(a, b, preferred_element_type=F32)


def _dot_nt(a, b):
    return lax.dot_general(a, b, (((1,), (1,)), ((), ())), preferred_element_type=F32)


def _dot_tn(a, b):
    return lax.dot_general(a, b, (((0,), (0,)), ((), ())), preferred_element_type=F32)


def _split_dot(x, m):
    hi = _bf(x)
    lo = _bf(x - hi.astype(F32))
    return _dot(hi, m) + _dot(lo, m)


def _split3_dot_left(m, x):
    hi = _bf(x)
    r1 = x - hi.astype(F32)
    mid = _bf(r1)
    lo = _bf(r1 - mid.astype(F32))
    return _dot(m, hi) + _dot(m, mid) + _dot(m, lo)


def _silu(x):
    return x * jax.nn.sigmoid(x)


def _layer_norm(z, g, b):
    mu = jnp.mean(z, axis=-1, keepdims=True)
    zc = z - mu
    var = jnp.mean(zc * zc, axis=-1, keepdims=True)
    return zc * lax.rsqrt(var + EPS) * g + b


def _ctx_rows(tm, tc, tiles_per_batch):
    row0 = (pl.program_id(0) % tiles_per_batch) * tm
    return row0 + lax.broadcasted_iota(jnp.int32, (tm, 1), 0) < tc


def _row_mod(mod_c_ref, mod_b_ref, is_ctx, idx):
    return jnp.where(is_ctx, mod_c_ref[idx:idx + 1, :], mod_b_ref[idx:idx + 1, :])


def _mod_kernel(c_ref, w_ref, b_ref, o_ref):
    sc = _silu(c_ref[...])
    o_ref[...] = jnp.dot(sc, w_ref[...], preferred_element_type=F32,
                         precision=lax.Precision.HIGHEST) + b_ref[...]


def _modulation(c_all, w_ada, b_ada):
    L, D, NM = w_ada.shape
    R = c_all.shape[0]
    tn = 1152
    return pl.pallas_call(
        _mod_kernel,
        grid=(L, NM // tn),
        in_specs=[pl.BlockSpec((R, D), lambda l, j: (0, 0)),
                  pl.BlockSpec((None, D, tn), lambda l, j: (l, 0, j)),
                  pl.BlockSpec((None, 1, tn), lambda l, j: (l, 0, j))],
        out_specs=pl.BlockSpec((None, R, tn), lambda l, j: (l, 0, j)),
        out_shape=jax.ShapeDtypeStruct((L, R, NM), F32),
        compiler_params=_cparams(("parallel", "parallel")),
        name="adaln_mod",
    )(c_all, w_ada, b_ada.reshape(L, 1, NM))


def _ffn_kernel(x_ref, mod_c_ref, mod_b_ref, w1_ref, w3_ref, w2_ref, g_ref, b_ref, o_ref,
                hm_ref, acc_ref, *, sub, tm, tc, tpb, nf):
    f = pl.program_id(1)
    rm = functools.partial(_row_mod, mod_c_ref, mod_b_ref, _ctx_rows(tm, tc, tpb))

    @pl.when(f == 0)
    def _():
        hm_ref[...] = _bf(x_ref[...] * (1.0 + rm(3 * sub + 1)) + rm(3 * sub))
        acc_ref[...] = jnp.zeros_like(acc_ref)

    hm = hm_ref[...]
    h1 = _dot(hm, w1_ref[...])
    h3 = _dot(hm, w3_ref[...])
    acc_ref[...] += _dot(_bf(_silu(h1) * h3), w2_ref[...])

    @pl.when(f == nf - 1)
    def _():
        z = ALPHA * x_ref[...] + 0.5 * rm(3 * sub + 2) * acc_ref[...]
        o_ref[...] = _layer_norm(z, g_ref[...], b_ref[...])


def _ffn(xa, mod, w1, w3, w2, g, b, *, sub, B, TT, tc):
    NT, D = xa.shape
    F = w1.shape[1]
    tpb = 2
    tm = TT // tpb
    fc = 256
    nf = F // fc
    kern = functools.partial(_ffn_kernel, sub=sub, tm=tm, tc=tc, tpb=tpb, nf=nf)
    return pl.pallas_call(
        kern,
        grid=(NT // tm, nf),
        in_specs=[pl.BlockSpec((tm, D), lambda i, f: (i, 0)),
                  pl.BlockSpec((None, N_MOD, D), lambda i, f: (B, 0, 0)),
                  pl.BlockSpec((None, N_MOD, D), lambda i, f: (i // tpb, 0, 0)),
                  pl.BlockSpec((D, fc), lambda i, f: (0, f)),
                  pl.BlockSpec((D, fc), lambda i, f: (0, f)),
                  pl.BlockSpec((fc, D), lambda i, f: (f, 0)),
                  pl.BlockSpec((1, D), lambda i, f: (0, 0)),
                  pl.BlockSpec((1, D), lambda i, f: (0, 0))],
        out_specs=pl.BlockSpec((tm, D), lambda i, f: (i, 0)),
        out_shape=jax.ShapeDtypeStruct((NT, D), F32),
        scratch_shapes=[pltpu.VMEM((tm, D), BF16), pltpu.VMEM((tm, D), F32)],
        compiler_params=_cparams(("parallel", "arbitrary")),
        name=f"ffn{sub}",
    )(xa, mod, mod, w1, w3, w2, g, b)


def _seg_ones():
    r = lax.broadcasted_iota(jnp.int32, (LANES, LANES), 0) // HALF
    c = lax.broadcasted_iota(jnp.int32, (LANES, LANES), 1) // HALF
    return jnp.where(r == c, 1.0, 0.0).astype(BF16)


def _kin_kernel(x_ref, mod_c_ref, mod_b_ref, w_ref, qw_ref, kw_ref, cos_ref, sa_ref, sb_ref,
                gqkv_ref, gz_ref, gba_ref, aq_ref, ak_ref, av_ref, su_ref, *, tm, tc, tpb):
    rm = functools.partial(_row_mod, mod_c_ref, mod_b_ref, _ctx_rows(tm, tc, tpb))
    hm = _bf(x_ref[...] * (1.0 + rm(4)) + rm(3))
    r = _dot(hm, w_ref[...])
    gqkv_ref[...] = r[:, C_QKV:C_Z]
    gz_ref[...] = r[:, C_Z:C_BA]
    gba_ref[...] = r[:, C_BA:C_AQ]
    av_ref[...] = _bf(r[:, C_AV:C_SU])
    su_ref[...] = r[:, C_SU:C_END]

    seg = _seg_ones()
    cos, sa, sb = cos_ref[...], sa_ref[...], sb_ref[...]

    def norm_rope(blk, w_row):
        ss = _split_dot(blk * blk, seg)
        y = blk * lax.rsqrt(ss * (1.0 / ATT_DH) + EPS) * w_row
        return y * cos + pltpu.roll(y, LANES - ROPE_PAIRS, 1) * sa + pltpu.roll(y, ROPE_PAIRS, 1) * sb

    qw = qw_ref[...] * (ATT_DH ** -0.5)
    for j in range(ATT_W // LANES):
        aq_ref[:, j * LANES:(j + 1) * LANES] = _bf(
            norm_rope(r[:, C_AQ + j * LANES:C_AQ + (j + 1) * LANES], qw))
    ak_ref[...] = _bf(norm_rope(r[:, C_AK:C_AV], kw_ref[...]))


def _kin(xa, mod, w_in_p, qw, kw, cos, sa, sb, *, B, TT, tc):
    NT, D = xa.shape
    tpb = 4
    tm = TT // tpb
    kern = functools.partial(_kin_kernel, tm=tm, tc=tc, tpb=tpb)
    widths = [(GDN_QKV, F32), (GDN_W, F32), (GDN_PAIRS * LANES, F32), (ATT_W, BF16),
              (LANES, BF16), (LANES, BF16), (S5_W, F32)]
    return pl.pallas_call(
        kern,
        grid=(NT // tm,),
        in_specs=[pl.BlockSpec((tm, D), lambda i: (i, 0)),
                  pl.BlockSpec((None, N_MOD, D), lambda i: (B, 0, 0)),
                  pl.BlockSpec((None, N_MOD, D), lambda i: (i // tpb, 0, 0)),
                  pl.BlockSpec((D, C_END), lambda i: (0, 0)),
                  pl.BlockSpec((1, LANES), lambda i: (0, 0)),
                  pl.BlockSpec((1, LANES), lambda i: (0, 0)),
                  pl.BlockSpec((tm, LANES), lambda i: (i % tpb, 0)),
                  pl.BlockSpec((tm, LANES), lambda i: (i % tpb, 0)),
                  pl.BlockSpec((tm, LANES), lambda i: (i % tpb, 0))],
        out_specs=[pl.BlockSpec((tm, w), lambda i: (i, 0)) for w, _ in widths],
        out_shape=[jax.ShapeDtypeStruct((NT, w), dt) for w, dt in widths],
        compiler_params=_cparams(("parallel",)),
        name="mixer_in",
    )(xa, mod, mod, w_in_p, qw, kw, cos, sa, sb)


def _blk(x, lo):
    z = jnp.zeros_like(x)
    return jnp.concatenate([jnp.where(lo, x, z), jnp.where(lo, z, x)], axis=0)


def _gdn_kernel(qkv_in_ref, z_ref, ba_ref, cw_ref, hp_ref, nw_ref, o_ref,
                xp_ref, qkv_ref, bg_ref, of_ref, ob_ref, *, tc, t):
    tt = tc + t
    seg = _seg_ones()
    lane1 = lax.broadcasted_iota(jnp.int32, (1, LANES), 1)
    lo1 = lane1 < HALF

    rb = CHUNK * 2
    w3 = 3 * LANES
    for off, n in ((0, tc), (tc, t)):
        xp_ref[0:8, :] = jnp.zeros((8, w3), F32)
        xp_ref[8:8 + n, :] = qkv_in_ref[off:off + n, :]
        xp_ref[8 + n:16 + n, :] = jnp.zeros((8, w3), F32)
        for r0 in range(0, n, rb):
            acc = cw_ref[0:1, :] * xp_ref[6 + r0:6 + r0 + rb, :]
            for j in range(1, CONV_K):
                acc = acc + cw_ref[j:j + 1, :] * xp_ref[6 + j + r0:6 + j + r0 + rb, :]
            y = _silu(acc)
            q, k, v = y[:, 0:LANES], y[:, LANES:2 * LANES], y[:, 2 * LANES:w3]
            q = q * lax.rsqrt(_split_dot(q * q, seg) + EPS) * (GDN_DK ** -0.5)
            k = k * lax.rsqrt(_split_dot(k * k, seg) + EPS)
            qkv_ref[off + r0:off + r0 + rb, 0:LANES] = q
            qkv_ref[off + r0:off + r0 + rb, LANES:2 * LANES] = k
            qkv_ref[off + r0:off + r0 + rb, 2 * LANES:w3] = v

    ba = ba_ref[...]
    bg_ref[:, 0:LANES] = jax.nn.sigmoid(ba)
    sp_in = ba + hp_ref[1:2, :]
    softplus = jnp.maximum(sp_in, 0.0) + jnp.log1p(jnp.exp(-jnp.abs(sp_in)))
    bg_ref[:, LANES:2 * LANES] = hp_ref[0:1, :] * softplus

    rowi = lax.broadcasted_iota(jnp.int32, (CHUNK, LANES), 0)
    colj = lax.broadcasted_iota(jnp.int32, (CHUNK, LANES), 1) % HALF
    lo = lax.broadcasted_iota(jnp.int32, (CHUNK, LANES), 1) < HALF
    ti = lax.broadcasted_iota(jnp.int32, (CHUNK, CHUNK), 0)
    tj = lax.broadcasted_iota(jnp.int32, (CHUNK, CHUNK), 1)
    br = lax.broadcasted_iota(jnp.int32, (LANES, LANES), 0) // HALF
    bc = lax.broadcasted_iota(jnp.int32, (LANES, LANES), 1) // HALF
    blockmask = br == bc

    def chunk_dir(c, state, backward):
        r0 = pl.multiple_of(c * CHUNK, CHUNK)
        rows = pl.ds(r0, CHUNK)
        qn = qkv_ref[rows, 0:LANES]
        kn = qkv_ref[rows, LANES:2 * LANES]
        v = qkv_ref[rows, 2 * LANES:3 * LANES]
        bt = bg_ref[rows, 0:LANES]
        gc = bg_ref[rows, LANES:2 * LANES]
        c0 = 2 if backward else 0
        if backward:
            tri = jnp.where(tj >= ti, 1.0, 0.0).astype(BF16)
            incl, strict = colj >= rowi, colj > rowi
        else:
            tri = jnp.where(tj <= ti, 1.0, 0.0).astype(BF16)
            incl, strict = colj <= rowi, colj < rowi
        bx = jnp.where(lo, bt[:, c0:c0 + 1], bt[:, c0 + 1:c0 + 2])
        gcum = _split3_dot_left(tri, gc)
        gx = jnp.where(lo, gcum[:, 4 + c0:5 + c0], gcum[:, 5 + c0:6 + c0])
        eg = jnp.exp(gx)
        kb = kn * bx
        vb = v * bx
        kbg = kb * eg
        grow = jnp.sum(jnp.where(rowi == colj, gx, 0.0), axis=0, keepdims=True)
        decay = jnp.where(incl, jnp.exp(jnp.where(incl, gx - grow, 0.0)), 0.0)
        kst = _blk(_bf(kn), lo)
        low = jnp.where(strict, _dot_nt(_bf(kb), kst) * decay, 0.0)
        intra = jnp.where(incl, _dot_nt(_bf(qn), kst) * decay, 0.0)
        lb = _bf(low)
        p = _dot(lb, _blk(lb, lo))
        nn = -low
        n_sq = int(math.log2(CHUNK)) - 1
        for r in range(n_sq):
            pb = _bf(p)
            pblk = _blk(pb, lo)
            nn = nn + p + _dot(_bf(nn), pblk)
            if r < n_sq - 1:
                p = _dot(pb, pblk)
        nb = _bf(nn)
        u = vb + _dot(nb, _blk(_bf(vb), lo))
        w = kbg + _dot(nb, _blk(_bf(kbg), lo))
        sb16 = _bf(state)
        v_new = u - _dot(_bf(w), sb16)
        o = _dot(_bf(qn * eg), sb16) + _dot(_bf(intra), _blk(_bf(v_new), lo))
        g_last = gx[0:1, :] if backward else gx[CHUNK - 1:CHUNK, :]
        kdec = kn * jnp.exp(g_last - gx)
        upd = _dot_tn(_bf(kdec), _bf(v_new))
        state = state * jnp.exp(g_last) + jnp.where(blockmask, upd, 0.0)
        return o, state

    ncc = tc // CHUNK
    nct = tt // CHUNK

    def body(i, carry):
        sf, sb = carry
        o_f, sf = chunk_dir(i, sf, False)
        of_ref[pl.ds(pl.multiple_of(i * CHUNK, CHUNK), CHUNK), :] = o_f
        cb = jnp.where(i < ncc, ncc - 1 - i, nct - 1 - (i - ncc))
        o_b, sb = chunk_dir(cb, sb, True)
        ob_ref[pl.ds(pl.multiple_of(cb * CHUNK, CHUNK), CHUNK), :] = o_b
        return sf, sb

    zero = jnp.zeros((LANES, LANES), F32)
    lax.fori_loop(0, nct, body, (zero, zero))

    o = of_ref[...] + ob_ref[...]
    ss = _split_dot(o * o, seg)
    o = o * lax.rsqrt(ss * (1.0 / GDN_DK) + EPS) * nw_ref[...]
    o_ref[...] = _bf(o * _silu(z_ref[...]))


def _gdn(gqkv, gz, gba, conv_p, hp, nw, *, B, TT, tc):
    NT = gqkv.shape[0]
    t = TT - tc
    w3 = 3 * LANES
    kern = functools.partial(_gdn_kernel, tc=tc, t=t)
    return pl.pallas_call(
        kern,
        grid=(B, GDN_PAIRS),
        in_specs=[pl.BlockSpec((TT, w3), lambda b, p: (b, p)),
                  pl.BlockSpec((TT, LANES), lambda b, p: (b, p)),
                  pl.BlockSpec((TT, LANES), lambda b, p: (b, p)),
                  pl.BlockSpec((None, 8, w3), lambda b, p: (p, 0, 0)),
                  pl.BlockSpec((None, 8, LANES), lambda b, p: (p, 0, 0)),
                  pl.BlockSpec((1, LANES), lambda b, p: (0, 0))],
        out_specs=pl.BlockSpec((TT, LANES), lambda b, p: (b, p)),
        out_shape=jax.ShapeDtypeStruct((NT, GDN_W), BF16),
        scratch_shapes=[pltpu.VMEM((t + 16, w3), F32), pltpu.VMEM((TT, w3), F32),
                        pltpu.VMEM((TT, 2 * LANES), F32), pltpu.VMEM((TT, LANES), F32),
                        pltpu.VMEM((TT, LANES), F32)],
        compiler_params=_cparams(("parallel", "parallel")),
        name="gdn",
    )(gqkv, gz, gba, conv_p, hp, nw)


def _att_kernel(q_ref, k_ref, v_ref, o_ref, *, nk):
    lo = lax.broadcasted_iota(jnp.int32, (1, LANES), 1) < HALF
    k = k_ref[0:nk, :]
    v = v_ref[0:nk, :]
    for j in range(ATT_GROUP):
        qj = q_ref[:, j * LANES:(j + 1) * LANES]
        zq = jnp.zeros_like(qj)
        outs = []
        for half in range(2):
            qh = jnp.where(lo, qj, zq) if half == 0 else jnp.where(lo, zq, qj)
            s = _dot_nt(qh, k)
            m = jnp.max(s, axis=-1, keepdims=True)
            p = jnp.exp(s - m)
            l = jnp.sum(p, axis=-1, keepdims=True)
            outs.append(_dot(_bf(p), v) / l)
        o_ref[:, j * LANES:(j + 1) * LANES] = _bf(jnp.where(lo, outs[0], outs[1]))


def _attention(aq, ak, av, *, B, TT, tc):
    NT = aq.shape[0]
    tq = 256
    t = TT - tc
    kv_spec = pl.BlockSpec((TT, LANES), lambda b, i: (b, 0))
    nql = t // tq
    off = tc // tq
    o_lat = pl.pallas_call(
        functools.partial(_att_kernel, nk=TT),
        grid=(B, nql),
        in_specs=[pl.BlockSpec((tq, ATT_W), lambda b, i: (b * (TT // tq) + off + i, 0)), kv_spec, kv_spec],
        out_specs=pl.BlockSpec((tq, ATT_W), lambda b, i: (b * nql + i, 0)),
        out_shape=jax.ShapeDtypeStruct((B * t, ATT_W), BF16),
        compiler_params=_cparams(("parallel", "parallel")),
        name="att_latent",
    )(aq, ak, av)
    nqc = tc // tq
    o_ctx = pl.pallas_call(
        functools.partial(_att_kernel, nk=tc),
        grid=(B, nqc),
        in_specs=[pl.BlockSpec((tq, ATT_W), lambda b, i: (b * (TT // tq) + i, 0)), kv_spec, kv_spec],
        out_specs=pl.BlockSpec((tq, ATT_W), lambda b, i: (b * nqc + i, 0)),
        out_shape=jax.ShapeDtypeStruct((B * tc, ATT_W), BF16),
        compiler_params=_cparams(("parallel", "parallel")),
        name="att_ctx",
    )(aq, ak, av)
    return jnp.concatenate([o_ctx.reshape(B, tc, ATT_W), o_lat.reshape(B, t, ATT_W)], axis=1).reshape(NT, ATT_W)


def _s5_kernel(u_ref, tg_ref, bp_ref, cp_ref, la_ref, lb_ref, y_ref, sloc_ref, sin_ref, *, B, ncc, nct):
    ub = _bf(u_ref[...])
    sloc_ref[...] = _dot(ub, bp_ref[...])
    la = la_ref[...]
    lb = lb_ref[...]
    w2 = 2 * S5_P

    def step(c, s, d):
        rows = pl.ds(pl.multiple_of(c * B, B), B)
        sin_ref[rows, d * w2:(d + 1) * w2] = s
        return s * la[:, d * w2:(d + 1) * w2] + pltpu.roll(s, S5_P, 1) * lb[:, d * w2:(d + 1) * w2] \
            + sloc_ref[rows, d * w2:(d + 1) * w2]

    def body(i, carry):
        sf, sb = carry
        sf = step(i, sf, 0)
        cb = jnp.where(i < ncc, ncc - 1 - i, nct - 1 - (i - ncc))
        sb = step(cb, sb, 1)
        return sf, sb

    zero = jnp.zeros((B, w2), F32)
    lax.fori_loop(0, nct, body, (zero, zero))
    y_ref[...] = _dot(ub, tg_ref[...]) + _dot(_bf(sin_ref[...]), cp_ref[...])


def _s5(u2, tg, bp, cp, la, lb, *, B, ncc, nct):
    G, R, W = u2.shape
    kern = functools.partial(_s5_kernel, B=B, ncc=ncc, nct=nct)
    mat = pl.BlockSpec((None, W, W), lambda g: (g, 0, 0))
    vec = pl.BlockSpec((None, 1, W), lambda g: (g, 0, 0))
    return pl.pallas_call(
        kern,
        grid=(G,),
        in_specs=[pl.BlockSpec((None, R, W), lambda g: (g, 0, 0)), mat, mat, mat, vec, vec],
        out_specs=pl.BlockSpec((None, R, W), lambda g: (g, 0, 0)),
        out_shape=jax.ShapeDtypeStruct((G, R, W), F32),
        scratch_shapes=[pltpu.VMEM((R, W), F32), pltpu.VMEM((R, W), F32)],
        compiler_params=_cparams(("parallel",)),
        name="s5",
    )(u2, tg, bp, cp, la, lb)


def _s5_matrices(lam_re, lam_im, log_dt, b_re, b_im, c_re, c_im):
    Lc = S5_CHUNK
    lam = lax.complex(lam_re.astype(F32), lam_im.astype(F32))
    dt = jnp.exp(log_dt.astype(F32))[..., None]
    lam_dt = lam * dt
    lam_bar = jnp.exp(lam_dt)
    b_bar = ((lam_bar - 1.0) / lam)[..., None] * lax.complex(b_re.astype(F32), b_im.astype(F32))
    cc = lax.complex(c_re.astype(F32), c_im.astype(F32))
    ks = jnp.arange(Lc + 1, dtype=F32)
    pw = jnp.exp(lam_dt[..., None, :] * ks[:, None])
    kern = jnp.real(jnp.einsum('dghp,dgkp,dgpi->dgkhi', cc, pw[:, :, :Lc], b_bar))
    i_idx = jnp.arange(Lc)[:, None]
    j_idx = jnp.arange(Lc)[None, :]
    lag = i_idx - j_idx
    kf = kern[0][:, jnp.clip(lag, 0, Lc - 1)]
    kb = kern[1][:, jnp.clip(-lag, 0, Lc - 1)]
    tmat = (jnp.where((lag >= 0)[None, :, :, None, None], kf, 0.0)
            + jnp.where((lag <= 0)[None, :, :, None, None], kb, 0.0))
    tg = tmat.transpose(0, 2, 4, 1, 3).reshape(S5_GROUPS, Lc * S5_GH, Lc * S5_GH)
    pf = pw[0][:, Lc - 1 - jnp.arange(Lc)]
    pb = pw[1][:, jnp.arange(Lc)]
    sf = jnp.einsum('gjp,gph->gjhp', pf, b_bar[0])
    sb = jnp.einsum('gjp,gph->gjhp', pb, b_bar[1])
    bp = jnp.concatenate([jnp.real(sf), jnp.imag(sf), jnp.real(sb), jnp.imag(sb)], axis=-1)
    bp = bp.reshape(S5_GROUPS, Lc * S5_GH, 4 * S5_P)
    qf = jnp.einsum('ghp,gip->gpih', cc[0], pw[0][:, 1 + jnp.arange(Lc)])
    qb = jnp.einsum('ghp,gip->gpih', cc[1], pw[1][:, Lc - jnp.arange(Lc)])
    cp = jnp.concatenate([jnp.real(qf), -jnp.imag(qf), jnp.real(qb), -jnp.imag(qb)], axis=1)
    cp = cp.reshape(S5_GROUPS, 4 * S5_P, Lc * S5_GH)
    lc = pw[:, :, Lc]
    la = jnp.concatenate([jnp.real(lc[0]), jnp.real(lc[0]), jnp.real(lc[1]), jnp.real(lc[1])], axis=-1)
    lb = jnp.concatenate([-jnp.imag(lc[0]), jnp.imag(lc[0]), -jnp.imag(lc[1]), jnp.imag(lc[1])], axis=-1)
    return (tg.astype(BF16), bp.astype(BF16), cp.astype(BF16),
            la.reshape(S5_GROUPS, 1, 4 * S5_P), lb.reshape(S5_GROUPS, 1, 4 * S5_P))


def _gelu_tanh(x):
    return 0.5 * x * (1.0 + jnp.tanh(math.sqrt(2.0 / math.pi) * (x + 0.044715 * (x * x * x))))


def _kout_kernel(x_ref, mod_c_ref, mod_b_ref, go_ref, ao_ref, ys_ref, su_ref, d_ref, gw_ref, gb_ref,
                 wg_ref, wa_ref, ws_ref, g_ref, b_ref, o_ref, *, tm, tc, tpb):
    rm = functools.partial(_row_mod, mod_c_ref, mod_b_ref, _ctx_rows(tm, tc, tpb))
    zz = _gelu_tanh(ys_ref[...] + d_ref[...] * su_ref[...])
    s5o = zz * jax.nn.sigmoid(_dot(_bf(zz), gw_ref[...]) + gb_ref[...])
    y = _dot(go_ref[...], wg_ref[...]) + _dot(ao_ref[...], wa_ref[...]) + _dot(_bf(s5o), ws_ref[...])
    z = ALPHA * x_ref[...] + rm(5) * y
    o_ref[...] = _layer_norm(z, g_ref[...], b_ref[...])


def _kout(xa, mod, go, ao, ys, su, d, gw, gb, wg, wa, ws, g, b, *, B, TT, tc):
    NT, D = xa.shape
    tpb = 4
    tm = TT // tpb
    kern = functools.partial(_kout_kernel, tm=tm, tc=tc, tpb=tpb)
    row = lambda w: pl.BlockSpec((tm, w), lambda i: (i, 0))
    full = lambda a: pl.BlockSpec(a.shape, lambda i: (0,) * a.ndim)
    return pl.pallas_call(
        kern,
        grid=(NT // tm,),
        in_specs=[row(D),
                  pl.BlockSpec((None, N_MOD, D), lambda i: (B, 0, 0)),
                  pl.BlockSpec((None, N_MOD, D), lambda i: (i // tpb, 0, 0)),
                  row(GDN_W), row(ATT_W), row(S5_W), row(S5_W),
                  full(d), full(gw), full(gb), full(wg), full(wa), full(ws), full(g), full(b)],
        out_specs=row(D),
        out_shape=jax.ShapeDtypeStruct((NT, D), F32),
        compiler_params=_cparams(("parallel",)),
        name="mixer_out",
    )(xa, mod, mod, go, ao, ys, su, d, gw, gb, wg, wa, ws, g, b)


def _perm_in_cols():
    off_z = GDN_QKV
    off_b = off_z + GDN_W
    off_a = off_b + 2 * GDN_HEADS
    off_q = off_a + 2 * GDN_HEADS
    off_k = off_q + ATT_W
    off_v = off_k + ATT_KV_HEADS * ATT_DH
    off_u = off_v + ATT_KV_HEADS * ATT_DH
    cols = []
    for p in range(GDN_PAIRS):
        for part in range(3):
            base = part * GDN_W + 2 * p * GDN_DK
            cols += list(range(base, base + 2 * GDN_DK))
    cols += list(range(off_z, off_z + GDN_W))
    for p in range(GDN_PAIRS):
        blk = []
        for off in (off_b, off_a):
            for d in range(2):
                blk += [off + d * GDN_HEADS + 2 * p, off + d * GDN_HEADS + 2 * p + 1]
        cols += blk + [-1] * (LANES - len(blk))
    for j in range(ATT_GROUP):
        for h in (j, j + ATT_GROUP):
            cols += list(range(off_q + h * ATT_DH, off_q + (h + 1) * ATT_DH))
    cols += list(range(off_k, off_k + 2 * ATT_DH))
    cols += list(range(off_v, off_v + 2 * ATT_DH))
    cols += list(range(off_u, off_u + S5_W))
    return np.asarray(cols, dtype=np.int32)


def _gdn_pair_cols():
    cols = []
    for p in range(GDN_PAIRS):
        for part in range(3):
            base = part * GDN_W + 2 * p * GDN_DK
            cols += list(range(base, base + 2 * GDN_DK))
    return np.asarray(cols, dtype=np.int32)


def _att_out_rows():
    rows = []
    for j in range(ATT_GROUP):
        for h in (j, j + ATT_GROUP):
            rows += list(range(h * ATT_DH, (h + 1) * ATT_DH))
    return np.asarray(rows, dtype=np.int32)


def _rope_tables(tc, t):
    rows = t // GRID_W
    row = jnp.repeat(jnp.arange(rows), GRID_W)
    col = jnp.tile(jnp.arange(GRID_W), rows)
    inv_freq = ROPE_THETA ** (-jnp.arange(ROPE_PAIRS, dtype=F32) / ROPE_PAIRS)
    ang = jnp.stack([row, col], axis=-1).astype(F32)[..., None] * inv_freq
    cos, sin = jnp.cos(ang), jnp.sin(ang)
    zero = jnp.zeros_like(sin)
    cos_h = jnp.stack([cos, cos], axis=2).reshape(t, ATT_DH)
    sa_h = jnp.stack([-sin, zero], axis=2).reshape(t, ATT_DH)
    sb_h = jnp.stack([zero, sin], axis=2).reshape(t, ATT_DH)
    def full(tab, fill):
        tab = jnp.concatenate([jnp.full((tc, ATT_DH), fill, F32), tab], axis=0)
        return jnp.concatenate([tab, tab], axis=1)
    return full(cos_h, 1.0), full(sa_h, 0.0), full(sb_h, 0.0)


def kernel(x, c, ctx, c_ctx, w_ada, b_ada, ln_g, ln_b, ffn_w1, ffn_w3, ffn_w2, w_in, w_out, gdn_conv_w,
           gdn_a_log, gdn_dt_bias, gdn_norm_w, q_norm_w, k_norm_w, s5_lam_re, s5_lam_im, s5_log_dt,
           s5_b_re, s5_b_im, s5_c_re, s5_c_im, s5_d, glu_w, glu_b):
    B, T, D = x.shape
    tc = ctx.shape[1]
    TT = tc + T
    NT = B * TT
    L = w_ada.shape[0]
    kw = dict(B=B, TT=TT, tc=tc)

    R = ((B + 1 + 7) // 8) * 8
    c_all = jnp.concatenate([c, c_ctx[None, :], jnp.zeros((R - B - 1, D), F32)], axis=0)
    mod_all = _modulation(c_all, w_ada, b_ada).reshape(L, R, N_MOD, D)

    in_cols = _perm_in_cols()
    in_mask = jnp.asarray(in_cols >= 0, F32)
    pair_cols = _gdn_pair_cols()
    att_rows = _att_out_rows()
    cos, sa, sb = _rope_tables(tc, T)
    ncc5, nct5 = tc // S5_CHUNK, TT // S5_CHUNK

    xa = jnp.concatenate([ctx, x], axis=1).reshape(NT, D)
    for l in range(L):
        mod = mod_all[l]
        w1, w3, w2 = _bf(ffn_w1[l]), _bf(ffn_w3[l]), _bf(ffn_w2[l])
        g_l, b_l = ln_g[l], ln_b[l]
        xa = _ffn(xa, mod, w1[0], w3[0], w2[0], g_l[0:1], b_l[0:1], sub=0, **kw)

        w_in_p = _bf(w_in[l][:, np.maximum(in_cols, 0)] * in_mask[None, :])
        qw = jnp.tile(q_norm_w[l], 2)[None, :]
        kwt = jnp.tile(k_norm_w[l], 2)[None, :]
        gqkv, gz, gba, aq, ak, av, su = _kin(xa, mod, w_in_p, qw, kwt, cos, sa, sb, **kw)

        conv_p = jnp.zeros((8, GDN_QKV), F32).at[:CONV_K].set(gdn_conv_w[l][:, pair_cols])
        conv_p = conv_p.reshape(8, GDN_PAIRS, 3 * LANES).transpose(1, 0, 2)
        neg_a = -jnp.exp(gdn_a_log[l].astype(F32))
        dtb = gdn_dt_bias[l].astype(F32)
        hp = jnp.zeros((GDN_PAIRS, 8, LANES), F32)
        for p in range(GDN_PAIRS):
            for d in range(2):
                for hh in range(2):
                    hp = hp.at[p, 0, 4 + 2 * d + hh].set(neg_a[d, 2 * p + hh])
                    hp = hp.at[p, 1, 4 + 2 * d + hh].set(dtb[d, 2 * p + hh])
        nw = jnp.tile(gdn_norm_w[l], 2)[None, :]
        go = _gdn(gqkv, gz, gba, conv_p, hp, nw, **kw)

        ao = _attention(aq, ak, av, **kw)

        tg, bp, cp, la, lb = _s5_matrices(s5_lam_re[l], s5_lam_im[l], s5_log_dt[l], s5_b_re[l], s5_b_im[l],
                                          s5_c_re[l], s5_c_im[l])
        u2 = su.reshape(B, nct5, S5_CHUNK, S5_GROUPS, S5_GH).transpose(3, 1, 0, 2, 4)
        u2 = u2.reshape(S5_GROUPS, nct5 * B, S5_CHUNK * S5_GH)
        y2 = _s5(u2, tg, bp, cp, la, lb, B=B, ncc=ncc5, nct=nct5)
        ys = y2.reshape(S5_GROUPS, nct5, B, S5_CHUNK, S5_GH).transpose(2, 1, 3, 0, 4).reshape(NT, S5_W)

        wo = w_out[l]
        wg = _bf(wo[0:GDN_W])
        wa = _bf(wo[GDN_W:GDN_W + ATT_W][att_rows])
        ws = _bf(wo[GDN_W + ATT_W:])
        xa = _kout(xa, mod, go, ao, ys, su, s5_d[l][None, :], _bf(glu_w[l]), glu_b[l][None, :],
                   wg, wa, ws, g_l[1:2], b_l[1:2], **kw)
        xa = _ffn(xa, mod, w1[1], w3[1], w2[1], g_l[2:3], b_l[2:3], sub=2, **kw)
    return xa.reshape(B, TT, D)[:, tc:, :]
```

```python
import functools
import math

import numpy as np
import jax
import jax.numpy as jnp
from jax import lax
from jax.experimental import pallas as pl
from jax.experimental.pallas import tpu as pltpu

F32 = jnp.float32
BF16 = jnp.bfloat16

D_MODEL = 1024
GRID_W = 64
GDN_HEADS = 6
GDN_DK = 64
GDN_PAIRS = GDN_HEADS // 2
GDN_QKV = 3 * GDN_HEADS * GDN_DK
GDN_W = GDN_HEADS * GDN_DK
CONV_K = 5
CHUNK = 64
UNROLL_GDN = 4
ATT_HEADS = 6
ATT_KV_HEADS = 2
ATT_GROUP = ATT_HEADS // ATT_KV_HEADS
ATT_DH = 64
ATT_W = ATT_HEADS * ATT_DH
ROPE_THETA = 10000.0
ROPE_PAIRS = ATT_DH // 4
S5_GROUPS = 16
S5_GH = 16
S5_P = 64
S5_W = S5_GROUPS * S5_GH
S5_CHUNK = 16
N_MOD = 9
DEPTH = 4
ALPHA = (2.0 * DEPTH) ** 0.25
EPS = 1e-6
LANES = 128
HALF = LANES // 2

C_QKV = 0
C_Z = C_QKV + GDN_QKV
C_BA = C_Z + GDN_W
C_AQ = C_BA + GDN_PAIRS * LANES
C_AK = C_AQ + ATT_W
C_AV = C_AK + LANES
C_SU = C_AV + LANES
C_END = C_SU + S5_W

VMEM_LIMIT = 56 * 1024 * 1024


def _cparams(sem):
    return pltpu.CompilerParams(dimension_semantics=sem, vmem_limit_bytes=VMEM_LIMIT)


def _bf(x):
    return x.astype(BF16)


def _dot(a, b):
    return jnp.dot(a, b, preferred_element_type=F32)


def _dot_nt(a, b):
    return lax.dot_general(a, b, (((1,), (1,)), ((), ())), preferred_element_type=F32)


def _dot_tn(a, b):
    return lax.dot_general(a, b, (((0,), (0,)), ((), ())), preferred_element_type=F32)


def _split_dot(x, m):
    hi = _bf(x)
    lo = _bf(x - hi.astype(F32))
    return _dot(hi, m) + _dot(lo, m)


def _split3_dot_left(m, x):
    hi = _bf(x)
    r1 = x - hi.astype(F32)
    mid = _bf(r1)
    lo = _bf(r1 - mid.astype(F32))
    return _dot(m, hi) + _dot(m, mid) + _dot(m, lo)


def _silu(x):
    return x * jax.nn.sigmoid(x)


def _layer_norm(z, g, b):
    mu = jnp.mean(z, axis=-1, keepdims=True)
    zc = z - mu
    var = jnp.mean(zc * zc, axis=-1, keepdims=True)
    return zc * lax.rsqrt(var + EPS) * g + b


def _ctx_rows(tm, tc, tiles_per_batch):
    row0 = (pl.program_id(0) % tiles_per_batch) * tm
    return row0 + lax.broadcasted_iota(jnp.int32, (tm, 1), 0) < tc


def _row_mod(mod_c_ref, mod_b_ref, is_ctx, idx):
    return jnp.where(is_ctx, mod_c_ref[idx:idx + 1, :], mod_b_ref[idx:idx + 1, :])


def _mod_kernel(c_ref, w_ref, b_ref, o_ref):
    sc = _silu(c_ref[...])
    o_ref[...] = jnp.dot(sc, w_ref[...], preferred_element_type=F32,
                         precision=lax.Precision.HIGHEST) + b_ref[...]


def _modulation(c_all, w_ada, b_ada):
    L, D, NM = w_ada.shape
    R = c_all.shape[0]
    tn = 1152
    return pl.pallas_call(
        _mod_kernel,
        grid=(L, NM // tn),
        in_specs=[pl.BlockSpec((R, D), lambda l, j: (0, 0)),
                  pl.BlockSpec((None, D, tn), lambda l, j: (l, 0, j)),
                  pl.BlockSpec((None, 1, tn), lambda l, j: (l, 0, j))],
        out_specs=pl.BlockSpec((None, R, tn), lambda l, j: (l, 0, j)),
        out_shape=jax.ShapeDtypeStruct((L, R, NM), F32),
        compiler_params=_cparams(("parallel", "parallel")),
        name="adaln_mod",
    )(c_all, w_ada, b_ada.reshape(L, 1, NM))


def _ffn_kernel(x_ref, mod_c_ref, mod_b_ref, w1_ref, w3_ref, w2_ref, g_ref, b_ref, o_ref,
                hm_ref, acc_ref, *, sub, tm, tc, tpb, nf):
    f = pl.program_id(1)
    rm = functools.partial(_row_mod, mod_c_ref, mod_b_ref, _ctx_rows(tm, tc, tpb))

    @pl.when(f == 0)
    def _():
        hm_ref[...] = _bf(x_ref[...] * (1.0 + rm(3 * sub + 1)) + rm(3 * sub))
        acc_ref[...] = jnp.zeros_like(acc_ref)

    hm = hm_ref[...]
    h1 = _dot(hm, w1_ref[...])
    h3 = _dot(hm, w3_ref[...])
    acc_ref[...] += _dot(_bf(_silu(h1) * h3), w2_ref[...])

    @pl.when(f == nf - 1)
    def _():
        z = ALPHA * x_ref[...] + 0.5 * rm(3 * sub + 2) * acc_ref[...]
        o_ref[...] = _layer_norm(z, g_ref[...], b_ref[...])


def _ffn(xa, mod, w1, w3, w2, g, b, *, sub, B, TT, tc):
    NT, D = xa.shape
    F = w1.shape[1]
    tpb = 2
    tm = TT // tpb
    fc = 256
    nf = F // fc
    kern = functools.partial(_ffn_kernel, sub=sub, tm=tm, tc=tc, tpb=tpb, nf=nf)
    return pl.pallas_call(
        kern,
        grid=(NT // tm, nf),
        in_specs=[pl.BlockSpec((tm, D), lambda i, f: (i, 0)),
                  pl.BlockSpec((None, N_MOD, D), lambda i, f: (B, 0, 0)),
                  pl.BlockSpec((None, N_MOD, D), lambda i, f: (i // tpb, 0, 0)),
                  pl.BlockSpec((D, fc), lambda i, f: (0, f)),
                  pl.BlockSpec((D, fc), lambda i, f: (0, f)),
                  pl.BlockSpec((fc, D), lambda i, f: (f, 0)),
                  pl.BlockSpec((1, D), lambda i, f: (0, 0)),
                  pl.BlockSpec((1, D), lambda i, f: (0, 0))],
        out_specs=pl.BlockSpec((tm, D), lambda i, f: (i, 0)),
        out_shape=jax.ShapeDtypeStruct((NT, D), F32),
        scratch_shapes=[pltpu.VMEM((tm, D), BF16), pltpu.VMEM((tm, D), F32)],
        compiler_params=_cparams(("parallel", "arbitrary")),
        name=f"ffn{sub}",
    )(xa, mod, mod, w1, w3, w2, g, b)


def _seg_ones():
    r = lax.broadcasted_iota(jnp.int32, (LANES, LANES), 0) // HALF
    c = lax.broadcasted_iota(jnp.int32, (LANES, LANES), 1) // HALF
    return jnp.where(r == c, 1.0, 0.0).astype(BF16)


def _kin_kernel(x_ref, mod_c_ref, mod_b_ref, w_ref, qw_ref, kw_ref, cos_ref, sa_ref, sb_ref,
                gqkv_ref, gz_ref, gba_ref, aq_ref, ak_ref, av_ref, su_ref, *, tm, tc, tpb):
    rm = functools.partial(_row_mod, mod_c_ref, mod_b_ref, _ctx_rows(tm, tc, tpb))
    hm = _bf(x_ref[...] * (1.0 + rm(4)) + rm(3))
    r = _dot(hm, w_ref[...])
    gqkv_ref[...] = r[:, C_QKV:C_Z]
    gz_ref[...] = r[:, C_Z:C_BA]
    gba_ref[...] = r[:, C_BA:C_AQ]
    av_ref[...] = _bf(r[:, C_AV:C_SU])
    su_ref[...] = r[:, C_SU:C_END]

    seg = _seg_ones()
    cos, sa, sb = cos_ref[...], sa_ref[...], sb_ref[...]

    def norm_rope(blk, w_row):
        ss = _split_dot(blk * blk, seg)
        y = blk * lax.rsqrt(ss * (1.0 / ATT_DH) + EPS) * w_row
        return y * cos + pltpu.roll(y, LANES - ROPE_PAIRS, 1) * sa + pltpu.roll(y, ROPE_PAIRS, 1) * sb

    qw = qw_ref[...] * (ATT_DH ** -0.5)
    for j in range(ATT_W // LANES):
        aq_ref[:, j * LANES:(j + 1) * LANES] = _bf(
            norm_rope(r[:, C_AQ + j * LANES:C_AQ + (j + 1) * LANES], qw))
    ak_ref[...] = _bf(norm_rope(r[:, C_AK:C_AV], kw_ref[...]))


def _kin(xa, mod, w_in_p, qw, kw, cos, sa, sb, *, B, TT, tc):
    NT, D = xa.shape
    tpb = 4
    tm = TT // tpb
    kern = functools.partial(_kin_kernel, tm=tm, tc=tc, tpb=tpb)
    widths = [(GDN_QKV, F32), (GDN_W, F32), (GDN_PAIRS * LANES, F32), (ATT_W, BF16),
              (LANES, BF16), (LANES, BF16), (S5_W, F32)]
    return pl.pallas_call(
        kern,
        grid=(NT // tm,),
        in_specs=[pl.BlockSpec((tm, D), lambda i: (i, 0)),
                  pl.BlockSpec((None, N_MOD, D), lambda i: (B, 0, 0)),
                  pl.BlockSpec((None, N_MOD, D), lambda i: (i // tpb, 0, 0)),
                  pl.BlockSpec((D, C_END), lambda i: (0, 0)),
                  pl.BlockSpec((1, LANES), lambda i: (0, 0)),
                  pl.BlockSpec((1, LANES), lambda i: (0, 0)),
                  pl.BlockSpec((tm, LANES), lambda i: (i % tpb, 0)),
                  pl.BlockSpec((tm, LANES), lambda i: (i % tpb, 0)),
                  pl.BlockSpec((tm, LANES), lambda i: (i % tpb, 0))],
        out_specs=[pl.BlockSpec((tm, w), lambda i: (i, 0)) for w, _ in widths],
        out_shape=[jax.ShapeDtypeStruct((NT, w), dt) for w, dt in widths],
        compiler_params=_cparams(("parallel",)),
        name="mixer_in",
    )(xa, mod, mod, w_in_p, qw, kw, cos, sa, sb)


def _blk(x, lo):
    z = jnp.zeros_like(x)
    return jnp.concatenate([jnp.where(lo, x, z), jnp.where(lo, z, x)], axis=0)


def _gdn_kernel(qkv_in_ref, z_ref, ba_ref, cw_ref, hp_ref, nw_ref, o_ref,
                xp_ref, qkv_ref, bg_ref, of_ref, ob_ref,
                qg_ref, in_ref, w_ref, u_ref, k2_ref, r_ref, el_ref, s_ref, *, tc, t):
    tt = tc + t
    seg = _seg_ones()
    lane1 = lax.broadcasted_iota(jnp.int32, (1, LANES), 1)
    lo1 = lane1 < HALF

    rb = CHUNK * 2
    w3 = 3 * LANES
    for off, n in ((0, tc), (tc, t)):
        xp_ref[0:8, :] = jnp.zeros((8, w3), F32)
        xp_ref[8:8 + n, :] = qkv_in_ref[off:off + n, :]
        xp_ref[8 + n:16 + n, :] = jnp.zeros((8, w3), F32)
        for r0 in range(0, n, rb):
            acc = cw_ref[0:1, :] * xp_ref[6 + r0:6 + r0 + rb, :]
            for j in range(1, CONV_K):
                acc = acc + cw_ref[j:j + 1, :] * xp_ref[6 + j + r0:6 + j + r0 + rb, :]
            y = _silu(acc)
            q, k, v = y[:, 0:LANES], y[:, LANES:2 * LANES], y[:, 2 * LANES:w3]
            q = q * lax.rsqrt(_split_dot(q * q, seg) + EPS) * (GDN_DK ** -0.5)
            k = k * lax.rsqrt(_split_dot(k * k, seg) + EPS)
            qkv_ref[off + r0:off + r0 + rb, 0:LANES] = q
            qkv_ref[off + r0:off + r0 + rb, LANES:2 * LANES] = k
            qkv_ref[off + r0:off + r0 + rb, 2 * LANES:w3] = v

    ba = ba_ref[...]
    bg_ref[:, 0:LANES] = jax.nn.sigmoid(ba)
    sp_in = ba + hp_ref[1:2, :]
    softplus = jnp.maximum(sp_in, 0.0) + jnp.log1p(jnp.exp(-jnp.abs(sp_in)))
    bg_ref[:, LANES:2 * LANES] = hp_ref[0:1, :] * softplus

    rowi = lax.broadcasted_iota(jnp.int32, (CHUNK, LANES), 0)
    colj = lax.broadcasted_iota(jnp.int32, (CHUNK, LANES), 1) % HALF
    lo = lax.broadcasted_iota(jnp.int32, (CHUNK, LANES), 1) < HALF
    ti = lax.broadcasted_iota(jnp.int32, (CHUNK, CHUNK), 0)
    tj = lax.broadcasted_iota(jnp.int32, (CHUNK, CHUNK), 1)
    br = lax.broadcasted_iota(jnp.int32, (LANES, LANES), 0) // HALF
    bc = lax.broadcasted_iota(jnp.int32, (LANES, LANES), 1) // HALF
    blockmask = br == bc

    tri_dir = (jnp.where(tj <= ti, 1.0, 0.0).astype(BF16), jnp.where(tj >= ti, 1.0, 0.0).astype(BF16))
    incl_dir = (colj <= rowi, colj >= rowi)
    strict_dir = (colj < rowi, colj > rowi)

    def each(f, *lists):
        return [f(*a) for a in zip(*lists)]

    def prepare(chains):
        ds = [d for _, d in chains]
        rows = [pl.ds(pl.multiple_of(c * CHUNK, CHUNK), CHUNK) for c, _ in chains]
        srows = [pl.ds(pl.multiple_of(c * LANES, LANES), LANES) for c, _ in chains]
        qn = [qkv_ref[r, 0:LANES] for r in rows]
        kn = [qkv_ref[r, LANES:2 * LANES] for r in rows]
        v = [qkv_ref[r, 2 * LANES:3 * LANES] for r in rows]
        bx = [jnp.where(lo, bg_ref[r, 2 * d:2 * d + 1], bg_ref[r, 2 * d + 1:2 * d + 2]) for r, d in zip(rows, ds)]
        gcum = [_split3_dot_left(tri_dir[d], bg_ref[r, LANES:2 * LANES]) for r, d in zip(rows, ds)]
        gx = [jnp.where(lo, g[:, 4 + 2 * d:5 + 2 * d], g[:, 5 + 2 * d:6 + 2 * d]) for g, d in zip(gcum, ds)]
        eg = each(jnp.exp, gx)
        kb = each(jnp.multiply, kn, bx)
        vb = each(jnp.multiply, v, bx)
        kbg = each(jnp.multiply, kb, eg)
        grow = [jnp.sum(jnp.where(rowi == colj, g, 0.0), axis=0, keepdims=True) for g in gx]
        decay = [jnp.where(incl_dir[d], jnp.exp(jnp.where(incl_dir[d], g - gr, 0.0)), 0.0)
                 for g, gr, d in zip(gx, grow, ds)]
        kst = [_blk(_bf(k), lo) for k in kn]
        low = [jnp.where(strict_dir[d], _dot_nt(_bf(a), b) * dc, 0.0) for a, b, dc, d in zip(kb, kst, decay, ds)]
        intra = [jnp.where(incl_dir[d], _dot_nt(_bf(a), b) * dc, 0.0) for a, b, dc, d in zip(qn, kst, decay, ds)]
        for r, x in zip(rows, zip(ds, qn, eg, intra)):
            d, q_, e_, i_ = x
            qg_ref[d, r, :] = _bf(q_ * e_)
            in_ref[d, r, :] = _bf(i_)
        lb = each(_bf, low)
        p = [_dot(a, _blk(a, lo)) for a in lb]
        nn = [-a for a in low]
        n_sq = int(math.log2(CHUNK)) - 1
        for r_ in range(n_sq):
            pb = each(_bf, p)
            pblk = [_blk(a, lo) for a in pb]
            nn = [n_ + p_ + _dot(_bf(n_), pk) for n_, p_, pk in zip(nn, p, pblk)]
            if r_ < n_sq - 1:
                p = each(_dot, pb, pblk)
        nb = each(_bf, nn)
        u = [a + _dot(n_, _blk(_bf(a), lo)) for a, n_ in zip(vb, nb)]
        w = [a + _dot(n_, _blk(_bf(a), lo)) for a, n_ in zip(kbg, nb)]
        g_last = [g[0:1, :] if d == 1 else g[CHUNK - 1:CHUNK, :] for g, d in zip(gx, ds)]
        kdec = [_bf(k * jnp.exp(gl - g)) for k, gl, g in zip(kn, g_last, gx)]
        wb = each(_bf, w)
        k2 = each(_dot_tn, kdec, wb)
        rr = [_dot_tn(a, _bf(b)) for a, b in zip(kdec, u)]
        for i, (c, d) in enumerate(chains):
            w_ref[d, rows[i], :] = wb[i]
            u_ref[d, rows[i], :] = u[i]
            k2_ref[d, srows[i], :] = _bf(jnp.where(blockmask, k2[i], 0.0))
            r_ref[d, srows[i], :] = jnp.where(blockmask, rr[i], 0.0)
            el_ref[d, pl.ds(pl.multiple_of(c * 8, 8), 8), :] = jnp.broadcast_to(jnp.exp(g_last[i]), (8, LANES))

    ncc = tc // CHUNK
    nct = tt // CHUNK
    ng = UNROLL_GDN

    def group_chains(i):
        return [(i * ng + k, d) for k in range(ng) for d in range(2)]

    def body_a(i, carry):
        prepare(group_chains(i))
        return carry

    lax.fori_loop(0, nct // ng, body_a, 0)

    def advance(c, d, s):
        srows = pl.ds(pl.multiple_of(c * LANES, LANES), LANES)
        s16 = _bf(s)
        s_ref[d, srows, :] = s16
        e = el_ref[d, pl.ds(pl.multiple_of(c * 8, 8), 1), :]
        return s * e - _dot(k2_ref[d, srows, :], s16) + r_ref[d, srows, :]

    def body_b(i, carry):
        sf, sb = carry
        cb = jnp.where(i < ncc, ncc - 1 - i, nct - 1 - (i - ncc))
        return advance(i, 0, sf), advance(cb, 1, sb)

    zero = jnp.zeros((LANES, LANES), F32)
    lax.fori_loop(0, nct, body_b, (zero, zero))

    def emit(chains):
        rows = [pl.ds(pl.multiple_of(c * CHUNK, CHUNK), CHUNK) for c, _ in chains]
        s16 = [s_ref[d, pl.ds(pl.multiple_of(c * LANES, LANES), LANES), :] for c, d in chains]
        ws = [_dot(w_ref[d, r, :], s) for (_, d), r, s in zip(chains, rows, s16)]
        qs = [_dot(qg_ref[d, r, :], s) for (_, d), r, s in zip(chains, rows, s16)]
        v_new = [u_ref[d, r, :] - a for (_, d), r, a in zip(chains, rows, ws)]
        iv = [_dot(in_ref[d, r, :], _blk(_bf(vn), lo)) for (_, d), r, vn in zip(chains, rows, v_new)]
        for (_, d), r, a, b in zip(chains, rows, qs, iv):
            (ob_ref if d == 1 else of_ref)[r, :] = a + b

    def body_c(i, carry):
        emit(group_chains(i))
        return carry

    lax.fori_loop(0, nct // ng, body_c, 0)

    o = of_ref[...] + ob_ref[...]
    ss = _split_dot(o * o, seg)
    o = o * lax.rsqrt(ss * (1.0 / GDN_DK) + EPS) * nw_ref[...]
    o_ref[...] = _bf(o * _silu(z_ref[...]))


def _gdn(gqkv, gz, gba, conv_p, hp, nw, *, B, TT, tc):
    NT = gqkv.shape[0]
    t = TT - tc
    w3 = 3 * LANES
    nch = TT // CHUNK
    kern = functools.partial(_gdn_kernel, tc=tc, t=t)
    return pl.pallas_call(
        kern,
        grid=(B, GDN_PAIRS),
        in_specs=[pl.BlockSpec((TT, w3), lambda b, p: (b, p)),
                  pl.BlockSpec((TT, LANES), lambda b, p: (b, p)),
                  pl.BlockSpec((TT, LANES), lambda b, p: (b, p)),
                  pl.BlockSpec((None, 8, w3), lambda b, p: (p, 0, 0)),
                  pl.BlockSpec((None, 8, LANES), lambda b, p: (p, 0, 0)),
                  pl.BlockSpec((1, LANES), lambda b, p: (0, 0))],
        out_specs=pl.BlockSpec((TT, LANES), lambda b, p: (b, p)),
        out_shape=jax.ShapeDtypeStruct((NT, GDN_W), BF16),
        scratch_shapes=[pltpu.VMEM((t + 16, w3), F32), pltpu.VMEM((TT, w3), F32),
                        pltpu.VMEM((TT, 2 * LANES), F32), pltpu.VMEM((TT, LANES), F32),
                        pltpu.VMEM((TT, LANES), F32),
                        pltpu.VMEM((2, TT, LANES), BF16), pltpu.VMEM((2, TT, LANES), BF16),
                        pltpu.VMEM((2, TT, LANES), BF16), pltpu.VMEM((2, TT, LANES), F32),
                        pltpu.VMEM((2, nch * LANES, LANES), BF16), pltpu.VMEM((2, nch * LANES, LANES), F32),
                        pltpu.VMEM((2, nch * 8, LANES), F32), pltpu.VMEM((2, nch * LANES, LANES), BF16)],
        compiler_params=_cparams(("parallel", "parallel")),
        name="gdn",
    )(gqkv, gz, gba, conv_p, hp, nw)


def _att_tile(q_ref, k_ref, v_ref, o_ref, nk):
    lo = lax.broadcasted_iota(jnp.int32, (1, LANES), 1) < HALF
    k = k_ref[0:nk, :]
    v = v_ref[0:nk, :]
    for j in range(ATT_GROUP):
        qj = q_ref[:, j * LANES:(j + 1) * LANES]
        zq = jnp.zeros_like(qj)
        outs = []
        for half in range(2):
            qh = jnp.where(lo, qj, zq) if half == 0 else jnp.where(lo, zq, qj)
            s = _dot_nt(qh, k)
            m = jnp.max(s, axis=-1, keepdims=True)
            p = jnp.exp(s - m)
            l = jnp.sum(p, axis=-1, keepdims=True)
            outs.append(_dot(_bf(p), v) / l)
        o_ref[:, j * LANES:(j + 1) * LANES] = _bf(jnp.where(lo, outs[0], outs[1]))


def _att_kernel(q_ref, k_ref, v_ref, o_ref, *, n_ctx_tiles, tc, tt):
    is_ctx = pl.program_id(1) < n_ctx_tiles

    @pl.when(is_ctx)
    def _():
        _att_tile(q_ref, k_ref, v_ref, o_ref, tc)

    @pl.when(jnp.logical_not(is_ctx))
    def _():
        _att_tile(q_ref, k_ref, v_ref, o_ref, tt)


def _attention(aq, ak, av, *, B, TT, tc):
    NT = aq.shape[0]
    tq = 256
    nq = TT // tq
    kv_spec = pl.BlockSpec((TT, LANES), lambda b, i: (b, 0))
    q_spec = pl.BlockSpec((tq, ATT_W), lambda b, i: (b * nq + i, 0))
    return pl.pallas_call(
        functools.partial(_att_kernel, n_ctx_tiles=tc // tq, tc=tc, tt=TT),
        grid=(B, nq),
        in_specs=[q_spec, kv_spec, kv_spec],
        out_specs=q_spec,
        out_shape=jax.ShapeDtypeStruct((NT, ATT_W), BF16),
        compiler_params=_cparams(("parallel", "parallel")),
        name="attention",
    )(aq, ak, av)


S5_BB = 2
GPV = LANES // S5_GH


def _s5_kernel(u0_ref, u1_ref, tg_ref, bp_ref, cp_ref, la_ref, lb_ref, d_ref, gw_ref, gb_ref, o_ref,
               uf_ref, sloc_ref, sin_ref, y0_ref, y1_ref, *, bb, ncc, nct):
    lc = S5_CHUNK
    nr = bb * nct
    w2 = 2 * S5_P
    blk8 = lax.broadcasted_iota(jnp.int32, (1, LANES), 1) // S5_GH
    u_refs = (u0_ref, u1_ref)
    y_refs = (y0_ref, y1_ref)

    for g in range(S5_GROUPS):
        for v in range(lc // GPV):
            acc = None
            for k in range(GPV):
                t = v * GPV + k
                src = u_refs[g // GPV][pl.ds(t, nr, stride=lc), :]
                sh = ((k - g % GPV) * S5_GH) % LANES
                src = pltpu.roll(src, sh, 1) if sh else src
                acc = src if acc is None else jnp.where(blk8 == k, src, acc)
            uf_ref[g, :, v * LANES:(v + 1) * LANES] = acc
        sl = _dot(_bf(uf_ref[g]), bp_ref[g])
        sloc_ref[2 * g] = sl[:, 0:w2]
        sloc_ref[2 * g + 1] = sl[:, w2:2 * w2]

    def body(i, carry):
        cb = jnp.where(i < ncc, ncc - 1 - i, nct - 1 - (i - ncc))
        out = []
        for g in range(S5_GROUPS):
            for d in range(2):
                s = carry[2 * g + d]
                rows = pl.ds(i if d == 0 else cb, bb, stride=nct)
                lanes = slice(d * w2, (d + 1) * w2)
                sin_ref[2 * g + d, rows, :] = s
                out.append(s * la_ref[g, :, lanes] + pltpu.roll(s, S5_P, 1) * lb_ref[g, :, lanes]
                           + sloc_ref[2 * g + d, rows, :])
        return tuple(out)

    zero = jnp.zeros((bb, w2), F32)
    lax.fori_loop(0, nct, body, (zero,) * (2 * S5_GROUPS))

    for g in range(S5_GROUPS):
        uf_ref[g] = (_dot(_bf(uf_ref[g]), tg_ref[g])
                     + _dot(_bf(sin_ref[2 * g]), cp_ref[g, 0:w2, :])
                     + _dot(_bf(sin_ref[2 * g + 1]), cp_ref[g, w2:2 * w2, :]))

    for t in range(lc):
        for v in range(S5_GROUPS // GPV):
            acc = None
            for k in range(GPV):
                g = v * GPV + k
                src = uf_ref[g, :, (t // GPV) * LANES:(t // GPV + 1) * LANES]
                sh = ((k - t % GPV) * S5_GH) % LANES
                src = pltpu.roll(src, sh, 1) if sh else src
                acc = src if acc is None else jnp.where(blk8 == k, src, acc)
            y_refs[v][pl.ds(t, nr, stride=lc), :] = acc

    y = jnp.concatenate([y0_ref[...], y1_ref[...]], axis=1)
    u = jnp.concatenate([u0_ref[...], u1_ref[...]], axis=1)
    zz = _gelu_tanh(y + d_ref[...] * u)
    o_ref[...] = _bf(zz * jax.nn.sigmoid(_dot(_bf(zz), gw_ref[...]) + gb_ref[...]))


def _s5(su, tg, bp, cp, la, lb, d, gw, gb, *, B, TT, tc):
    NT, W = su.shape
    bb = S5_BB
    nct = TT // S5_CHUNK
    nr = bb * nct
    kern = functools.partial(_s5_kernel, bb=bb, ncc=tc // S5_CHUNK, nct=nct)
    full = lambda a: pl.BlockSpec(a.shape, lambda i: (0,) * a.ndim)
    return pl.pallas_call(
        kern,
        grid=(B // bb,),
        in_specs=[pl.BlockSpec((bb * TT, LANES), lambda i: (i, 0)),
                  pl.BlockSpec((bb * TT, LANES), lambda i: (i, 1)),
                  full(tg), full(bp), full(cp), full(la), full(lb), full(d), full(gw), full(gb)],
        out_specs=pl.BlockSpec((bb * TT, W), lambda i: (i, 0)),
        out_shape=jax.ShapeDtypeStruct((NT, W), BF16),
        scratch_shapes=[pltpu.VMEM((S5_GROUPS, nr, W), F32), pltpu.VMEM((2 * S5_GROUPS, nr, LANES), F32),
                        pltpu.VMEM((2 * S5_GROUPS, nr, LANES), F32),
                        pltpu.VMEM((bb * TT, LANES), F32), pltpu.VMEM((bb * TT, LANES), F32)],
        compiler_params=_cparams(("parallel",)),
        name="s5",
    )(su, su, tg, bp, cp, la, lb, d, gw, gb)


def _s5_matrices(lam_re, lam_im, log_dt, b_re, b_im, c_re, c_im):
    Lc = S5_CHUNK
    lam = lax.complex(lam_re.astype(F32), lam_im.astype(F32))
    dt = jnp.exp(log_dt.astype(F32))[..., None]
    lam_dt = lam * dt
    lam_bar = jnp.exp(lam_dt)
    b_bar = ((lam_bar - 1.0) / lam)[..., None] * lax.complex(b_re.astype(F32), b_im.astype(F32))
    cc = lax.complex(c_re.astype(F32), c_im.astype(F32))
    ks = jnp.arange(Lc + 1, dtype=F32)
    pw = jnp.exp(lam_dt[..., None, :] * ks[:, None])
    kern = jnp.real(jnp.einsum('dghp,dgkp,dgpi->dgkhi', cc, pw[:, :, :Lc], b_bar))
    i_idx = jnp.arange(Lc)[:, None]
    j_idx = jnp.arange(Lc)[None, :]
    lag = i_idx - j_idx
    kf = kern[0][:, jnp.clip(lag, 0, Lc - 1)]
    kb = kern[1][:, jnp.clip(-lag, 0, Lc - 1)]
    tmat = (jnp.where((lag >= 0)[None, :, :, None, None], kf, 0.0)
            + jnp.where((lag <= 0)[None, :, :, None, None], kb, 0.0))
    tg = tmat.transpose(0, 2, 4, 1, 3).reshape(S5_GROUPS, Lc * S5_GH, Lc * S5_GH)
    pf = pw[0][:, Lc - 1 - jnp.arange(Lc)]
    pb = pw[1][:, jnp.arange(Lc)]
    sf = jnp.einsum('gjp,gph->gjhp', pf, b_bar[0])
    sb = jnp.einsum('gjp,gph->gjhp', pb, b_bar[1])
    bp = jnp.concatenate([jnp.real(sf), jnp.imag(sf), jnp.real(sb), jnp.imag(sb)], axis=-1)
    bp = bp.reshape(S5_GROUPS, Lc * S5_GH, 4 * S5_P)
    qf = jnp.einsum('ghp,gip->gpih', cc[0], pw[0][:, 1 + jnp.arange(Lc)])
    qb = jnp.einsum('ghp,gip->gpih', cc[1], pw[1][:, Lc - jnp.arange(Lc)])
    cp = jnp.concatenate([jnp.real(qf), -jnp.imag(qf), jnp.real(qb), -jnp.imag(qb)], axis=1)
    cp = cp.reshape(S5_GROUPS, 4 * S5_P, Lc * S5_GH)
    lc = pw[:, :, Lc]
    la = jnp.concatenate([jnp.real(lc[0]), jnp.real(lc[0]), jnp.real(lc[1]), jnp.real(lc[1])], axis=-1)
    lb = jnp.concatenate([-jnp.imag(lc[0]), jnp.imag(lc[0]), -jnp.imag(lc[1]), jnp.imag(lc[1])], axis=-1)
    return (tg.astype(BF16), bp.astype(BF16), cp.astype(BF16),
            la.reshape(S5_GROUPS, 1, 4 * S5_P), lb.reshape(S5_GROUPS, 1, 4 * S5_P))


def _gelu_tanh(x):
    return 0.5 * x * (1.0 + jnp.tanh(math.sqrt(2.0 / math.pi) * (x + 0.044715 * (x * x * x))))


def _kout_kernel(x_ref, mod_c_ref, mod_b_ref, go_ref, ao_ref, so_ref,
                 wg_ref, wa_ref, ws_ref, g_ref, b_ref, o_ref, *, tm, tc, tpb):
    rm = functools.partial(_row_mod, mod_c_ref, mod_b_ref, _ctx_rows(tm, tc, tpb))
    y = _dot(go_ref[...], wg_ref[...]) + _dot(ao_ref[...], wa_ref[...]) + _dot(so_ref[...], ws_ref[...])
    z = ALPHA * x_ref[...] + rm(5) * y
    o_ref[...] = _layer_norm(z, g_ref[...], b_ref[...])


def _kout(xa, mod, go, ao, so, wg, wa, ws, g, b, *, B, TT, tc):
    NT, D = xa.shape
    tpb = 4
    tm = TT // tpb
    kern = functools.partial(_kout_kernel, tm=tm, tc=tc, tpb=tpb)
    row = lambda w: pl.BlockSpec((tm, w), lambda i: (i, 0))
    full = lambda a: pl.BlockSpec(a.shape, lambda i: (0,) * a.ndim)
    return pl.pallas_call(
        kern,
        grid=(NT // tm,),
        in_specs=[row(D),
                  pl.BlockSpec((None, N_MOD, D), lambda i: (B, 0, 0)),
                  pl.BlockSpec((None, N_MOD, D), lambda i: (i // tpb, 0, 0)),
                  row(GDN_W), row(ATT_W), row(S5_W),
                  full(wg), full(wa), full(ws), full(g), full(b)],
        out_specs=row(D),
        out_shape=jax.ShapeDtypeStruct((NT, D), F32),
        compiler_params=_cparams(("parallel",)),
        name="mixer_out",
    )(xa, mod, mod, go, ao, so, wg, wa, ws, g, b)


def _perm_in_cols():
    off_z = GDN_QKV
    off_b = off_z + GDN_W
    off_a = off_b + 2 * GDN_HEADS
    off_q = off_a + 2 * GDN_HEADS
    off_k = off_q + ATT_W
    off_v = off_k + ATT_KV_HEADS * ATT_DH
    off_u = off_v + ATT_KV_HEADS * ATT_DH
    cols = []
    for p in range(GDN_PAIRS):
        for part in range(3):
            base = part * GDN_W + 2 * p * GDN_DK
            cols += list(range(base, base + 2 * GDN_DK))
    cols += list(range(off_z, off_z + GDN_W))
    for p in range(GDN_PAIRS):
        blk = []
        for off in (off_b, off_a):
            for d in range(2):
                blk += [off + d * GDN_HEADS + 2 * p, off + d * GDN_HEADS + 2 * p + 1]
        cols += blk + [-1] * (LANES - len(blk))
    for j in range(ATT_GROUP):
        for h in (j, j + ATT_GROUP):
            cols += list(range(off_q + h * ATT_DH, off_q + (h + 1) * ATT_DH))
    cols += list(range(off_k, off_k + 2 * ATT_DH))
    cols += list(range(off_v, off_v + 2 * ATT_DH))
    cols += list(range(off_u, off_u + S5_W))
    return np.asarray(cols, dtype=np.int32)


def _gdn_pair_cols():
    cols = []
    for p in range(GDN_PAIRS):
        for part in range(3):
            base = part * GDN_W + 2 * p * GDN_DK
            cols += list(range(base, base + 2 * GDN_DK))
    return np.asarray(cols, dtype=np.int32)


def _att_out_rows():
    rows = []
    for j in range(ATT_GROUP):
        for h in (j, j + ATT_GROUP):
            rows += list(range(h * ATT_DH, (h + 1) * ATT_DH))
    return np.asarray(rows, dtype=np.int32)


def _take(w, idx, axis):
    idx = [int(i) for i in idx]
    parts = []
    i = 0
    while i < len(idx):
        j = i + 1
        if idx[i] < 0:
            while j < len(idx) and idx[j] < 0:
                j += 1
            shape = list(w.shape)
            shape[axis] = j - i
            parts.append(jnp.zeros(shape, w.dtype))
        else:
            while j < len(idx) and idx[j] == idx[j - 1] + 1:
                j += 1
            parts.append(lax.slice_in_dim(w, idx[i], idx[j - 1] + 1, axis=axis))
        i = j
    return jnp.concatenate(parts, axis=axis)


def _rope_tables(tc, t):
    rows = t // GRID_W
    row = jnp.repeat(jnp.arange(rows), GRID_W)
    col = jnp.tile(jnp.arange(GRID_W), rows)
    inv_freq = ROPE_THETA ** (-jnp.arange(ROPE_PAIRS, dtype=F32) / ROPE_PAIRS)
    ang = jnp.stack([row, col], axis=-1).astype(F32)[..., None] * inv_freq
    cos, sin = jnp.cos(ang), jnp.sin(ang)
    zero = jnp.zeros_like(sin)
    cos_h = jnp.stack([cos, cos], axis=2).reshape(t, ATT_DH)
    sa_h = jnp.stack([-sin, zero], axis=2).reshape(t, ATT_DH)
    sb_h = jnp.stack([zero, sin], axis=2).reshape(t, ATT_DH)
    def full(tab, fill):
        tab = jnp.concatenate([jnp.full((tc, ATT_DH), fill, F32), tab], axis=0)
        return jnp.concatenate([tab, tab], axis=1)
    return full(cos_h, 1.0), full(sa_h, 0.0), full(sb_h, 0.0)


def kernel(x, c, ctx, c_ctx, w_ada, b_ada, ln_g, ln_b, ffn_w1, ffn_w3, ffn_w2, w_in, w_out, gdn_conv_w,
           gdn_a_log, gdn_dt_bias, gdn_norm_w, q_norm_w, k_norm_w, s5_lam_re, s5_lam_im, s5_log_dt,
           s5_b_re, s5_b_im, s5_c_re, s5_c_im, s5_d, glu_w, glu_b):
    B, T, D = x.shape
    tc = ctx.shape[1]
    TT = tc + T
    NT = B * TT
    L = w_ada.shape[0]
    kw = dict(B=B, TT=TT, tc=tc)

    R = ((B + 1 + 7) // 8) * 8
    c_all = jnp.concatenate([c, c_ctx[None, :], jnp.zeros((R - B - 1, D), F32)], axis=0)
    mod_all = _modulation(c_all, w_ada, b_ada).reshape(L, R, N_MOD, D)

    in_cols = _perm_in_cols()
    pair_cols = _gdn_pair_cols()
    att_rows = _att_out_rows()
    cos, sa, sb = _rope_tables(tc, T)

    def head_params(v):
        v = v.astype(F32).reshape(2, GDN_PAIRS, 2).transpose(1, 0, 2).reshape(GDN_PAIRS, 4)
        return jnp.pad(v, ((0, 0), (4, LANES - 8)))

    xa = jnp.concatenate([ctx, x], axis=1).reshape(NT, D)
    for l in range(L):
        mod = mod_all[l]
        w1, w3, w2 = _bf(ffn_w1[l]), _bf(ffn_w3[l]), _bf(ffn_w2[l])
        g_l, b_l = ln_g[l], ln_b[l]
        xa = _ffn(xa, mod, w1[0], w3[0], w2[0], g_l[0:1], b_l[0:1], sub=0, **kw)

        w_in_p = _take(_bf(w_in[l]), in_cols, 1)
        qw = jnp.tile(q_norm_w[l], 2)[None, :]
        kwt = jnp.tile(k_norm_w[l], 2)[None, :]
        gqkv, gz, gba, aq, ak, av, su = _kin(xa, mod, w_in_p, qw, kwt, cos, sa, sb, **kw)

        conv_p = jnp.pad(_take(gdn_conv_w[l], pair_cols, 1), ((0, 8 - CONV_K), (0, 0)))
        conv_p = conv_p.reshape(8, GDN_PAIRS, 3 * LANES).transpose(1, 0, 2)
        hp = jnp.stack([head_params(-jnp.exp(gdn_a_log[l].astype(F32))), head_params(gdn_dt_bias[l])], axis=1)
        hp = jnp.pad(hp, ((0, 0), (0, 6), (0, 0)))
        nw = jnp.tile(gdn_norm_w[l], 2)[None, :]
        go = _gdn(gqkv, gz, gba, conv_p, hp, nw, **kw)

        ao = _attention(aq, ak, av, **kw)

        tg, bp, cp, la, lb = _s5_matrices(s5_lam_re[l], s5_lam_im[l], s5_log_dt[l], s5_b_re[l], s5_b_im[l],
                                          s5_c_re[l], s5_c_im[l])
        so = _s5(su, tg, bp, cp, la, lb, s5_d[l][None, :], _bf(glu_w[l]), glu_b[l][None, :], **kw)

        wo = _bf(w_out[l])
        wg = wo[0:GDN_W]
        wa = _take(wo[GDN_W:GDN_W + ATT_W], att_rows, 0)
        ws = wo[GDN_W + ATT_W:]
        xa = _kout(xa, mod, go, ao, so, wg, wa, ws, g_l[1:2], b_l[1:2], **kw)
        xa = _ffn(xa, mod, w1[1], w3[1], w2[1], g_l[2:3], b_l[2:3], sub=2, **kw)
    return xa.reshape(B, TT, D)[:, tc:, :]
```

```python
import functools
import math

import numpy as np
import jax
import jax.numpy as jnp
from jax import lax
from jax.experimental import pallas as pl
from jax.experimental.pallas import tpu as pltpu

F32 = jnp.float32
BF16 = jnp.bfloat16

D_MODEL = 1024
GRID_W = 64
GDN_HEADS = 6
GDN_DK = 64
GDN_PAIRS = GDN_HEADS // 2
GDN_QKV = 3 * GDN_HEADS * GDN_DK
GDN_W = GDN_HEADS * GDN_DK
CONV_K = 5
CHUNK = 64
UNROLL_GDN = 9
ATT_HEADS = 6
ATT_KV_HEADS = 2
ATT_GROUP = ATT_HEADS // ATT_KV_HEADS
ATT_DH = 64
ATT_W = ATT_HEADS * ATT_DH
ROPE_THETA = 10000.0
ROPE_PAIRS = ATT_DH // 4
S5_GROUPS = 16
S5_GH = 16
S5_P = 64
S5_W = S5_GROUPS * S5_GH
S5_CHUNK = 16
N_MOD = 9
DEPTH = 4
ALPHA = (2.0 * DEPTH) ** 0.25
EPS = 1e-6
LANES = 128
HALF = LANES // 2

C_QKV = 0
C_Z = C_QKV + GDN_QKV
C_BA = C_Z + GDN_W
C_AQ = C_BA + GDN_PAIRS * LANES
C_AK = C_AQ + ATT_W
C_AV = C_AK + LANES
C_SU = C_AV + LANES
C_END = C_SU + S5_W

VMEM_LIMIT = 56 * 1024 * 1024


def _cparams(sem):
    return pltpu.CompilerParams(dimension_semantics=sem, vmem_limit_bytes=VMEM_LIMIT)


def _bf(x):
    return x.astype(BF16)


def _dot(a, b):
    return jnp.dot(a, b, preferred_element_type=F32)


def _dot_nt(a, b):
    return lax.dot_general(a, b, (((1,), (1,)), ((), ())), preferred_element_type=F32)


def _dot_tn(a, b):
    return lax.dot_general(a, b, (((0,), (0,)), ((), ())), preferred_element_type=F32)


def _split_dot(x, m):
    hi = _bf(x)
    lo = _bf(x - hi.astype(F32))
    return _dot(hi, m) + _dot(lo, m)


def _split3_dot_left(m, x):
    hi = _bf(x)
    r1 = x - hi.astype(F32)
    mid = _bf(r1)
    lo = _bf(r1 - mid.astype(F32))
    return _dot(m, hi) + _dot(m, mid) + _dot(m, lo)


def _sigmoid(x):
    return 0.5 + 0.5 * jnp.tanh(0.5 * x)


def _silu(x):
    h = 0.5 * x
    return h + h * jnp.tanh(h)


def _layer_norm(z, g, b):
    mu = jnp.mean(z, axis=-1, keepdims=True)
    zc = z - mu
    var = jnp.mean(zc * zc, axis=-1, keepdims=True)
    return zc * lax.rsqrt(var + EPS) * g + b


def _ctx_rows(tm, tc, tiles_per_batch):
    row0 = (pl.program_id(0) % tiles_per_batch) * tm
    return row0 + lax.broadcasted_iota(jnp.int32, (tm, 1), 0) < tc


def _row_mod(mod_c_ref, mod_b_ref, is_ctx, idx):
    return jnp.where(is_ctx, mod_c_ref[idx:idx + 1, :], mod_b_ref[idx:idx + 1, :])


def _mod_kernel(c_ref, w_ref, b_ref, o_ref):
    sc = _silu(c_ref[...])
    o_ref[...] = jnp.dot(sc, w_ref[...], preferred_element_type=F32,
                         precision=lax.Precision.HIGHEST) + b_ref[...]


def _modulation(c_all, w_ada, b_ada):
    L, D, NM = w_ada.shape
    R = c_all.shape[0]
    tn = 1152
    return pl.pallas_call(
        _mod_kernel,
        grid=(L, NM // tn),
        in_specs=[pl.BlockSpec((R, D), lambda l, j: (0, 0)),
                  pl.BlockSpec((None, D, tn), lambda l, j: (l, 0, j)),
                  pl.BlockSpec((None, 1, tn), lambda l, j: (l, 0, j))],
        out_specs=pl.BlockSpec((None, R, tn), lambda l, j: (l, 0, j)),
        out_shape=jax.ShapeDtypeStruct((L, R, NM), F32),
        compiler_params=_cparams(("parallel", "parallel")),
        name="adaln_mod",
    )(c_all, w_ada, b_ada.reshape(L, 1, NM))


def _ffn_kernel(x_ref, mod_c_ref, mod_b_ref, w1_ref, w3_ref, w2_ref, g_ref, b_ref, o_ref,
                hm_ref, acc_ref, *, sub, tm, tc, tpb, nf):
    f = pl.program_id(1)
    rm = functools.partial(_row_mod, mod_c_ref, mod_b_ref, _ctx_rows(tm, tc, tpb))

    @pl.when(f == 0)
    def _():
        hm_ref[...] = _bf(x_ref[...] * (1.0 + rm(3 * sub + 1)) + rm(3 * sub))
        acc_ref[...] = jnp.zeros_like(acc_ref)

    hm = hm_ref[...]
    h1 = _dot(hm, w1_ref[...])
    h3 = _dot(hm, w3_ref[...])
    acc_ref[...] += _dot(_bf(_silu(h1) * h3), w2_ref[...])

    @pl.when(f == nf - 1)
    def _():
        z = ALPHA * x_ref[...] + 0.5 * rm(3 * sub + 2) * acc_ref[...]
        o_ref[...] = _layer_norm(z, g_ref[...], b_ref[...])


def _ffn(xa, mod, w1, w3, w2, g, b, *, sub, B, TT, tc):
    NT, D = xa.shape
    F = w1.shape[1]
    tpb = 2
    tm = TT // tpb
    fc = 256
    nf = F // fc
    kern = functools.partial(_ffn_kernel, sub=sub, tm=tm, tc=tc, tpb=tpb, nf=nf)
    return pl.pallas_call(
        kern,
        grid=(NT // tm, nf),
        in_specs=[pl.BlockSpec((tm, D), lambda i, f: (i, 0)),
                  pl.BlockSpec((None, N_MOD, D), lambda i, f: (B, 0, 0)),
                  pl.BlockSpec((None, N_MOD, D), lambda i, f: (i // tpb, 0, 0)),
                  pl.BlockSpec((D, fc), lambda i, f: (0, f)),
                  pl.BlockSpec((D, fc), lambda i, f: (0, f)),
                  pl.BlockSpec((fc, D), lambda i, f: (f, 0)),
                  pl.BlockSpec((1, D), lambda i, f: (0, 0)),
                  pl.BlockSpec((1, D), lambda i, f: (0, 0))],
        out_specs=pl.BlockSpec((tm, D), lambda i, f: (i, 0)),
        out_shape=jax.ShapeDtypeStruct((NT, D), F32),
        scratch_shapes=[pltpu.VMEM((tm, D), BF16), pltpu.VMEM((tm, D), F32)],
        compiler_params=_cparams(("parallel", "arbitrary")),
        name=f"ffn{sub}",
    )(xa, mod, mod, w1, w3, w2, g, b)


def _seg_ones():
    r = lax.broadcasted_iota(jnp.int32, (LANES, LANES), 0) // HALF
    c = lax.broadcasted_iota(jnp.int32, (LANES, LANES), 1) // HALF
    return jnp.where(r == c, 1.0, 0.0).astype(BF16)


def _kin_kernel(x_ref, mod_c_ref, mod_b_ref, w_ref, qw_ref, kw_ref, cos_ref, sa_ref, sb_ref,
                gqkv_ref, gz_ref, gba_ref, aq_ref, ak_ref, av_ref, su_ref, *, tm, tc, tpb):
    rm = functools.partial(_row_mod, mod_c_ref, mod_b_ref, _ctx_rows(tm, tc, tpb))
    hm = _bf(x_ref[...] * (1.0 + rm(4)) + rm(3))
    r = _dot(hm, w_ref[...])
    gqkv_ref[...] = r[:, C_QKV:C_Z]
    gz_ref[...] = r[:, C_Z:C_BA]
    gba_ref[...] = r[:, C_BA:C_AQ]
    av_ref[...] = _bf(r[:, C_AV:C_SU])
    su_ref[...] = r[:, C_SU:C_END]

    seg = _seg_ones()
    cos, sa, sb = cos_ref[...], sa_ref[...], sb_ref[...]

    def norm_rope(blk, w_row):
        ss = _split_dot(blk * blk, seg)
        y = blk * lax.rsqrt(ss * (1.0 / ATT_DH) + EPS) * w_row
        return y * cos + pltpu.roll(y, LANES - ROPE_PAIRS, 1) * sa + pltpu.roll(y, ROPE_PAIRS, 1) * sb

    qw = qw_ref[...] * (ATT_DH ** -0.5)
    for j in range(ATT_W // LANES):
        aq_ref[:, j * LANES:(j + 1) * LANES] = _bf(
            norm_rope(r[:, C_AQ + j * LANES:C_AQ + (j + 1) * LANES], qw))
    ak_ref[...] = _bf(norm_rope(r[:, C_AK:C_AV], kw_ref[...]))


def _kin(xa, mod, w_in_p, qw, kw, cos, sa, sb, *, B, TT, tc):
    NT, D = xa.shape
    tpb = 4
    tm = TT // tpb
    kern = functools.partial(_kin_kernel, tm=tm, tc=tc, tpb=tpb)
    widths = [(GDN_QKV, F32), (GDN_W, F32), (GDN_PAIRS * LANES, F32), (ATT_W, BF16),
              (LANES, BF16), (LANES, BF16), (S5_W, F32)]
    return pl.pallas_call(
        kern,
        grid=(NT // tm,),
        in_specs=[pl.BlockSpec((tm, D), lambda i: (i, 0)),
                  pl.BlockSpec((None, N_MOD, D), lambda i: (B, 0, 0)),
                  pl.BlockSpec((None, N_MOD, D), lambda i: (i // tpb, 0, 0)),
                  pl.BlockSpec((D, C_END), lambda i: (0, 0)),
                  pl.BlockSpec((1, LANES), lambda i: (0, 0)),
                  pl.BlockSpec((1, LANES), lambda i: (0, 0)),
                  pl.BlockSpec((tm, LANES), lambda i: (i % tpb, 0)),
                  pl.BlockSpec((tm, LANES), lambda i: (i % tpb, 0)),
                  pl.BlockSpec((tm, LANES), lambda i: (i % tpb, 0))],
        out_specs=[pl.BlockSpec((tm, w), lambda i: (i, 0)) for w, _ in widths],
        out_shape=[jax.ShapeDtypeStruct((NT, w), dt) for w, dt in widths],
        compiler_params=_cparams(("parallel",)),
        name="mixer_in",
    )(xa, mod, mod, w_in_p, qw, kw, cos, sa, sb)


def _blk(x, lo):
    z = jnp.zeros_like(x)
    return jnp.concatenate([jnp.where(lo, x, z), jnp.where(lo, z, x)], axis=0)


def _gdn_kernel(qkv_in_ref, z_ref, ba_ref, cw_ref, hp_ref, nw_ref, o_ref,
                xp_ref, qkv_ref, bg_ref, of_ref, ob_ref,
                qg_ref, in_ref, w_ref, u_ref, k2_ref, r_ref, el_ref, s_ref, *, tc, t):
    tt = tc + t
    seg = _seg_ones()

    pad = 8
    for off, n in ((0, tc), (tc, t)):
        for part in range(3):
            xp_ref[part, 0:pad, :] = jnp.zeros((pad, LANES), F32)
            xp_ref[part, pad:pad + n, :] = qkv_in_ref[off:off + n, part * LANES:(part + 1) * LANES]
            xp_ref[part, pad + n:2 * pad + n, :] = jnp.zeros((pad, LANES), F32)
        nb = n // 8
        rb = min(nb, 128)
        for r in range(8):
            for i0 in range(0, nb, rb):
                for part in range(3):
                    lanes = slice(part * LANES, (part + 1) * LANES)
                    base = pad - CONV_K // 2 + r + 8 * i0
                    acc = cw_ref[0:1, lanes] * xp_ref[part, pl.ds(base, rb, stride=8), :]
                    for j in range(1, CONV_K):
                        acc = acc + cw_ref[j:j + 1, lanes] * xp_ref[part, pl.ds(base + j, rb, stride=8), :]
                    y = _silu(acc)
                    if part == 0:
                        y = y * lax.rsqrt(_split_dot(y * y, seg) + EPS) * (GDN_DK ** -0.5)
                    elif part == 1:
                        y = y * lax.rsqrt(_split_dot(y * y, seg) + EPS)
                    qkv_ref[part, pl.ds(off + r + 8 * i0, rb, stride=8), :] = y

    ba = ba_ref[...]
    bg_ref[:, 0:LANES] = _sigmoid(ba)
    sp_in = ba + hp_ref[1:2, :]
    softplus = jnp.maximum(sp_in, 0.0) + jnp.log1p(jnp.exp(-jnp.abs(sp_in)))
    bg_ref[:, LANES:2 * LANES] = hp_ref[0:1, :] * softplus

    rowi = lax.broadcasted_iota(jnp.int32, (CHUNK, LANES), 0)
    colj = lax.broadcasted_iota(jnp.int32, (CHUNK, LANES), 1) % HALF
    lo = lax.broadcasted_iota(jnp.int32, (CHUNK, LANES), 1) < HALF
    ti = lax.broadcasted_iota(jnp.int32, (CHUNK, CHUNK), 0)
    tj = lax.broadcasted_iota(jnp.int32, (CHUNK, CHUNK), 1)
    br = lax.broadcasted_iota(jnp.int32, (LANES, LANES), 0) // HALF
    bc = lax.broadcasted_iota(jnp.int32, (LANES, LANES), 1) // HALF
    blockmask = br == bc

    tri_dir = (jnp.where(tj <= ti, 1.0, 0.0).astype(BF16), jnp.where(tj >= ti, 1.0, 0.0).astype(BF16))
    incl_dir = (colj <= rowi, colj >= rowi)
    strict_dir = (colj < rowi, colj > rowi)

    def each(f, *lists):
        return [f(*a) for a in zip(*lists)]

    def prepare(chains):
        ds = [d for _, d in chains]
        rows = [pl.ds(pl.multiple_of(c * CHUNK, CHUNK), CHUNK) for c, _ in chains]
        srows = [pl.ds(pl.multiple_of(c * LANES, LANES), LANES) for c, _ in chains]
        qn = [qkv_ref[0, r, :] for r in rows]
        kn = [qkv_ref[1, r, :] for r in rows]
        v = [qkv_ref[2, r, :] for r in rows]
        bx = [jnp.where(lo, bg_ref[r, 2 * d:2 * d + 1], bg_ref[r, 2 * d + 1:2 * d + 2]) for r, d in zip(rows, ds)]
        gcum = [_split3_dot_left(tri_dir[d], bg_ref[r, LANES:2 * LANES]) for r, d in zip(rows, ds)]
        gx = [jnp.where(lo, g[:, 4 + 2 * d:5 + 2 * d], g[:, 5 + 2 * d:6 + 2 * d]) for g, d in zip(gcum, ds)]
        eg = each(jnp.exp, gx)
        kb = each(jnp.multiply, kn, bx)
        vb = each(jnp.multiply, v, bx)
        kbg = each(jnp.multiply, kb, eg)
        grow = [jnp.sum(jnp.where(rowi == colj, g, 0.0), axis=0, keepdims=True) for g in gx]
        decay = [jnp.where(incl_dir[d], jnp.exp(jnp.where(incl_dir[d], g - gr, 0.0)), 0.0)
                 for g, gr, d in zip(gx, grow, ds)]
        kst = [_blk(_bf(k), lo) for k in kn]
        low = [jnp.where(strict_dir[d], _dot_nt(_bf(a), b) * dc, 0.0) for a, b, dc, d in zip(kb, kst, decay, ds)]
        intra = [jnp.where(incl_dir[d], _dot_nt(_bf(a), b) * dc, 0.0) for a, b, dc, d in zip(qn, kst, decay, ds)]
        for r, x in zip(rows, zip(ds, qn, eg, intra)):
            d, q_, e_, i_ = x
            qg_ref[d, r, :] = _bf(q_ * e_)
            in_ref[d, r, :] = _bf(i_)
        lb = each(_bf, low)
        p = [_dot(a, _blk(a, lo)) for a in lb]
        nn = [-a for a in low]
        n_sq = int(math.log2(CHUNK)) - 1
        for r_ in range(n_sq):
            pb = each(_bf, p)
            pblk = [_blk(a, lo) for a in pb]
            nn = [n_ + p_ + _dot(_bf(n_), pk) for n_, p_, pk in zip(nn, p, pblk)]
            if r_ < n_sq - 1:
                p = each(_dot, pb, pblk)
        nb = each(_bf, nn)
        u = [a + _dot(n_, _blk(_bf(a), lo)) for a, n_ in zip(vb, nb)]
        w = [a + _dot(n_, _blk(_bf(a), lo)) for a, n_ in zip(kbg, nb)]
        g_last = [g[0:1, :] if d == 1 else g[CHUNK - 1:CHUNK, :] for g, d in zip(gx, ds)]
        kdec = [_bf(k * jnp.exp(gl - g)) for k, gl, g in zip(kn, g_last, gx)]
        wb = each(_bf, w)
        k2 = each(_dot_tn, kdec, wb)
        rr = [_dot_tn(a, _bf(b)) for a, b in zip(kdec, u)]
        for i, (c, d) in enumerate(chains):
            w_ref[d, rows[i], :] = wb[i]
            u_ref[d, rows[i], :] = u[i]
            k2_ref[d, srows[i], :] = _bf(jnp.where(blockmask, k2[i], 0.0))
            r_ref[d, srows[i], :] = jnp.where(blockmask, rr[i], 0.0)
            el_ref[d, pl.ds(pl.multiple_of(c * 8, 8), 8), :] = jnp.broadcast_to(jnp.exp(g_last[i]), (8, LANES))

    ncc = tc // CHUNK
    nct = tt // CHUNK
    ng = max(g for g in range(1, UNROLL_GDN + 1) if nct % g == 0)

    def group_chains(i):
        return [(i * ng + k, d) for k in range(ng) for d in range(2)]

    def body_a(i, carry):
        prepare(group_chains(i))
        return carry

    lax.fori_loop(0, nct // ng, body_a, 0)

    def advance(c, d, s):
        srows = pl.ds(pl.multiple_of(c * LANES, LANES), LANES)
        s16 = _bf(s)
        s_ref[d, srows, :] = s16
        e = el_ref[d, pl.ds(pl.multiple_of(c * 8, 8), 1), :]
        return s * e - _dot(k2_ref[d, srows, :], s16) + r_ref[d, srows, :]

    def body_b(i, carry):
        sf, sb = carry
        cb = jnp.where(i < ncc, ncc - 1 - i, nct - 1 - (i - ncc))
        return advance(i, 0, sf), advance(cb, 1, sb)

    zero = jnp.zeros((LANES, LANES), F32)
    lax.fori_loop(0, nct, body_b, (zero, zero))

    def emit(chains):
        rows = [pl.ds(pl.multiple_of(c * CHUNK, CHUNK), CHUNK) for c, _ in chains]
        s16 = [s_ref[d, pl.ds(pl.multiple_of(c * LANES, LANES), LANES), :] for c, d in chains]
        ws = [_dot(w_ref[d, r, :], s) for (_, d), r, s in zip(chains, rows, s16)]
        qs = [_dot(qg_ref[d, r, :], s) for (_, d), r, s in zip(chains, rows, s16)]
        v_new = [u_ref[d, r, :] - a for (_, d), r, a in zip(chains, rows, ws)]
        iv = [_dot(in_ref[d, r, :], _blk(_bf(vn), lo)) for (_, d), r, vn in zip(chains, rows, v_new)]
        for (_, d), r, a, b in zip(chains, rows, qs, iv):
            (ob_ref if d == 1 else of_ref)[r, :] = a + b

    def body_c(i, carry):
        emit(group_chains(i))
        return carry

    lax.fori_loop(0, nct // ng, body_c, 0)

    o = of_ref[...] + ob_ref[...]
    ss = _split_dot(o * o, seg)
    o = o * lax.rsqrt(ss * (1.0 / GDN_DK) + EPS) * nw_ref[...]
    o_ref[...] = _bf(o * _silu(z_ref[...]))


def _gdn(gqkv, gz, gba, conv_p, hp, nw, *, B, TT, tc):
    NT = gqkv.shape[0]
    t = TT - tc
    w3 = 3 * LANES
    nch = TT // CHUNK
    kern = functools.partial(_gdn_kernel, tc=tc, t=t)
    return pl.pallas_call(
        kern,
        grid=(B, GDN_PAIRS),
        in_specs=[pl.BlockSpec((TT, w3), lambda b, p: (b, p)),
                  pl.BlockSpec((TT, LANES), lambda b, p: (b, p)),
                  pl.BlockSpec((TT, LANES), lambda b, p: (b, p)),
                  pl.BlockSpec((None, 8, w3), lambda b, p: (p, 0, 0)),
                  pl.BlockSpec((None, 8, LANES), lambda b, p: (p, 0, 0)),
                  pl.BlockSpec((1, LANES), lambda b, p: (0, 0))],
        out_specs=pl.BlockSpec((TT, LANES), lambda b, p: (b, p)),
        out_shape=jax.ShapeDtypeStruct((NT, GDN_W), BF16),
        scratch_shapes=[pltpu.VMEM((3, t + 16, LANES), F32), pltpu.VMEM((3, TT, LANES), F32),
                        pltpu.VMEM((TT, 2 * LANES), F32), pltpu.VMEM((TT, LANES), F32),
                        pltpu.VMEM((TT, LANES), F32),
                        pltpu.VMEM((2, TT, LANES), BF16), pltpu.VMEM((2, TT, LANES), BF16),
                        pltpu.VMEM((2, TT, LANES), BF16), pltpu.VMEM((2, TT, LANES), F32),
                        pltpu.VMEM((2, nch * LANES, LANES), BF16), pltpu.VMEM((2, nch * LANES, LANES), F32),
                        pltpu.VMEM((2, nch * 8, LANES), F32), pltpu.VMEM((2, nch * LANES, LANES), BF16)],
        compiler_params=_cparams(("parallel", "parallel")),
        name="gdn",
    )(gqkv, gz, gba, conv_p, hp, nw)


def _att_tile(q_ref, k_ref, v_ref, o_ref, nk):
    lo = lax.broadcasted_iota(jnp.int32, (1, LANES), 1) < HALF
    k = k_ref[0:nk, :]
    v = v_ref[0:nk, :]

    def scores(h):
        j, half = h // 2, h % 2
        qj = q_ref[:, j * LANES:(j + 1) * LANES]
        zq = jnp.zeros_like(qj)
        return _dot_nt(jnp.where(lo, qj, zq) if half == 0 else jnp.where(lo, zq, qj), k)

    outs = []
    s = scores(0)
    for h in range(ATT_HEADS):
        s_next = scores(h + 1) if h + 1 < ATT_HEADS else None
        m = jnp.max(s, axis=-1, keepdims=True)
        p = jnp.exp(s - m)
        l = jnp.sum(p, axis=-1, keepdims=True)
        outs.append(_dot(_bf(p), v) / l)
        s = s_next
    for j in range(ATT_GROUP):
        o_ref[:, j * LANES:(j + 1) * LANES] = _bf(jnp.where(lo, outs[2 * j], outs[2 * j + 1]))


def _att_kernel(q_ref, k_ref, v_ref, o_ref, *, n_ctx_tiles, tc, tt):
    is_ctx = pl.program_id(1) < n_ctx_tiles

    @pl.when(is_ctx)
    def _():
        _att_tile(q_ref, k_ref, v_ref, o_ref, tc)

    @pl.when(jnp.logical_not(is_ctx))
    def _():
        _att_tile(q_ref, k_ref, v_ref, o_ref, tt)


def _attention(aq, ak, av, *, B, TT, tc):
    NT = aq.shape[0]
    tq = 256
    nq = TT // tq
    kv_spec = pl.BlockSpec((TT, LANES), lambda b, i: (b, 0))
    q_spec = pl.BlockSpec((tq, ATT_W), lambda b, i: (b * nq + i, 0))
    return pl.pallas_call(
        functools.partial(_att_kernel, n_ctx_tiles=tc // tq, tc=tc, tt=TT),
        grid=(B, nq),
        in_specs=[q_spec, kv_spec, kv_spec],
        out_specs=q_spec,
        out_shape=jax.ShapeDtypeStruct((NT, ATT_W), BF16),
        compiler_params=_cparams(("parallel", "parallel")),
        name="attention",
    )(aq, ak, av)


S5_BB = 2
GPV = LANES // S5_GH


def _s5_kernel(u0_ref, u1_ref, tg_ref, bp_ref, cp_ref, la_ref, lb_ref, d_ref, gw_ref, gb_ref, o_ref,
               uf_ref, sloc_ref, sin_ref, y0_ref, y1_ref, *, bb, ncc, nct):
    lc = S5_CHUNK
    nr = bb * nct
    w2 = 2 * S5_P
    blk8 = lax.broadcasted_iota(jnp.int32, (1, LANES), 1) // S5_GH
    u_refs = (u0_ref, u1_ref)
    y_refs = (y0_ref, y1_ref)

    for g in range(S5_GROUPS):
        for v in range(lc // GPV):
            acc = None
            for k in range(GPV):
                t = v * GPV + k
                src = u_refs[g // GPV][pl.ds(t, nr, stride=lc), :]
                sh = ((k - g % GPV) * S5_GH) % LANES
                src = pltpu.roll(src, sh, 1) if sh else src
                acc = src if acc is None else jnp.where(blk8 == k, src, acc)
            uf_ref[g, :, v * LANES:(v + 1) * LANES] = acc
        sl = _dot(_bf(uf_ref[g]), bp_ref[g])
        sloc_ref[2 * g] = sl[:, 0:w2]
        sloc_ref[2 * g + 1] = sl[:, w2:2 * w2]

    def body(i, carry):
        cb = jnp.where(i < ncc, ncc - 1 - i, nct - 1 - (i - ncc))
        out = []
        for g in range(S5_GROUPS):
            for d in range(2):
                s = carry[2 * g + d]
                rows = pl.ds(i if d == 0 else cb, bb, stride=nct)
                lanes = slice(d * w2, (d + 1) * w2)
                sin_ref[2 * g + d, rows, :] = s
                out.append(s * la_ref[g, :, lanes] + pltpu.roll(s, S5_P, 1) * lb_ref[g, :, lanes]
                           + sloc_ref[2 * g + d, rows, :])
        return tuple(out)

    zero = jnp.zeros((bb, w2), F32)
    lax.fori_loop(0, nct, body, (zero,) * (2 * S5_GROUPS))

    for g in range(S5_GROUPS):
        uf_ref[g] = (_dot(_bf(uf_ref[g]), tg_ref[g])
                     + _dot(_bf(sin_ref[2 * g]), cp_ref[g, 0:w2, :])
                     + _dot(_bf(sin_ref[2 * g + 1]), cp_ref[g, w2:2 * w2, :]))

    for t in range(lc):
        for v in range(S5_GROUPS // GPV):
            acc = None
            for k in range(GPV):
                g = v * GPV + k
                src = uf_ref[g, :, (t // GPV) * LANES:(t // GPV + 1) * LANES]
                sh = ((k - t % GPV) * S5_GH) % LANES
                src = pltpu.roll(src, sh, 1) if sh else src
                acc = src if acc is None else jnp.where(blk8 == k, src, acc)
            y_refs[v][pl.ds(t, nr, stride=lc), :] = acc

    y = jnp.concatenate([y0_ref[...], y1_ref[...]], axis=1)
    u = jnp.concatenate([u0_ref[...], u1_ref[...]], axis=1)
    zz = _gelu_tanh(y + d_ref[...] * u)
    o_ref[...] = _bf(zz * _sigmoid(_dot(_bf(zz), gw_ref[...]) + gb_ref[...]))


def _s5(su, tg, bp, cp, la, lb, d, gw, gb, *, B, TT, tc):
    NT, W = su.shape
    bb = S5_BB
    nct = TT // S5_CHUNK
    nr = bb * nct
    kern = functools.partial(_s5_kernel, bb=bb, ncc=tc // S5_CHUNK, nct=nct)
    full = lambda a: pl.BlockSpec(a.shape, lambda i: (0,) * a.ndim)
    return pl.pallas_call(
        kern,
        grid=(B // bb,),
        in_specs=[pl.BlockSpec((bb * TT, LANES), lambda i: (i, 0)),
                  pl.BlockSpec((bb * TT, LANES), lambda i: (i, 1)),
                  full(tg), full(bp), full(cp), full(la), full(lb), full(d), full(gw), full(gb)],
        out_specs=pl.BlockSpec((bb * TT, W), lambda i: (i, 0)),
        out_shape=jax.ShapeDtypeStruct((NT, W), BF16),
        scratch_shapes=[pltpu.VMEM((S5_GROUPS, nr, W), F32), pltpu.VMEM((2 * S5_GROUPS, nr, LANES), F32),
                        pltpu.VMEM((2 * S5_GROUPS, nr, LANES), F32),
                        pltpu.VMEM((bb * TT, LANES), F32), pltpu.VMEM((bb * TT, LANES), F32)],
        compiler_params=_cparams(("parallel",)),
        name="s5",
    )(su, su, tg, bp, cp, la, lb, d, gw, gb)


def _s5_matrices(lam_re, lam_im, log_dt, b_re, b_im, c_re, c_im):
    Lc = S5_CHUNK
    lam = lax.complex(lam_re.astype(F32), lam_im.astype(F32))
    dt = jnp.exp(log_dt.astype(F32))[..., None]
    lam_dt = lam * dt
    lam_bar = jnp.exp(lam_dt)
    b_bar = ((lam_bar - 1.0) / lam)[..., None] * lax.complex(b_re.astype(F32), b_im.astype(F32))
    cc = lax.complex(c_re.astype(F32), c_im.astype(F32))
    ks = jnp.arange(Lc + 1, dtype=F32)
    pw = jnp.exp(lam_dt[..., None, :] * ks[:, None])
    kern = jnp.real(jnp.einsum('dghp,dgkp,dgpi->dgkhi', cc, pw[:, :, :Lc], b_bar))
    i_idx = jnp.arange(Lc)[:, None]
    j_idx = jnp.arange(Lc)[None, :]
    lag = i_idx - j_idx
    kf = kern[0][:, jnp.clip(lag, 0, Lc - 1)]
    kb = kern[1][:, jnp.clip(-lag, 0, Lc - 1)]
    tmat = (jnp.where((lag >= 0)[None, :, :, None, None], kf, 0.0)
            + jnp.where((lag <= 0)[None, :, :, None, None], kb, 0.0))
    tg = tmat.transpose(0, 2, 4, 1, 3).reshape(S5_GROUPS, Lc * S5_GH, Lc * S5_GH)
    pf = pw[0][:, Lc - 1 - jnp.arange(Lc)]
    pb = pw[1][:, jnp.arange(Lc)]
    sf = jnp.einsum('gjp,gph->gjhp', pf, b_bar[0])
    sb = jnp.einsum('gjp,gph->gjhp', pb, b_bar[1])
    bp = jnp.concatenate([jnp.real(sf), jnp.imag(sf), jnp.real(sb), jnp.imag(sb)], axis=-1)
    bp = bp.reshape(S5_GROUPS, Lc * S5_GH, 4 * S5_P)
    qf = jnp.einsum('ghp,gip->gpih', cc[0], pw[0][:, 1 + jnp.arange(Lc)])
    qb = jnp.einsum('ghp,gip->gpih', cc[1], pw[1][:, Lc - jnp.arange(Lc)])
    cp = jnp.concatenate([jnp.real(qf), -jnp.imag(qf), jnp.real(qb), -jnp.imag(qb)], axis=1)
    cp = cp.reshape(S5_GROUPS, 4 * S5_P, Lc * S5_GH)
    lc = pw[:, :, Lc]
    la = jnp.concatenate([jnp.real(lc[0]), jnp.real(lc[0]), jnp.real(lc[1]), jnp.real(lc[1])], axis=-1)
    lb = jnp.concatenate([-jnp.imag(lc[0]), jnp.imag(lc[0]), -jnp.imag(lc[1]), jnp.imag(lc[1])], axis=-1)
    return (tg.astype(BF16), bp.astype(BF16), cp.astype(BF16),
            la.reshape(S5_GROUPS, 1, 4 * S5_P), lb.reshape(S5_GROUPS, 1, 4 * S5_P))


def _gelu_tanh(x):
    return 0.5 * x * (1.0 + jnp.tanh(math.sqrt(2.0 / math.pi) * (x + 0.044715 * (x * x * x))))


def _kout_kernel(x_ref, mod_c_ref, mod_b_ref, go_ref, ao_ref, so_ref,
                 wg_ref, wa_ref, ws_ref, g_ref, b_ref, o_ref, *, tm, tc, tpb):
    rm = functools.partial(_row_mod, mod_c_ref, mod_b_ref, _ctx_rows(tm, tc, tpb))
    y = _dot(go_ref[...], wg_ref[...]) + _dot(ao_ref[...], wa_ref[...]) + _dot(so_ref[...], ws_ref[...])
    z = ALPHA * x_ref[...] + rm(5) * y
    o_ref[...] = _layer_norm(z, g_ref[...], b_ref[...])


def _kout(xa, mod, go, ao, so, wg, wa, ws, g, b, *, B, TT, tc):
    NT, D = xa.shape
    tpb = 4
    tm = TT // tpb
    kern = functools.partial(_kout_kernel, tm=tm, tc=tc, tpb=tpb)
    row = lambda w: pl.BlockSpec((tm, w), lambda i: (i, 0))
    full = lambda a: pl.BlockSpec(a.shape, lambda i: (0,) * a.ndim)
    return pl.pallas_call(
        kern,
        grid=(NT // tm,),
        in_specs=[row(D),
                  pl.BlockSpec((None, N_MOD, D), lambda i: (B, 0, 0)),
                  pl.BlockSpec((None, N_MOD, D), lambda i: (i // tpb, 0, 0)),
                  row(GDN_W), row(ATT_W), row(S5_W),
                  full(wg), full(wa), full(ws), full(g), full(b)],
        out_specs=row(D),
        out_shape=jax.ShapeDtypeStruct((NT, D), F32),
        compiler_params=_cparams(("parallel",)),
        name="mixer_out",
    )(xa, mod, mod, go, ao, so, wg, wa, ws, g, b)


def _perm_in_cols():
    off_z = GDN_QKV
    off_b = off_z + GDN_W
    off_a = off_b + 2 * GDN_HEADS
    off_q = off_a + 2 * GDN_HEADS
    off_k = off_q + ATT_W
    off_v = off_k + ATT_KV_HEADS * ATT_DH
    off_u = off_v + ATT_KV_HEADS * ATT_DH
    cols = []
    for p in range(GDN_PAIRS):
        for part in range(3):
            base = part * GDN_W + 2 * p * GDN_DK
            cols += list(range(base, base + 2 * GDN_DK))
    cols += list(range(off_z, off_z + GDN_W))
    for p in range(GDN_PAIRS):
        blk = []
        for off in (off_b, off_a):
            for d in range(2):
                blk += [off + d * GDN_HEADS + 2 * p, off + d * GDN_HEADS + 2 * p + 1]
        cols += blk + [-1] * (LANES - len(blk))
    for j in range(ATT_GROUP):
        for h in (j, j + ATT_GROUP):
            cols += list(range(off_q + h * ATT_DH, off_q + (h + 1) * ATT_DH))
    cols += list(range(off_k, off_k + 2 * ATT_DH))
    cols += list(range(off_v, off_v + 2 * ATT_DH))
    cols += list(range(off_u, off_u + S5_W))
    return np.asarray(cols, dtype=np.int32)


def _gdn_pair_cols():
    cols = []
    for p in range(GDN_PAIRS):
        for part in range(3):
            base = part * GDN_W + 2 * p * GDN_DK
            cols += list(range(base, base + 2 * GDN_DK))
    return np.asarray(cols, dtype=np.int32)


def _att_out_rows():
    rows = []
    for j in range(ATT_GROUP):
        for h in (j, j + ATT_GROUP):
            rows += list(range(h * ATT_DH, (h + 1) * ATT_DH))
    return np.asarray(rows, dtype=np.int32)


def _take(w, idx, axis):
    idx = [int(i) for i in idx]
    parts = []
    i = 0
    while i < len(idx):
        j = i + 1
        if idx[i] < 0:
            while j < len(idx) and idx[j] < 0:
                j += 1
            shape = list(w.shape)
            shape[axis] = j - i
            parts.append(jnp.zeros(shape, w.dtype))
        else:
            while j < len(idx) and idx[j] == idx[j - 1] + 1:
                j += 1
            parts.append(lax.slice_in_dim(w, idx[i], idx[j - 1] + 1, axis=axis))
        i = j
    return jnp.concatenate(parts, axis=axis)


def _rope_tables(tc, t):
    rows = t // GRID_W
    row = jnp.repeat(jnp.arange(rows), GRID_W)
    col = jnp.tile(jnp.arange(GRID_W), rows)
    inv_freq = ROPE_THETA ** (-jnp.arange(ROPE_PAIRS, dtype=F32) / ROPE_PAIRS)
    ang = jnp.stack([row, col], axis=-1).astype(F32)[..., None] * inv_freq
    cos, sin = jnp.cos(ang), jnp.sin(ang)
    zero = jnp.zeros_like(sin)
    cos_h = jnp.stack([cos, cos], axis=2).reshape(t, ATT_DH)
    sa_h = jnp.stack([-sin, zero], axis=2).reshape(t, ATT_DH)
    sb_h = jnp.stack([zero, sin], axis=2).reshape(t, ATT_DH)
    def full(tab, fill):
        tab = jnp.concatenate([jnp.full((tc, ATT_DH), fill, F32), tab], axis=0)
        return jnp.concatenate([tab, tab], axis=1)
    return full(cos_h, 1.0), full(sa_h, 0.0), full(sb_h, 0.0)


def kernel(x, c, ctx, c_ctx, w_ada, b_ada, ln_g, ln_b, ffn_w1, ffn_w3, ffn_w2, w_in, w_out, gdn_conv_w,
           gdn_a_log, gdn_dt_bias, gdn_norm_w, q_norm_w, k_norm_w, s5_lam_re, s5_lam_im, s5_log_dt,
           s5_b_re, s5_b_im, s5_c_re, s5_c_im, s5_d, glu_w, glu_b):
    B, T, D = x.shape
    tc = ctx.shape[1]
    TT = tc + T
    NT = B * TT
    L = w_ada.shape[0]
    kw = dict(B=B, TT=TT, tc=tc)

    R = ((B + 1 + 7) // 8) * 8
    c_all = jnp.concatenate([c, c_ctx[None, :], jnp.zeros((R - B - 1, D), F32)], axis=0)
    mod_all = _modulation(c_all, w_ada, b_ada).reshape(L, R, N_MOD, D)

    in_cols = _perm_in_cols()
    pair_cols = _gdn_pair_cols()
    att_rows = _att_out_rows()
    cos, sa, sb = _rope_tables(tc, T)

    def head_params(v):
        v = v.astype(F32).reshape(2, GDN_PAIRS, 2).transpose(1, 0, 2).reshape(GDN_PAIRS, 4)
        return jnp.pad(v, ((0, 0), (4, LANES - 8)))

    xa = jnp.concatenate([ctx, x], axis=1).reshape(NT, D)
    for l in range(L):
        mod = mod_all[l]
        w1, w3, w2 = _bf(ffn_w1[l]), _bf(ffn_w3[l]), _bf(ffn_w2[l])
        g_l, b_l = ln_g[l], ln_b[l]
        xa = _ffn(xa, mod, w1[0], w3[0], w2[0], g_l[0:1], b_l[0:1], sub=0, **kw)

        w_in_p = _take(_bf(w_in[l]), in_cols, 1)
        qw = jnp.tile(q_norm_w[l], 2)[None, :]
        kwt = jnp.tile(k_norm_w[l], 2)[None, :]
        gqkv, gz, gba, aq, ak, av, su = _kin(xa, mod, w_in_p, qw, kwt, cos, sa, sb, **kw)

        conv_p = jnp.pad(_take(gdn_conv_w[l], pair_cols, 1), ((0, 8 - CONV_K), (0, 0)))
        conv_p = conv_p.reshape(8, GDN_PAIRS, 3 * LANES).transpose(1, 0, 2)
        hp = jnp.stack([head_params(-jnp.exp(gdn_a_log[l].astype(F32))), head_params(gdn_dt_bias[l])], axis=1)
        hp = jnp.pad(hp, ((0, 0), (0, 6), (0, 0)))
        nw = jnp.tile(gdn_norm_w[l], 2)[None, :]
        go = _gdn(gqkv, gz, gba, conv_p, hp, nw, **kw)

        ao = _attention(aq, ak, av, **kw)

        tg, bp, cp, la, lb = _s5_matrices(s5_lam_re[l], s5_lam_im[l], s5_log_dt[l], s5_b_re[l], s5_b_im[l],
                                          s5_c_re[l], s5_c_im[l])
        so = _s5(su, tg, bp, cp, la, lb, s5_d[l][None, :], _bf(glu_w[l]), glu_b[l][None, :], **kw)

        wo = _bf(w_out[l])
        wg = wo[0:GDN_W]
        wa = _take(wo[GDN_W:GDN_W + ATT_W], att_rows, 0)
        ws = wo[GDN_W + ATT_W:]
        xa = _kout(xa, mod, go, ao, so, wg, wa, ws, g_l[1:2], b_l[1:2], **kw)
        xa = _ffn(xa, mod, w1[1], w3[1], w2[1], g_l[2:3], b_l[2:3], sub=2, **kw)
    return xa.reshape(B, TT, D)[:, tc:, :]
```

```python
import functools
import math

import numpy as np
import jax
import jax.numpy as jnp
from jax import lax
from jax.experimental import pallas as pl
from jax.experimental.pallas import tpu as pltpu

F32 = jnp.float32
BF16 = jnp.bfloat16

D_MODEL = 1024
GRID_W = 64
GDN_HEADS = 6
GDN_DK = 64
GDN_PAIRS = GDN_HEADS // 2
GDN_QKV = 3 * GDN_HEADS * GDN_DK
GDN_W = GDN_HEADS * GDN_DK
CONV_K = 5
CHUNK = 64
UNROLL_GDN = 9
ATT_HEADS = 6
ATT_KV_HEADS = 2
ATT_GROUP = ATT_HEADS // ATT_KV_HEADS
ATT_DH = 64
ATT_W = ATT_HEADS * ATT_DH
ROPE_THETA = 10000.0
ROPE_PAIRS = ATT_DH // 4
S5_GROUPS = 16
S5_GH = 16
S5_P = 64
S5_W = S5_GROUPS * S5_GH
S5_CHUNK = 16
N_MOD = 9
DEPTH = 4
ALPHA = (2.0 * DEPTH) ** 0.25
EPS = 1e-6
LANES = 128
HALF = LANES // 2

C_QKV = 0
C_Z = C_QKV + GDN_QKV
C_BA = C_Z + GDN_W
C_AQ = C_BA + GDN_PAIRS * LANES
C_AK = C_AQ + ATT_W
C_AV = C_AK + LANES
C_SU = C_AV + LANES
C_END = C_SU + S5_W

FFN_FINISH_STEPS = 4
VMEM_LIMIT = 56 * 1024 * 1024


def _cparams(sem):
    return pltpu.CompilerParams(dimension_semantics=sem, vmem_limit_bytes=VMEM_LIMIT)


def _bf(x):
    return x.astype(BF16)


def _dot(a, b):
    return jnp.dot(a, b, preferred_element_type=F32)


def _dot_nt(a, b):
    return lax.dot_general(a, b, (((1,), (1,)), ((), ())), preferred_element_type=F32)


def _dot_tn(a, b):
    return lax.dot_general(a, b, (((0,), (0,)), ((), ())), preferred_element_type=F32)


def _split_dot(x, m):
    hi = _bf(x)
    lo = _bf(x - hi.astype(F32))
    return _dot(hi, m) + _dot(lo, m)


def _split3_dot_left(m, x):
    hi = _bf(x)
    r1 = x - hi.astype(F32)
    mid = _bf(r1)
    lo = _bf(r1 - mid.astype(F32))
    return _dot(m, hi) + _dot(m, mid) + _dot(m, lo)


def _sigmoid(x):
    return 0.5 + 0.5 * jnp.tanh(0.5 * x)


def _silu(x):
    h = 0.5 * x
    return h + h * jnp.tanh(h)


def _layer_norm(z, g, b):
    mu = jnp.mean(z, axis=-1, keepdims=True)
    zc = z - mu
    var = jnp.mean(zc * zc, axis=-1, keepdims=True)
    return zc * lax.rsqrt(var + EPS) * g + b


def _ctx_rows(tm, tc, tiles_per_batch):
    row0 = (pl.program_id(0) % tiles_per_batch) * tm
    return row0 + lax.broadcasted_iota(jnp.int32, (tm, 1), 0) < tc


def _row_mod(mod_c_ref, mod_b_ref, is_ctx, idx):
    return jnp.where(is_ctx, mod_c_ref[idx:idx + 1, :], mod_b_ref[idx:idx + 1, :])


def _mod_kernel(c_ref, w_ref, b_ref, o_ref):
    sc = _silu(c_ref[...])
    o_ref[...] = jnp.dot(sc, w_ref[...], preferred_element_type=F32,
                         precision=lax.Precision.HIGHEST) + b_ref[...]


def _modulation(c_all, w_ada, b_ada):
    L, D, NM = w_ada.shape
    R = c_all.shape[0]
    tn = 1152
    return pl.pallas_call(
        _mod_kernel,
        grid=(L, NM // tn),
        in_specs=[pl.BlockSpec((R, D), lambda l, j: (0, 0)),
                  pl.BlockSpec((None, D, tn), lambda l, j: (l, 0, j)),
                  pl.BlockSpec((None, 1, tn), lambda l, j: (l, 0, j))],
        out_specs=pl.BlockSpec((None, R, tn), lambda l, j: (l, 0, j)),
        out_shape=jax.ShapeDtypeStruct((L, R, NM), F32),
        compiler_params=_cparams(("parallel", "parallel")),
        name="adaln_mod",
    )(c_all, w_ada, b_ada.reshape(L, 1, NM))


def _ffn_kernel(x_ref, xp_ref, mod_c_ref, mod_b_ref, modp_b_ref, w1_ref, w3_ref, w2_ref, g_ref, b_ref, o_ref,
                hm_ref, acc_ref, *, sub, tm, tc, tpb, nf, nt):
    i = pl.program_id(0)
    f = pl.program_id(1)
    slot = i % 2
    row = lax.broadcasted_iota(jnp.int32, (tm, 1), 0)
    rm = functools.partial(_row_mod, mod_c_ref, mod_b_ref, (i % tpb) * tm + row < tc)
    qr = tm // FFN_FINISH_STEPS

    def modulate():
        hm_ref[...] = _bf(x_ref[...] * (1.0 + rm(3 * sub + 1)) + rm(3 * sub))

    def partial_out():
        hm = hm_ref[...]
        h1 = _dot(hm, w1_ref[...])
        h3 = _dot(hm, w3_ref[...])
        return _dot(_bf(_silu(h1) * h3), w2_ref[...])

    def finish_previous():
        rg = 32
        for k in range(qr // rg):
            r0 = pl.multiple_of(f * qr + k * rg, 8)
            rows = pl.ds(r0, rg)
            prev_row = ((i + tpb - 1) % tpb) * tm + r0 + lax.broadcasted_iota(jnp.int32, (rg, 1), 0)
            gate = _row_mod(mod_c_ref, modp_b_ref, prev_row < tc, 3 * sub + 2)
            z = ALPHA * xp_ref[rows, :] + 0.5 * gate * acc_ref[1 - slot, rows, :]
            o_ref[rows, :] = _layer_norm(z, g_ref[...], b_ref[...])

    do_mm = i < nt
    do_fin = (i > 0) & (f < FFN_FINISH_STEPS)
    first = f == 0

    @pl.when(first & do_mm & jnp.logical_not(do_fin))
    def _():
        modulate()
        acc_ref[slot] = partial_out()

    @pl.when(first & do_mm & do_fin)
    def _():
        finish_previous()
        modulate()
        acc_ref[slot] = partial_out()

    @pl.when(jnp.logical_not(first) & do_mm & do_fin)
    def _():
        finish_previous()
        acc_ref[slot] += partial_out()

    @pl.when(jnp.logical_not(first) & do_mm & jnp.logical_not(do_fin))
    def _():
        acc_ref[slot] += partial_out()

    @pl.when(jnp.logical_not(do_mm) & do_fin)
    def _():
        finish_previous()


def _ffn(xa, mod, w1, w3, w2, g, b, *, sub, B, TT, tc):
    NT, D = xa.shape
    F = w1.shape[1]
    tpb = 2
    tm = TT // tpb
    fc = 256
    nf = F // fc
    nt = NT // tm
    kern = functools.partial(_ffn_kernel, sub=sub, tm=tm, tc=tc, tpb=tpb, nf=nf, nt=nt)
    cur = lambda i: jnp.minimum(i, nt - 1)
    prev = lambda i: jnp.maximum(i - 1, 0)
    return pl.pallas_call(
        kern,
        grid=(nt + 1, nf),
        in_specs=[pl.BlockSpec((tm, D), lambda i, f: (cur(i), 0)),
                  pl.BlockSpec((tm, D), lambda i, f: (prev(i), 0)),
                  pl.BlockSpec((None, N_MOD, D), lambda i, f: (B, 0, 0)),
                  pl.BlockSpec((None, N_MOD, D), lambda i, f: (cur(i) // tpb, 0, 0)),
                  pl.BlockSpec((None, N_MOD, D), lambda i, f: (prev(i) // tpb, 0, 0)),
                  pl.BlockSpec((D, fc), lambda i, f: (0, f)),
                  pl.BlockSpec((D, fc), lambda i, f: (0, f)),
                  pl.BlockSpec((fc, D), lambda i, f: (f, 0)),
                  pl.BlockSpec((1, D), lambda i, f: (0, 0)),
                  pl.BlockSpec((1, D), lambda i, f: (0, 0))],
        out_specs=pl.BlockSpec((tm, D), lambda i, f: (prev(i), 0)),
        out_shape=jax.ShapeDtypeStruct((NT, D), F32),
        scratch_shapes=[pltpu.VMEM((tm, D), BF16), pltpu.VMEM((2, tm, D), F32)],
        compiler_params=_cparams(("arbitrary", "arbitrary")),
        name=f"ffn{sub}",
    )(xa, xa, mod, mod, mod, w1, w3, w2, g, b)


def _seg_ones():
    r = lax.broadcasted_iota(jnp.int32, (LANES, LANES), 0) // HALF
    c = lax.broadcasted_iota(jnp.int32, (LANES, LANES), 1) // HALF
    return jnp.where(r == c, 1.0, 0.0).astype(BF16)


def _kin_kernel(x_ref, mod_c_ref, mod_b_ref, w_ref, qw_ref, kw_ref, cos_ref, sa_ref, sb_ref,
                gqkv_ref, gz_ref, gba_ref, aq_ref, ak_ref, av_ref, su_ref, *, tm, tc, tpb):
    rm = functools.partial(_row_mod, mod_c_ref, mod_b_ref, _ctx_rows(tm, tc, tpb))
    hm = _bf(x_ref[...] * (1.0 + rm(4)) + rm(3))
    r = _dot(hm, w_ref[...])
    gqkv_ref[...] = r[:, C_QKV:C_Z]
    gz_ref[...] = r[:, C_Z:C_BA]
    gba_ref[...] = r[:, C_BA:C_AQ]
    av_ref[...] = _bf(r[:, C_AV:C_SU])
    su_ref[...] = r[:, C_SU:C_END]

    seg = _seg_ones()
    cos, sa, sb = cos_ref[...], sa_ref[...], sb_ref[...]

    def norm_rope(blk, w_row):
        ss = _split_dot(blk * blk, seg)
        y = blk * lax.rsqrt(ss * (1.0 / ATT_DH) + EPS) * w_row
        return y * cos + pltpu.roll(y, LANES - ROPE_PAIRS, 1) * sa + pltpu.roll(y, ROPE_PAIRS, 1) * sb

    qw = qw_ref[...] * (ATT_DH ** -0.5)
    for j in range(ATT_W // LANES):
        aq_ref[:, j * LANES:(j + 1) * LANES] = _bf(
            norm_rope(r[:, C_AQ + j * LANES:C_AQ + (j + 1) * LANES], qw))
    ak_ref[...] = _bf(norm_rope(r[:, C_AK:C_AV], kw_ref[...]))


def _kin(xa, mod, w_in_p, qw, kw, cos, sa, sb, *, B, TT, tc):
    NT, D = xa.shape
    tpb = 4
    tm = TT // tpb
    kern = functools.partial(_kin_kernel, tm=tm, tc=tc, tpb=tpb)
    widths = [(GDN_QKV, F32), (GDN_W, F32), (GDN_PAIRS * LANES, F32), (ATT_W, BF16),
              (LANES, BF16), (LANES, BF16), (S5_W, F32)]
    return pl.pallas_call(
        kern,
        grid=(NT // tm,),
        in_specs=[pl.BlockSpec((tm, D), lambda i: (i, 0)),
                  pl.BlockSpec((None, N_MOD, D), lambda i: (B, 0, 0)),
                  pl.BlockSpec((None, N_MOD, D), lambda i: (i // tpb, 0, 0)),
                  pl.BlockSpec((D, C_END), lambda i: (0, 0)),
                  pl.BlockSpec((1, LANES), lambda i: (0, 0)),
                  pl.BlockSpec((1, LANES), lambda i: (0, 0)),
                  pl.BlockSpec((tm, LANES), lambda i: (i % tpb, 0)),
                  pl.BlockSpec((tm, LANES), lambda i: (i % tpb, 0)),
                  pl.BlockSpec((tm, LANES), lambda i: (i % tpb, 0))],
        out_specs=[pl.BlockSpec((tm, w), lambda i: (i, 0)) for w, _ in widths],
        out_shape=[jax.ShapeDtypeStruct((NT, w), dt) for w, dt in widths],
        compiler_params=_cparams(("parallel",)),
        name="mixer_in",
    )(xa, mod, mod, w_in_p, qw, kw, cos, sa, sb)


def _blk(x, lo):
    z = jnp.zeros_like(x)
    return jnp.concatenate([jnp.where(lo, x, z), jnp.where(lo, z, x)], axis=0)


def _gdn_kernel(qkv_in_ref, z_ref, ba_ref, cw_ref, hp_ref, nw_ref, o_ref,
                xp_ref, qkv_ref, bg_ref, of_ref, ob_ref,
                qg_ref, in_ref, w_ref, u_ref, k2_ref, r_ref, el_ref, s_ref, *, tc, t):
    tt = tc + t
    seg = _seg_ones()

    pad = 8
    for off, n in ((0, tc), (tc, t)):
        for part in range(3):
            xp_ref[part, 0:pad, :] = jnp.zeros((pad, LANES), F32)
            xp_ref[part, pad:pad + n, :] = qkv_in_ref[off:off + n, part * LANES:(part + 1) * LANES]
            xp_ref[part, pad + n:2 * pad + n, :] = jnp.zeros((pad, LANES), F32)
        nb = n // 8
        rb = min(nb, 128)
        for r in range(8):
            for i0 in range(0, nb, rb):
                for part in range(3):
                    lanes = slice(part * LANES, (part + 1) * LANES)
                    base = pad - CONV_K // 2 + r + 8 * i0
                    acc = cw_ref[0:1, lanes] * xp_ref[part, pl.ds(base, rb, stride=8), :]
                    for j in range(1, CONV_K):
                        acc = acc + cw_ref[j:j + 1, lanes] * xp_ref[part, pl.ds(base + j, rb, stride=8), :]
                    y = _silu(acc)
                    if part == 0:
                        y = y * lax.rsqrt(_split_dot(y * y, seg) + EPS) * (GDN_DK ** -0.5)
                    elif part == 1:
                        y = y * lax.rsqrt(_split_dot(y * y, seg) + EPS)
                    qkv_ref[part, pl.ds(off + r + 8 * i0, rb, stride=8), :] = y

    ba = ba_ref[...]
    bg_ref[:, 0:LANES] = _sigmoid(ba)
    sp_in = ba + hp_ref[1:2, :]
    softplus = jnp.maximum(sp_in, 0.0) + jnp.log1p(jnp.exp(-jnp.abs(sp_in)))
    bg_ref[:, LANES:2 * LANES] = hp_ref[0:1, :] * softplus

    rowi = lax.broadcasted_iota(jnp.int32, (CHUNK, LANES), 0)
    colj = lax.broadcasted_iota(jnp.int32, (CHUNK, LANES), 1) % HALF
    lo = lax.broadcasted_iota(jnp.int32, (CHUNK, LANES), 1) < HALF
    ti = lax.broadcasted_iota(jnp.int32, (CHUNK, CHUNK), 0)
    tj = lax.broadcasted_iota(jnp.int32, (CHUNK, CHUNK), 1)
    br = lax.broadcasted_iota(jnp.int32, (LANES, LANES), 0) // HALF
    bc = lax.broadcasted_iota(jnp.int32, (LANES, LANES), 1) // HALF
    blockmask = br == bc

    sub8 = rowi % 8
    nv = CHUNK // 8

    def chunk_cumsum(x, backward):
        for s in (1, 2, 4):
            if backward:
                x = x + jnp.where(sub8 < 8 - s, pltpu.roll(x, CHUNK - s, 0), 0.0)
            else:
                x = x + jnp.where(sub8 >= s, pltpu.roll(x, s, 0), 0.0)
        parts = [x[8 * j:8 * j + 8, :] for j in range(nv)]
        order = range(nv - 1, -1, -1) if backward else range(nv)
        edge = 0 if backward else 7
        run = None
        for j in order:
            if run is not None:
                parts[j] = parts[j] + run
            run = parts[j][edge:edge + 1, :]
        return jnp.concatenate(parts, axis=0)

    incl_dir = (colj <= rowi, colj >= rowi)
    strict_dir = (colj < rowi, colj > rowi)

    def each(f, *lists):
        return [f(*a) for a in zip(*lists)]

    def prepare(chains):
        ds = [d for _, d in chains]
        rows = [pl.ds(pl.multiple_of(c * CHUNK, CHUNK), CHUNK) for c, _ in chains]
        srows = [pl.ds(pl.multiple_of(c * LANES, LANES), LANES) for c, _ in chains]
        qn = [qkv_ref[0, r, :] for r in rows]
        kn = [qkv_ref[1, r, :] for r in rows]
        v = [qkv_ref[2, r, :] for r in rows]
        bx = [jnp.where(lo, bg_ref[r, 2 * d:2 * d + 1], bg_ref[r, 2 * d + 1:2 * d + 2]) for r, d in zip(rows, ds)]
        gcum = [chunk_cumsum(bg_ref[r, LANES:2 * LANES], d == 1) for r, d in zip(rows, ds)]
        gx = [jnp.where(lo, g[:, 4 + 2 * d:5 + 2 * d], g[:, 5 + 2 * d:6 + 2 * d]) for g, d in zip(gcum, ds)]
        eg = each(jnp.exp, gx)
        kb = each(jnp.multiply, kn, bx)
        vb = each(jnp.multiply, v, bx)
        kbg = each(jnp.multiply, kb, eg)
        grow = [jnp.sum(jnp.where(rowi == colj, g, 0.0), axis=0, keepdims=True) for g in gx]
        decay = [jnp.where(incl_dir[d], jnp.exp(jnp.where(incl_dir[d], g - gr, 0.0)), 0.0)
                 for g, gr, d in zip(gx, grow, ds)]
        kst = [_blk(_bf(k), lo) for k in kn]
        kq = [_dot_nt(jnp.concatenate([_bf(a), _bf(b)], axis=0), c) for a, b, c in zip(kb, qn, kst)]
        low = [jnp.where(strict_dir[d], x[0:CHUNK] * dc, 0.0) for x, dc, d in zip(kq, decay, ds)]
        intra = [jnp.where(incl_dir[d], x[CHUNK:2 * CHUNK] * dc, 0.0) for x, dc, d in zip(kq, decay, ds)]
        for r, x in zip(rows, zip(ds, qn, eg, intra)):
            d, q_, e_, i_ = x
            qg_ref[d, r, :] = _bf(q_ * e_)
            in_ref[d, r, :] = _bf(i_)
        lb = each(_bf, low)
        p = [_dot(a, _blk(a, lo)) for a in lb]
        nn = [-a for a in low]
        n_sq = int(math.log2(CHUNK)) - 1
        for r_ in range(n_sq):
            pb = each(_bf, p)
            pblk = [_blk(a, lo) for a in pb]
            if r_ < n_sq - 1:
                both = [_dot(jnp.concatenate([_bf(n_), b], axis=0), pk) for n_, b, pk in zip(nn, pb, pblk)]
                nn = [n_ + p_ + x[0:CHUNK] for n_, p_, x in zip(nn, p, both)]
                p = [x[CHUNK:2 * CHUNK] for x in both]
            else:
                nn = [n_ + p_ + _dot(_bf(n_), pk) for n_, p_, pk in zip(nn, p, pblk)]
        nb = each(_bf, nn)
        uw = [_dot(n_, jnp.concatenate([_blk(_bf(a), lo), _blk(_bf(b), lo)], axis=1))
              for n_, a, b in zip(nb, vb, kbg)]
        u = [a + x[:, 0:LANES] for a, x in zip(vb, uw)]
        w = [a + x[:, LANES:2 * LANES] for a, x in zip(kbg, uw)]
        g_last = [g[0:1, :] if d == 1 else g[CHUNK - 1:CHUNK, :] for g, d in zip(gx, ds)]
        kdec = [_bf(k * jnp.exp(gl - g)) for k, gl, g in zip(kn, g_last, gx)]
        wb = each(_bf, w)
        kr = [_dot_tn(a, jnp.concatenate([b, _bf(c)], axis=1)) for a, b, c in zip(kdec, wb, u)]
        for i, (c, d) in enumerate(chains):
            w_ref[d, rows[i], :] = wb[i]
            u_ref[d, rows[i], :] = u[i]
            k2_ref[d, srows[i], :] = _bf(jnp.where(blockmask, kr[i][:, 0:LANES], 0.0))
            r_ref[d, srows[i], :] = jnp.where(blockmask, kr[i][:, LANES:2 * LANES], 0.0)
            el_ref[d, pl.ds(pl.multiple_of(c * 8, 8), 8), :] = jnp.broadcast_to(jnp.exp(g_last[i]), (8, LANES))

    ncc = tc // CHUNK
    nct = tt // CHUNK
    ng = max(g for g in range(1, UNROLL_GDN + 1) if nct % g == 0)

    def group_chains(i):
        return [(i * ng + k, d) for k in range(ng) for d in range(2)]

    def body_a(i, carry):
        prepare(group_chains(i))
        return carry

    lax.fori_loop(0, nct // ng, body_a, 0)

    def advance(c, d, s):
        srows = pl.ds(pl.multiple_of(c * LANES, LANES), LANES)
        s16 = _bf(s)
        s_ref[d, srows, :] = s16
        e = el_ref[d, pl.ds(pl.multiple_of(c * 8, 8), 1), :]
        return s * e - _dot(k2_ref[d, srows, :], s16) + r_ref[d, srows, :]

    def body_b(i, carry):
        sf, sb = carry
        cb = jnp.where(i < ncc, ncc - 1 - i, nct - 1 - (i - ncc))
        return advance(i, 0, sf), advance(cb, 1, sb)

    zero = jnp.zeros((LANES, LANES), F32)
    lax.fori_loop(0, nct, body_b, (zero, zero))

    def emit(chains):
        rows = [pl.ds(pl.multiple_of(c * CHUNK, CHUNK), CHUNK) for c, _ in chains]
        s16 = [s_ref[d, pl.ds(pl.multiple_of(c * LANES, LANES), LANES), :] for c, d in chains]
        ws = [_dot(w_ref[d, r, :], s) for (_, d), r, s in zip(chains, rows, s16)]
        qs = [_dot(qg_ref[d, r, :], s) for (_, d), r, s in zip(chains, rows, s16)]
        v_new = [u_ref[d, r, :] - a for (_, d), r, a in zip(chains, rows, ws)]
        iv = [_dot(in_ref[d, r, :], _blk(_bf(vn), lo)) for (_, d), r, vn in zip(chains, rows, v_new)]
        for (_, d), r, a, b in zip(chains, rows, qs, iv):
            (ob_ref if d == 1 else of_ref)[r, :] = a + b

    def body_c(i, carry):
        emit(group_chains(i))
        return carry

    lax.fori_loop(0, nct // ng, body_c, 0)

    o = of_ref[...] + ob_ref[...]
    ss = _split_dot(o * o, seg)
    o = o * lax.rsqrt(ss * (1.0 / GDN_DK) + EPS) * nw_ref[...]
    o_ref[...] = _bf(o * _silu(z_ref[...]))


def _gdn(gqkv, gz, gba, conv_p, hp, nw, *, B, TT, tc):
    NT = gqkv.shape[0]
    t = TT - tc
    w3 = 3 * LANES
    nch = TT // CHUNK
    kern = functools.partial(_gdn_kernel, tc=tc, t=t)
    return pl.pallas_call(
        kern,
        grid=(B, GDN_PAIRS),
        in_specs=[pl.BlockSpec((TT, w3), lambda b, p: (b, p)),
                  pl.BlockSpec((TT, LANES), lambda b, p: (b, p)),
                  pl.BlockSpec((TT, LANES), lambda b, p: (b, p)),
                  pl.BlockSpec((None, 8, w3), lambda b, p: (p, 0, 0)),
                  pl.BlockSpec((None, 8, LANES), lambda b, p: (p, 0, 0)),
                  pl.BlockSpec((1, LANES), lambda b, p: (0, 0))],
        out_specs=pl.BlockSpec((TT, LANES), lambda b, p: (b, p)),
        out_shape=jax.ShapeDtypeStruct((NT, GDN_W), BF16),
        scratch_shapes=[pltpu.VMEM((3, t + 16, LANES), F32), pltpu.VMEM((3, TT, LANES), F32),
                        pltpu.VMEM((TT, 2 * LANES), F32), pltpu.VMEM((TT, LANES), F32),
                        pltpu.VMEM((TT, LANES), F32),
                        pltpu.VMEM((2, TT, LANES), BF16), pltpu.VMEM((2, TT, LANES), BF16),
                        pltpu.VMEM((2, TT, LANES), BF16), pltpu.VMEM((2, TT, LANES), F32),
                        pltpu.VMEM((2, nch * LANES, LANES), BF16), pltpu.VMEM((2, nch * LANES, LANES), F32),
                        pltpu.VMEM((2, nch * 8, LANES), F32), pltpu.VMEM((2, nch * LANES, LANES), BF16)],
        compiler_params=_cparams(("parallel", "parallel")),
        name="gdn",
    )(gqkv, gz, gba, conv_p, hp, nw)


def _att_tile(q_ref, k_ref, v_ref, o_ref, nk):
    lo = lax.broadcasted_iota(jnp.int32, (1, LANES), 1) < HALF
    k = k_ref[0:nk, :]
    v = v_ref[0:nk, :]

    def scores(h):
        j, half = h // 2, h % 2
        qj = q_ref[:, j * LANES:(j + 1) * LANES]
        zq = jnp.zeros_like(qj)
        return _dot_nt(jnp.where(lo, qj, zq) if half == 0 else jnp.where(lo, zq, qj), k)

    outs = []
    s = scores(0)
    for h in range(ATT_HEADS):
        s_next = scores(h + 1) if h + 1 < ATT_HEADS else None
        m = jnp.max(s, axis=-1, keepdims=True)
        p = jnp.exp(s - m)
        l = jnp.sum(p, axis=-1, keepdims=True)
        outs.append(_dot(_bf(p), v) / l)
        s = s_next
    for j in range(ATT_GROUP):
        o_ref[:, j * LANES:(j + 1) * LANES] = _bf(jnp.where(lo, outs[2 * j], outs[2 * j + 1]))


def _att_kernel(q_ref, k_ref, v_ref, o_ref, *, n_ctx_tiles, tc, tt):
    is_ctx = pl.program_id(1) < n_ctx_tiles

    @pl.when(is_ctx)
    def _():
        _att_tile(q_ref, k_ref, v_ref, o_ref, tc)

    @pl.when(jnp.logical_not(is_ctx))
    def _():
        _att_tile(q_ref, k_ref, v_ref, o_ref, tt)


def _attention(aq, ak, av, *, B, TT, tc):
    NT = aq.shape[0]
    tq = 256
    nq = TT // tq
    kv_spec = pl.BlockSpec((TT, LANES), lambda b, i: (b, 0))
    q_spec = pl.BlockSpec((tq, ATT_W), lambda b, i: (b * nq + i, 0))
    return pl.pallas_call(
        functools.partial(_att_kernel, n_ctx_tiles=tc // tq, tc=tc, tt=TT),
        grid=(B, nq),
        in_specs=[q_spec, kv_spec, kv_spec],
        out_specs=q_spec,
        out_shape=jax.ShapeDtypeStruct((NT, ATT_W), BF16),
        compiler_params=_cparams(("parallel", "parallel")),
        name="attention",
    )(aq, ak, av)


S5_BB = 2
GPV = LANES // S5_GH


def _s5_kernel(u0_ref, u1_ref, tg_ref, bp_ref, cp_ref, la_ref, lb_ref, d_ref, gw_ref, gb_ref, o_ref,
               uf_ref, sloc_ref, sin_ref, y0_ref, y1_ref, *, bb, ncc, nct):
    lc = S5_CHUNK
    nr = bb * nct
    w2 = 2 * S5_P
    blk8 = lax.broadcasted_iota(jnp.int32, (1, LANES), 1) // S5_GH
    u_refs = (u0_ref, u1_ref)
    y_refs = (y0_ref, y1_ref)

    for g in range(S5_GROUPS):
        for v in range(lc // GPV):
            acc = None
            for k in range(GPV):
                t = v * GPV + k
                src = u_refs[g // GPV][pl.ds(t, nr, stride=lc), :]
                sh = ((k - g % GPV) * S5_GH) % LANES
                src = pltpu.roll(src, sh, 1) if sh else src
                acc = src if acc is None else jnp.where(blk8 == k, src, acc)
            uf_ref[g, :, v * LANES:(v + 1) * LANES] = acc
        sl = _dot(_bf(uf_ref[g]), bp_ref[g])
        sloc_ref[2 * g] = sl[:, 0:w2]
        sloc_ref[2 * g + 1] = sl[:, w2:2 * w2]

    def body(i, carry):
        cb = jnp.where(i < ncc, ncc - 1 - i, nct - 1 - (i - ncc))
        out = []
        for g in range(S5_GROUPS):
            for d in range(2):
                s = carry[2 * g + d]
                rows = pl.ds(i if d == 0 else cb, bb, stride=nct)
                lanes = slice(d * w2, (d + 1) * w2)
                sin_ref[2 * g + d, rows, :] = s
                out.append(s * la_ref[g, :, lanes] + pltpu.roll(s, S5_P, 1) * lb_ref[g, :, lanes]
                           + sloc_ref[2 * g + d, rows, :])
        return tuple(out)

    zero = jnp.zeros((bb, w2), F32)
    lax.fori_loop(0, nct, body, (zero,) * (2 * S5_GROUPS))

    for g in range(S5_GROUPS):
        uf_ref[g] = (_dot(_bf(uf_ref[g]), tg_ref[g])
                     + _dot(_bf(sin_ref[2 * g]), cp_ref[g, 0:w2, :])
                     + _dot(_bf(sin_ref[2 * g + 1]), cp_ref[g, w2:2 * w2, :]))

    for t in range(lc):
        for v in range(S5_GROUPS // GPV):
            acc = None
            for k in range(GPV):
                g = v * GPV + k
                src = uf_ref[g, :, (t // GPV) * LANES:(t // GPV + 1) * LANES]
                sh = ((k - t % GPV) * S5_GH) % LANES
                src = pltpu.roll(src, sh, 1) if sh else src
                acc = src if acc is None else jnp.where(blk8 == k, src, acc)
            y_refs[v][pl.ds(t, nr, stride=lc), :] = acc

    y = jnp.concatenate([y0_ref[...], y1_ref[...]], axis=1)
    u = jnp.concatenate([u0_ref[...], u1_ref[...]], axis=1)
    zz = _gelu_tanh(y + d_ref[...] * u)
    o_ref[...] = _bf(zz * _sigmoid(_dot(_bf(zz), gw_ref[...]) + gb_ref[...]))


def _s5(su, tg, bp, cp, la, lb, d, gw, gb, *, B, TT, tc):
    NT, W = su.shape
    bb = S5_BB
    nct = TT // S5_CHUNK
    nr = bb * nct
    kern = functools.partial(_s5_kernel, bb=bb, ncc=tc // S5_CHUNK, nct=nct)
    full = lambda a: pl.BlockSpec(a.shape, lambda i: (0,) * a.ndim)
    return pl.pallas_call(
        kern,
        grid=(B // bb,),
        in_specs=[pl.BlockSpec((bb * TT, LANES), lambda i: (i, 0)),
                  pl.BlockSpec((bb * TT, LANES), lambda i: (i, 1)),
                  full(tg), full(bp), full(cp), full(la), full(lb), full(d), full(gw), full(gb)],
        out_specs=pl.BlockSpec((bb * TT, W), lambda i: (i, 0)),
        out_shape=jax.ShapeDtypeStruct((NT, W), BF16),
        scratch_shapes=[pltpu.VMEM((S5_GROUPS, nr, W), F32), pltpu.VMEM((2 * S5_GROUPS, nr, LANES), F32),
                        pltpu.VMEM((2 * S5_GROUPS, nr, LANES), F32),
                        pltpu.VMEM((bb * TT, LANES), F32), pltpu.VMEM((bb * TT, LANES), F32)],
        compiler_params=_cparams(("parallel",)),
        name="s5",
    )(su, su, tg, bp, cp, la, lb, d, gw, gb)


def _s5_matrices(lam_re, lam_im, log_dt, b_re, b_im, c_re, c_im):
    Lc = S5_CHUNK
    lam = lax.complex(lam_re.astype(F32), lam_im.astype(F32))
    dt = jnp.exp(log_dt.astype(F32))[..., None]
    lam_dt = lam * dt
    lam_bar = jnp.exp(lam_dt)
    b_bar = ((lam_bar - 1.0) / lam)[..., None] * lax.complex(b_re.astype(F32), b_im.astype(F32))
    cc = lax.complex(c_re.astype(F32), c_im.astype(F32))
    ks = jnp.arange(Lc + 1, dtype=F32)
    pw = jnp.exp(lam_dt[..., None, :] * ks[:, None])
    kern = jnp.real(jnp.einsum('dghp,dgkp,dgpi->dgkhi', cc, pw[:, :, :Lc], b_bar))
    i_idx = jnp.arange(Lc)[:, None]
    j_idx = jnp.arange(Lc)[None, :]
    lag = i_idx - j_idx
    kf = kern[0][:, jnp.clip(lag, 0, Lc - 1)]
    kb = kern[1][:, jnp.clip(-lag, 0, Lc - 1)]
    tmat = (jnp.where((lag >= 0)[None, :, :, None, None], kf, 0.0)
            + jnp.where((lag <= 0)[None, :, :, None, None], kb, 0.0))
    tg = tmat.transpose(0, 2, 4, 1, 3).reshape(S5_GROUPS, Lc * S5_GH, Lc * S5_GH)
    pf = pw[0][:, Lc - 1 - jnp.arange(Lc)]
    pb = pw[1][:, jnp.arange(Lc)]
    sf = jnp.einsum('gjp,gph->gjhp', pf, b_bar[0])
    sb = jnp.einsum('gjp,gph->gjhp', pb, b_bar[1])
    bp = jnp.concatenate([jnp.real(sf), jnp.imag(sf), jnp.real(sb), jnp.imag(sb)], axis=-1)
    bp = bp.reshape(S5_GROUPS, Lc * S5_GH, 4 * S5_P)
    qf = jnp.einsum('ghp,gip->gpih', cc[0], pw[0][:, 1 + jnp.arange(Lc)])
    qb = jnp.einsum('ghp,gip->gpih', cc[1], pw[1][:, Lc - jnp.arange(Lc)])
    cp = jnp.concatenate([jnp.real(qf), -jnp.imag(qf), jnp.real(qb), -jnp.imag(qb)], axis=1)
    cp = cp.reshape(S5_GROUPS, 4 * S5_P, Lc * S5_GH)
    lc = pw[:, :, Lc]
    la = jnp.concatenate([jnp.real(lc[0]), jnp.real(lc[0]), jnp.real(lc[1]), jnp.real(lc[1])], axis=-1)
    lb = jnp.concatenate([-jnp.imag(lc[0]), jnp.imag(lc[0]), -jnp.imag(lc[1]), jnp.imag(lc[1])], axis=-1)
    return (tg.astype(BF16), bp.astype(BF16), cp.astype(BF16),
            la.reshape(S5_GROUPS, 1, 4 * S5_P), lb.reshape(S5_GROUPS, 1, 4 * S5_P))


def _gelu_tanh(x):
    return 0.5 * x * (1.0 + jnp.tanh(math.sqrt(2.0 / math.pi) * (x + 0.044715 * (x * x * x))))


def _kout_kernel(x_ref, mod_c_ref, mod_b_ref, go_ref, ao_ref, so_ref,
                 wg_ref, wa_ref, ws_ref, g_ref, b_ref, o_ref, *, tm, tc, tpb):
    rm = functools.partial(_row_mod, mod_c_ref, mod_b_ref, _ctx_rows(tm, tc, tpb))
    y = _dot(go_ref[...], wg_ref[...]) + _dot(ao_ref[...], wa_ref[...]) + _dot(so_ref[...], ws_ref[...])
    z = ALPHA * x_ref[...] + rm(5) * y
    o_ref[...] = _layer_norm(z, g_ref[...], b_ref[...])


def _kout(xa, mod, go, ao, so, wg, wa, ws, g, b, *, B, TT, tc):
    NT, D = xa.shape
    tpb = 4
    tm = TT // tpb
    kern = functools.partial(_kout_kernel, tm=tm, tc=tc, tpb=tpb)
    row = lambda w: pl.BlockSpec((tm, w), lambda i: (i, 0))
    full = lambda a: pl.BlockSpec(a.shape, lambda i: (0,) * a.ndim)
    return pl.pallas_call(
        kern,
        grid=(NT // tm,),
        in_specs=[row(D),
                  pl.BlockSpec((None, N_MOD, D), lambda i: (B, 0, 0)),
                  pl.BlockSpec((None, N_MOD, D), lambda i: (i // tpb, 0, 0)),
                  row(GDN_W), row(ATT_W), row(S5_W),
                  full(wg), full(wa), full(ws), full(g), full(b)],
        out_specs=row(D),
        out_shape=jax.ShapeDtypeStruct((NT, D), F32),
        compiler_params=_cparams(("parallel",)),
        name="mixer_out",
    )(xa, mod, mod, go, ao, so, wg, wa, ws, g, b)


def _perm_in_cols():
    off_z = GDN_QKV
    off_b = off_z + GDN_W
    off_a = off_b + 2 * GDN_HEADS
    off_q = off_a + 2 * GDN_HEADS
    off_k = off_q + ATT_W
    off_v = off_k + ATT_KV_HEADS * ATT_DH
    off_u = off_v + ATT_KV_HEADS * ATT_DH
    cols = []
    for p in range(GDN_PAIRS):
        for part in range(3):
            base = part * GDN_W + 2 * p * GDN_DK
            cols += list(range(base, base + 2 * GDN_DK))
    cols += list(range(off_z, off_z + GDN_W))
    for p in range(GDN_PAIRS):
        blk = []
        for off in (off_b, off_a):
            for d in range(2):
                blk += [off + d * GDN_HEADS + 2 * p, off + d * GDN_HEADS + 2 * p + 1]
        cols += blk + [-1] * (LANES - len(blk))
    for j in range(ATT_GROUP):
        for h in (j, j + ATT_GROUP):
            cols += list(range(off_q + h * ATT_DH, off_q + (h + 1) * ATT_DH))
    cols += list(range(off_k, off_k + 2 * ATT_DH))
    cols += list(range(off_v, off_v + 2 * ATT_DH))
    cols += list(range(off_u, off_u + S5_W))
    return np.asarray(cols, dtype=np.int32)


def _gdn_pair_cols():
    cols = []
    for p in range(GDN_PAIRS):
        for part in range(3):
            base = part * GDN_W + 2 * p * GDN_DK
            cols += list(range(base, base + 2 * GDN_DK))
    return np.asarray(cols, dtype=np.int32)


def _att_out_rows():
    rows = []
    for j in range(ATT_GROUP):
        for h in (j, j + ATT_GROUP):
            rows += list(range(h * ATT_DH, (h + 1) * ATT_DH))
    return np.asarray(rows, dtype=np.int32)


def _take(w, idx, axis):
    idx = [int(i) for i in idx]
    parts = []
    i = 0
    while i < len(idx):
        j = i + 1
        if idx[i] < 0:
            while j < len(idx) and idx[j] < 0:
                j += 1
            shape = list(w.shape)
            shape[axis] = j - i
            parts.append(jnp.zeros(shape, w.dtype))
        else:
            while j < len(idx) and idx[j] == idx[j - 1] + 1:
                j += 1
            parts.append(lax.slice_in_dim(w, idx[i], idx[j - 1] + 1, axis=axis))
        i = j
    return jnp.concatenate(parts, axis=axis)


def _rope_tables(tc, t):
    rows = t // GRID_W
    row = jnp.repeat(jnp.arange(rows), GRID_W)
    col = jnp.tile(jnp.arange(GRID_W), rows)
    inv_freq = ROPE_THETA ** (-jnp.arange(ROPE_PAIRS, dtype=F32) / ROPE_PAIRS)
    ang = jnp.stack([row, col], axis=-1).astype(F32)[..., None] * inv_freq
    cos, sin = jnp.cos(ang), jnp.sin(ang)
    zero = jnp.zeros_like(sin)
    cos_h = jnp.stack([cos, cos], axis=2).reshape(t, ATT_DH)
    sa_h = jnp.stack([-sin, zero], axis=2).reshape(t, ATT_DH)
    sb_h = jnp.stack([zero, sin], axis=2).reshape(t, ATT_DH)
    def full(tab, fill):
        tab = jnp.concatenate([jnp.full((tc, ATT_DH), fill, F32), tab], axis=0)
        return jnp.concatenate([tab, tab], axis=1)
    return full(cos_h, 1.0), full(sa_h, 0.0), full(sb_h, 0.0)


def kernel(x, c, ctx, c_ctx, w_ada, b_ada, ln_g, ln_b, ffn_w1, ffn_w3, ffn_w2, w_in, w_out, gdn_conv_w,
           gdn_a_log, gdn_dt_bias, gdn_norm_w, q_norm_w, k_norm_w, s5_lam_re, s5_lam_im, s5_log_dt,
           s5_b_re, s5_b_im, s5_c_re, s5_c_im, s5_d, glu_w, glu_b):
    B, T, D = x.shape
    tc = ctx.shape[1]
    TT = tc + T
    NT = B * TT
    L = w_ada.shape[0]
    kw = dict(B=B, TT=TT, tc=tc)

    R = ((B + 1 + 7) // 8) * 8
    c_all = jnp.concatenate([c, c_ctx[None, :], jnp.zeros((R - B - 1, D), F32)], axis=0)
    mod_all = _modulation(c_all, w_ada, b_ada).reshape(L, R, N_MOD, D)

    in_cols = _perm_in_cols()
    pair_cols = _gdn_pair_cols()
    att_rows = _att_out_rows()
    cos, sa, sb = _rope_tables(tc, T)

    def head_params(v):
        v = v.astype(F32).reshape(2, GDN_PAIRS, 2).transpose(1, 0, 2).reshape(GDN_PAIRS, 4)
        return jnp.pad(v, ((0, 0), (4, LANES - 8)))

    xa = jnp.concatenate([ctx, x], axis=1).reshape(NT, D)
    for l in range(L):
        mod = mod_all[l]
        w1, w3, w2 = _bf(ffn_w1[l]), _bf(ffn_w3[l]), _bf(ffn_w2[l])
        g_l, b_l = ln_g[l], ln_b[l]
        xa = _ffn(xa, mod, w1[0], w3[0], w2[0], g_l[0:1], b_l[0:1], sub=0, **kw)

        w_in_p = _take(_bf(w_in[l]), in_cols, 1)
        qw = jnp.tile(q_norm_w[l], 2)[None, :]
        kwt = jnp.tile(k_norm_w[l], 2)[None, :]
        gqkv, gz, gba, aq, ak, av, su = _kin(xa, mod, w_in_p, qw, kwt, cos, sa, sb, **kw)

        conv_p = jnp.pad(_take(gdn_conv_w[l], pair_cols, 1), ((0, 8 - CONV_K), (0, 0)))
        conv_p = conv_p.reshape(8, GDN_PAIRS, 3 * LANES).transpose(1, 0, 2)
        hp = jnp.stack([head_params(-jnp.exp(gdn_a_log[l].astype(F32))), head_params(gdn_dt_bias[l])], axis=1)
        hp = jnp.pad(hp, ((0, 0), (0, 6), (0, 0)))
        nw = jnp.tile(gdn_norm_w[l], 2)[None, :]
        go = _gdn(gqkv, gz, gba, conv_p, hp, nw, **kw)

        ao = _attention(aq, ak, av, **kw)

        tg, bp, cp, la, lb = _s5_matrices(s5_lam_re[l], s5_lam_im[l], s5_log_dt[l], s5_b_re[l], s5_b_im[l],
                                          s5_c_re[l], s5_c_im[l])
        so = _s5(su, tg, bp, cp, la, lb, s5_d[l][None, :], _bf(glu_w[l]), glu_b[l][None, :], **kw)

        wo = _bf(w_out[l])
        wg = wo[0:GDN_W]
        wa = _take(wo[GDN_W:GDN_W + ATT_W], att_rows, 0)
        ws = wo[GDN_W + ATT_W:]
        xa = _kout(xa, mod, go, ao, so, wg, wa, ws, g_l[1:2], b_l[1:2], **kw)
        xa = _ffn(xa, mod, w1[1], w3[1], w2[1], g_l[2:3], b_l[2:3], sub=2, **kw)
    return xa.reshape(B, TT, D)[:, tc:, :]
```

```python
import functools
import math

import numpy as np
import jax
import jax.numpy as jnp
from jax import lax
from jax.experimental import pallas as pl
from jax.experimental.pallas import tpu as pltpu

F32 = jnp.float32
BF16 = jnp.bfloat16

D_MODEL = 1024
GRID_W = 64
GDN_HEADS = 6
GDN_DK = 64
GDN_PAIRS = GDN_HEADS // 2
GDN_QKV = 3 * GDN_HEADS * GDN_DK
GDN_W = GDN_HEADS * GDN_DK
CONV_K = 5
CHUNK = 64
UNROLL_GDN = 9
ATT_HEADS = 6
ATT_KV_HEADS = 2
ATT_GROUP = ATT_HEADS // ATT_KV_HEADS
ATT_DH = 64
ATT_W = ATT_HEADS * ATT_DH
ROPE_THETA = 10000.0
ROPE_PAIRS = ATT_DH // 4
S5_GROUPS = 16
S5_GH = 16
S5_P = 64
S5_W = S5_GROUPS * S5_GH
S5_CHUNK = 16
N_MOD = 9
DEPTH = 4
ALPHA = (2.0 * DEPTH) ** 0.25
EPS = 1e-6
LANES = 128
HALF = LANES // 2

C_QKV = 0
C_Z = C_QKV + GDN_QKV
C_BA = C_Z + GDN_W
C_AQ = C_BA + GDN_PAIRS * LANES
C_AK = C_AQ + ATT_W
C_AV = C_AK + LANES
C_SU = C_AV + LANES
C_END = C_SU + S5_W

FFN_FINISH_STEPS = 4
VMEM_LIMIT = 56 * 1024 * 1024


def _cparams(sem):
    return pltpu.CompilerParams(dimension_semantics=sem, vmem_limit_bytes=VMEM_LIMIT)


def _bf(x):
    return x.astype(BF16)


def _dot(a, b):
    return jnp.dot(a, b, preferred_element_type=F32)


def _dot_nt(a, b):
    return lax.dot_general(a, b, (((1,), (1,)), ((), ())), preferred_element_type=F32)


def _dot_tn(a, b):
    return lax.dot_general(a, b, (((0,), (0,)), ((), ())), preferred_element_type=F32)


def _split_dot(x, m):
    hi = _bf(x)
    lo = _bf(x - hi.astype(F32))
    return _dot(hi, m) + _dot(lo, m)


def _split3_dot_left(m, x):
    hi = _bf(x)
    r1 = x - hi.astype(F32)
    mid = _bf(r1)
    lo = _bf(r1 - mid.astype(F32))
    return _dot(m, hi) + _dot(m, mid) + _dot(m, lo)


def _sigmoid(x):
    return 0.5 + 0.5 * jnp.tanh(0.5 * x)


def _silu(x):
    h = 0.5 * x
    return h + h * jnp.tanh(h)


def _layer_norm(z, g, b):
    mu = jnp.mean(z, axis=-1, keepdims=True)
    zc = z - mu
    var = jnp.mean(zc * zc, axis=-1, keepdims=True)
    return zc * lax.rsqrt(var + EPS) * g + b


def _ctx_rows(tm, tc, tiles_per_batch):
    row0 = (pl.program_id(0) % tiles_per_batch) * tm
    return row0 + lax.broadcasted_iota(jnp.int32, (tm, 1), 0) < tc


def _row_mod(mod_c_ref, mod_b_ref, is_ctx, idx):
    return jnp.where(is_ctx, mod_c_ref[idx:idx + 1, :], mod_b_ref[idx:idx + 1, :])


def _mod_kernel(c_ref, w_ref, b_ref, o_ref):
    sc = _silu(c_ref[...])
    o_ref[...] = jnp.dot(sc, w_ref[...], preferred_element_type=F32,
                         precision=lax.Precision.HIGHEST) + b_ref[...]


def _modulation(c_all, w_ada, b_ada):
    L, D, NM = w_ada.shape
    R = c_all.shape[0]
    tn = 1152
    return pl.pallas_call(
        _mod_kernel,
        grid=(L, NM // tn),
        in_specs=[pl.BlockSpec((R, D), lambda l, j: (0, 0)),
                  pl.BlockSpec((None, D, tn), lambda l, j: (l, 0, j)),
                  pl.BlockSpec((None, 1, tn), lambda l, j: (l, 0, j))],
        out_specs=pl.BlockSpec((None, R, tn), lambda l, j: (l, 0, j)),
        out_shape=jax.ShapeDtypeStruct((L, R, NM), F32),
        compiler_params=_cparams(("parallel", "parallel")),
        name="adaln_mod",
    )(c_all, w_ada, b_ada.reshape(L, 1, NM))


def _ffn_kernel(x_ref, xp_ref, mod_c_ref, mod_b_ref, modp_b_ref, w13_ref, w2_ref, g_ref, b_ref, o_ref,
                hm_ref, acc_ref, *, sub, tm, tc, tpb, nf, nt):
    i = pl.program_id(0)
    f = pl.program_id(1)
    slot = i % 2
    row = lax.broadcasted_iota(jnp.int32, (tm, 1), 0)
    rm = functools.partial(_row_mod, mod_c_ref, mod_b_ref, (i % tpb) * tm + row < tc)
    qr = tm // FFN_FINISH_STEPS

    def modulate():
        hm_ref[...] = _bf(x_ref[...] * (1.0 + rm(3 * sub + 1)) + rm(3 * sub))

    def partial_out():
        hm = hm_ref[...]
        h = _dot(hm, w13_ref[...])
        return _dot(_bf(_silu(h[:, 0:FFN_FC]) * h[:, FFN_FC:2 * FFN_FC]), w2_ref[...])

    def finish_previous():
        rg = 32
        for k in range(qr // rg):
            r0 = pl.multiple_of(f * qr + k * rg, 8)
            rows = pl.ds(r0, rg)
            prev_row = ((i + tpb - 1) % tpb) * tm + r0 + lax.broadcasted_iota(jnp.int32, (rg, 1), 0)
            gate = _row_mod(mod_c_ref, modp_b_ref, prev_row < tc, 3 * sub + 2)
            z = ALPHA * xp_ref[rows, :] + 0.5 * gate * acc_ref[1 - slot, rows, :]
            o_ref[rows, :] = _layer_norm(z, g_ref[...], b_ref[...])

    do_mm = i < nt
    do_fin = (i > 0) & (f < FFN_FINISH_STEPS)
    first = f == 0

    @pl.when(first & do_mm & jnp.logical_not(do_fin))
    def _():
        modulate()
        acc_ref[slot] = partial_out()

    @pl.when(first & do_mm & do_fin)
    def _():
        finish_previous()
        modulate()
        acc_ref[slot] = partial_out()

    @pl.when(jnp.logical_not(first) & do_mm & do_fin)
    def _():
        finish_previous()
        acc_ref[slot] += partial_out()

    @pl.when(jnp.logical_not(first) & do_mm & jnp.logical_not(do_fin))
    def _():
        acc_ref[slot] += partial_out()

    @pl.when(jnp.logical_not(do_mm) & do_fin)
    def _():
        finish_previous()


FFN_FC = 256


def _ffn_pack_w13(w1, w3):
    D, F = w1.shape
    nf = F // FFN_FC
    w = jnp.concatenate([_bf(w1).reshape(D, nf, FFN_FC), _bf(w3).reshape(D, nf, FFN_FC)], axis=2)
    return w.transpose(1, 0, 2)


def _ffn(xa, mod, w13, w2, g, b, *, sub, B, TT, tc):
    NT, D = xa.shape
    nf = w13.shape[0]
    fc = FFN_FC
    tpb = 2
    tm = TT // tpb
    nt = NT // tm
    kern = functools.partial(_ffn_kernel, sub=sub, tm=tm, tc=tc, tpb=tpb, nf=nf, nt=nt)
    cur = lambda i: jnp.minimum(i, nt - 1)
    prev = lambda i: jnp.maximum(i - 1, 0)
    return pl.pallas_call(
        kern,
        grid=(nt + 1, nf),
        in_specs=[pl.BlockSpec((tm, D), lambda i, f: (cur(i), 0)),
                  pl.BlockSpec((tm, D), lambda i, f: (prev(i), 0)),
                  pl.BlockSpec((None, N_MOD, D), lambda i, f: (B, 0, 0)),
                  pl.BlockSpec((None, N_MOD, D), lambda i, f: (cur(i) // tpb, 0, 0)),
                  pl.BlockSpec((None, N_MOD, D), lambda i, f: (prev(i) // tpb, 0, 0)),
                  pl.BlockSpec((None, D, 2 * fc), lambda i, f: (f, 0, 0)),
                  pl.BlockSpec((fc, D), lambda i, f: (f, 0)),
                  pl.BlockSpec((1, D), lambda i, f: (0, 0)),
                  pl.BlockSpec((1, D), lambda i, f: (0, 0))],
        out_specs=pl.BlockSpec((tm, D), lambda i, f: (prev(i), 0)),
        out_shape=jax.ShapeDtypeStruct((NT, D), F32),
        scratch_shapes=[pltpu.VMEM((tm, D), BF16), pltpu.VMEM((2, tm, D), F32)],
        compiler_params=_cparams(("arbitrary", "arbitrary")),
        name=f"ffn{sub}",
    )(xa, xa, mod, mod, mod, w13, w2, g, b)


def _seg_ones():
    r = lax.broadcasted_iota(jnp.int32, (LANES, LANES), 0) // HALF
    c = lax.broadcasted_iota(jnp.int32, (LANES, LANES), 1) // HALF
    return jnp.where(r == c, 1.0, 0.0).astype(BF16)


def _kin_kernel(x_ref, mod_c_ref, mod_b_ref, w_ref, qw_ref, kw_ref, cos_ref, sa_ref, sb_ref,
                gqkv_ref, gz_ref, gba_ref, aq_ref, ak_ref, av_ref, su_ref, *, tm, tc, tpb):
    rm = functools.partial(_row_mod, mod_c_ref, mod_b_ref, _ctx_rows(tm, tc, tpb))
    hm = _bf(x_ref[...] * (1.0 + rm(4)) + rm(3))
    r = _dot(hm, w_ref[...])
    gqkv_ref[...] = r[:, C_QKV:C_Z]
    gz_ref[...] = r[:, C_Z:C_BA]
    gba_ref[...] = r[:, C_BA:C_AQ]
    av_ref[...] = _bf(r[:, C_AV:C_SU])
    su_ref[...] = r[:, C_SU:C_END]

    seg = _seg_ones()
    cos, sa, sb = cos_ref[...], sa_ref[...], sb_ref[...]

    def norm_rope(blk, w_row):
        ss = _split_dot(blk * blk, seg)
        y = blk * lax.rsqrt(ss * (1.0 / ATT_DH) + EPS) * w_row
        return y * cos + pltpu.roll(y, LANES - ROPE_PAIRS, 1) * sa + pltpu.roll(y, ROPE_PAIRS, 1) * sb

    qw = qw_ref[...] * (ATT_DH ** -0.5)
    for j in range(ATT_W // LANES):
        aq_ref[:, j * LANES:(j + 1) * LANES] = _bf(
            norm_rope(r[:, C_AQ + j * LANES:C_AQ + (j + 1) * LANES], qw))
    ak_ref[...] = _bf(norm_rope(r[:, C_AK:C_AV], kw_ref[...]))


def _kin(xa, mod, w_in_p, qw, kw, cos, sa, sb, *, B, TT, tc):
    NT, D = xa.shape
    tpb = 4
    tm = TT // tpb
    kern = functools.partial(_kin_kernel, tm=tm, tc=tc, tpb=tpb)
    widths = [(GDN_QKV, F32), (GDN_W, F32), (GDN_PAIRS * LANES, F32), (ATT_W, BF16),
              (LANES, BF16), (LANES, BF16), (S5_W, F32)]
    return pl.pallas_call(
        kern,
        grid=(NT // tm,),
        in_specs=[pl.BlockSpec((tm, D), lambda i: (i, 0)),
                  pl.BlockSpec((None, N_MOD, D), lambda i: (B, 0, 0)),
                  pl.BlockSpec((None, N_MOD, D), lambda i: (i // tpb, 0, 0)),
                  pl.BlockSpec((D, C_END), lambda i: (0, 0)),
                  pl.BlockSpec((1, LANES), lambda i: (0, 0)),
                  pl.BlockSpec((1, LANES), lambda i: (0, 0)),
                  pl.BlockSpec((tm, LANES), lambda i: (i % tpb, 0)),
                  pl.BlockSpec((tm, LANES), lambda i: (i % tpb, 0)),
                  pl.BlockSpec((tm, LANES), lambda i: (i % tpb, 0))],
        out_specs=[pl.BlockSpec((tm, w), lambda i: (i, 0)) for w, _ in widths],
        out_shape=[jax.ShapeDtypeStruct((NT, w), dt) for w, dt in widths],
        compiler_params=_cparams(("parallel",)),
        name="mixer_in",
    )(xa, mod, mod, w_in_p, qw, kw, cos, sa, sb)


def _blk(x, lo):
    z = jnp.zeros_like(x)
    return jnp.concatenate([jnp.where(lo, x, z), jnp.where(lo, z, x)], axis=0)


def _gdn_kernel(qkv_in_ref, z_ref, ba_ref, cw_ref, hp_ref, nw_ref, o_ref,
                xp_ref, qkv_ref, bg_ref, of_ref, ob_ref,
                qg_ref, in_ref, w_ref, u_ref, k2_ref, r_ref, el_ref, s_ref, *, tc, t):
    tt = tc + t
    seg = _seg_ones()

    pad = 8
    for off, n in ((0, tc), (tc, t)):
        for part in range(3):
            xp_ref[part, 0:pad, :] = jnp.zeros((pad, LANES), F32)
            xp_ref[part, pad:pad + n, :] = qkv_in_ref[off:off + n, part * LANES:(part + 1) * LANES]
            xp_ref[part, pad + n:2 * pad + n, :] = jnp.zeros((pad, LANES), F32)
        nb = n // 8
        rb = min(nb, 128)
        for r in range(8):
            for i0 in range(0, nb, rb):
                for part in range(3):
                    lanes = slice(part * LANES, (part + 1) * LANES)
                    base = pad - CONV_K // 2 + r + 8 * i0
                    acc = cw_ref[0:1, lanes] * xp_ref[part, pl.ds(base, rb, stride=8), :]
                    for j in range(1, CONV_K):
                        acc = acc + cw_ref[j:j + 1, lanes] * xp_ref[part, pl.ds(base + j, rb, stride=8), :]
                    y = _silu(acc)
                    if part == 0:
                        y = y * lax.rsqrt(_split_dot(y * y, seg) + EPS) * (GDN_DK ** -0.5)
                    elif part == 1:
                        y = y * lax.rsqrt(_split_dot(y * y, seg) + EPS)
                    qkv_ref[part, pl.ds(off + r + 8 * i0, rb, stride=8), :] = y

    ba = ba_ref[...]
    bg_ref[:, 0:LANES] = _sigmoid(ba)
    sp_in = ba + hp_ref[1:2, :]
    softplus = jnp.maximum(sp_in, 0.0) + jnp.log1p(jnp.exp(-jnp.abs(sp_in)))
    bg_ref[:, LANES:2 * LANES] = hp_ref[0:1, :] * softplus

    rowi = lax.broadcasted_iota(jnp.int32, (CHUNK, LANES), 0)
    colj = lax.broadcasted_iota(jnp.int32, (CHUNK, LANES), 1) % HALF
    lo = lax.broadcasted_iota(jnp.int32, (CHUNK, LANES), 1) < HALF
    ti = lax.broadcasted_iota(jnp.int32, (CHUNK, CHUNK), 0)
    tj = lax.broadcasted_iota(jnp.int32, (CHUNK, CHUNK), 1)
    br = lax.broadcasted_iota(jnp.int32, (LANES, LANES), 0) // HALF
    bc = lax.broadcasted_iota(jnp.int32, (LANES, LANES), 1) // HALF
    blockmask = br == bc

    sub8 = rowi % 8
    nv = CHUNK // 8

    def chunk_cumsum(x, backward):
        for s in (1, 2, 4):
            if backward:
                x = x + jnp.where(sub8 < 8 - s, pltpu.roll(x, CHUNK - s, 0), 0.0)
            else:
                x = x + jnp.where(sub8 >= s, pltpu.roll(x, s, 0), 0.0)
        parts = [x[8 * j:8 * j + 8, :] for j in range(nv)]
        order = range(nv - 1, -1, -1) if backward else range(nv)
        edge = 0 if backward else 7
        run = None
        for j in order:
            if run is not None:
                parts[j] = parts[j] + run
            run = parts[j][edge:edge + 1, :]
        return jnp.concatenate(parts, axis=0)

    incl_dir = (colj <= rowi, colj >= rowi)
    strict_dir = (colj < rowi, colj > rowi)

    def each(f, *lists):
        return [f(*a) for a in zip(*lists)]

    def prepare(chains):
        ds = [d for _, d in chains]
        rows = [pl.ds(pl.multiple_of(c * CHUNK, CHUNK), CHUNK) for c, _ in chains]
        srows = [pl.ds(pl.multiple_of(c * LANES, LANES), LANES) for c, _ in chains]
        qn = [qkv_ref[0, r, :] for r in rows]
        kn = [qkv_ref[1, r, :] for r in rows]
        v = [qkv_ref[2, r, :] for r in rows]
        bx = [jnp.where(lo, bg_ref[r, 2 * d:2 * d + 1], bg_ref[r, 2 * d + 1:2 * d + 2]) for r, d in zip(rows, ds)]
        gcum = [chunk_cumsum(bg_ref[r, LANES:2 * LANES], d == 1) for r, d in zip(rows, ds)]
        gx = [jnp.where(lo, g[:, 4 + 2 * d:5 + 2 * d], g[:, 5 + 2 * d:6 + 2 * d]) for g, d in zip(gcum, ds)]
        eg = each(jnp.exp, gx)
        kb = each(jnp.multiply, kn, bx)
        vb = each(jnp.multiply, v, bx)
        kbg = each(jnp.multiply, kb, eg)
        grow = [jnp.sum(jnp.where(rowi == colj, g, 0.0), axis=0, keepdims=True) for g in gx]
        decay = [jnp.where(incl_dir[d], jnp.exp(jnp.where(incl_dir[d], g - gr, 0.0)), 0.0)
                 for g, gr, d in zip(gx, grow, ds)]
        kst = [_blk(_bf(k), lo) for k in kn]
        kq = [_dot_nt(jnp.concatenate([_bf(a), _bf(b)], axis=0), c) for a, b, c in zip(kb, qn, kst)]
        low = [jnp.where(strict_dir[d], x[0:CHUNK] * dc, 0.0) for x, dc, d in zip(kq, decay, ds)]
        intra = [jnp.where(incl_dir[d], x[CHUNK:2 * CHUNK] * dc, 0.0) for x, dc, d in zip(kq, decay, ds)]
        for r, x in zip(rows, zip(ds, qn, eg, intra)):
            d, q_, e_, i_ = x
            qg_ref[d, r, :] = _bf(q_ * e_)
            in_ref[d, r, :] = _bf(i_)
        lb = each(_bf, low)
        p = [_dot(a, _blk(a, lo)) for a in lb]
        nn = [-a for a in low]
        n_sq = int(math.log2(CHUNK)) - 1
        for r_ in range(n_sq):
            pb = each(_bf, p)
            pblk = [_blk(a, lo) for a in pb]
            if r_ < n_sq - 1:
                both = [_dot(jnp.concatenate([_bf(n_), b], axis=0), pk) for n_, b, pk in zip(nn, pb, pblk)]
                nn = [n_ + p_ + x[0:CHUNK] for n_, p_, x in zip(nn, p, both)]
                p = [x[CHUNK:2 * CHUNK] for x in both]
            else:
                nn = [n_ + p_ + _dot(_bf(n_), pk) for n_, p_, pk in zip(nn, p, pblk)]
        nb = each(_bf, nn)
        uw = [_dot(n_, jnp.concatenate([_blk(_bf(a), lo), _blk(_bf(b), lo)], axis=1))
              for n_, a, b in zip(nb, vb, kbg)]
        u = [a + x[:, 0:LANES] for a, x in zip(vb, uw)]
        w = [a + x[:, LANES:2 * LANES] for a, x in zip(kbg, uw)]
        g_last = [g[0:1, :] if d == 1 else g[CHUNK - 1:CHUNK, :] for g, d in zip(gx, ds)]
        kdec = [_bf(k * jnp.exp(gl - g)) for k, gl, g in zip(kn, g_last, gx)]
        wb = each(_bf, w)
        kr = [_dot_tn(a, jnp.concatenate([b, _bf(c)], axis=1)) for a, b, c in zip(kdec, wb, u)]
        for i, (c, d) in enumerate(chains):
            w_ref[d, rows[i], :] = wb[i]
            u_ref[d, rows[i], :] = u[i]
            k2_ref[d, srows[i], :] = _bf(jnp.where(blockmask, kr[i][:, 0:LANES], 0.0))
            r_ref[d, srows[i], :] = jnp.where(blockmask, kr[i][:, LANES:2 * LANES], 0.0)
            el_ref[d, pl.ds(pl.multiple_of(c * 8, 8), 8), :] = jnp.broadcast_to(jnp.exp(g_last[i]), (8, LANES))

    ncc = tc // CHUNK
    nct = tt // CHUNK
    ng = max(g for g in range(1, UNROLL_GDN + 1) if nct % g == 0)

    def group_chains(i):
        return [(i * ng + k, d) for k in range(ng) for d in range(2)]

    def body_a(i, carry):
        prepare(group_chains(i))
        return carry

    lax.fori_loop(0, nct // ng, body_a, 0)

    def advance(c, d, s):
        srows = pl.ds(pl.multiple_of(c * LANES, LANES), LANES)
        s16 = _bf(s)
        s_ref[d, srows, :] = s16
        e = el_ref[d, pl.ds(pl.multiple_of(c * 8, 8), 1), :]
        return s * e - _dot(k2_ref[d, srows, :], s16) + r_ref[d, srows, :]

    def body_b(i, carry):
        sf, sb = carry
        cb = jnp.where(i < ncc, ncc - 1 - i, nct - 1 - (i - ncc))
        return advance(i, 0, sf), advance(cb, 1, sb)

    zero = jnp.zeros((LANES, LANES), F32)
    lax.fori_loop(0, nct, body_b, (zero, zero))

    def emit(chains):
        rows = [pl.ds(pl.multiple_of(c * CHUNK, CHUNK), CHUNK) for c, _ in chains]
        s16 = [s_ref[d, pl.ds(pl.multiple_of(c * LANES, LANES), LANES), :] for c, d in chains]
        ws = [_dot(w_ref[d, r, :], s) for (_, d), r, s in zip(chains, rows, s16)]
        qs = [_dot(qg_ref[d, r, :], s) for (_, d), r, s in zip(chains, rows, s16)]
        v_new = [u_ref[d, r, :] - a for (_, d), r, a in zip(chains, rows, ws)]
        iv = [_dot(in_ref[d, r, :], _blk(_bf(vn), lo)) for (_, d), r, vn in zip(chains, rows, v_new)]
        for (_, d), r, a, b in zip(chains, rows, qs, iv):
            (ob_ref if d == 1 else of_ref)[r, :] = a + b

    def body_c(i, carry):
        emit(group_chains(i))
        return carry

    lax.fori_loop(0, nct // ng, body_c, 0)

    o = of_ref[...] + ob_ref[...]
    ss = _split_dot(o * o, seg)
    o = o * lax.rsqrt(ss * (1.0 / GDN_DK) + EPS) * nw_ref[...]
    o_ref[...] = _bf(o * _silu(z_ref[...]))


def _gdn(gqkv, gz, gba, conv_p, hp, nw, *, B, TT, tc):
    NT = gqkv.shape[0]
    t = TT - tc
    w3 = 3 * LANES
    nch = TT // CHUNK
    kern = functools.partial(_gdn_kernel, tc=tc, t=t)
    return pl.pallas_call(
        kern,
        grid=(B, GDN_PAIRS),
        in_specs=[pl.BlockSpec((TT, w3), lambda b, p: (b, p)),
                  pl.BlockSpec((TT, LANES), lambda b, p: (b, p)),
                  pl.BlockSpec((TT, LANES), lambda b, p: (b, p)),
                  pl.BlockSpec((None, 8, w3), lambda b, p: (p, 0, 0)),
                  pl.BlockSpec((None, 8, LANES), lambda b, p: (p, 0, 0)),
                  pl.BlockSpec((1, LANES), lambda b, p: (0, 0))],
        out_specs=pl.BlockSpec((TT, LANES), lambda b, p: (b, p)),
        out_shape=jax.ShapeDtypeStruct((NT, GDN_W), BF16),
        scratch_shapes=[pltpu.VMEM((3, t + 16, LANES), F32), pltpu.VMEM((3, TT, LANES), F32),
                        pltpu.VMEM((TT, 2 * LANES), F32), pltpu.VMEM((TT, LANES), F32),
                        pltpu.VMEM((TT, LANES), F32),
                        pltpu.VMEM((2, TT, LANES), BF16), pltpu.VMEM((2, TT, LANES), BF16),
                        pltpu.VMEM((2, TT, LANES), BF16), pltpu.VMEM((2, TT, LANES), F32),
                        pltpu.VMEM((2, nch * LANES, LANES), BF16), pltpu.VMEM((2, nch * LANES, LANES), F32),
                        pltpu.VMEM((2, nch * 8, LANES), F32), pltpu.VMEM((2, nch * LANES, LANES), BF16)],
        compiler_params=_cparams(("parallel", "parallel")),
        name="gdn",
    )(gqkv, gz, gba, conv_p, hp, nw)


def _att_tile(q_ref, k_ref, v_ref, o_ref, nk):
    lo = lax.broadcasted_iota(jnp.int32, (1, LANES), 1) < HALF
    k = k_ref[0:nk, :]
    v = v_ref[0:nk, :]

    def scores(h):
        j, half = h // 2, h % 2
        qj = q_ref[:, j * LANES:(j + 1) * LANES]
        zq = jnp.zeros_like(qj)
        return _dot_nt(jnp.where(lo, qj, zq) if half == 0 else jnp.where(lo, zq, qj), k)

    outs = []
    s = scores(0)
    for h in range(ATT_HEADS):
        s_next = scores(h + 1) if h + 1 < ATT_HEADS else None
        m = jnp.max(s, axis=-1, keepdims=True)
        p = jnp.exp(s - m)
        l = jnp.sum(p, axis=-1, keepdims=True)
        outs.append(_dot(_bf(p), v) / l)
        s = s_next
    for j in range(ATT_GROUP):
        o_ref[:, j * LANES:(j + 1) * LANES] = _bf(jnp.where(lo, outs[2 * j], outs[2 * j + 1]))


def _att_kernel(q_ref, k_ref, v_ref, o_ref, *, n_ctx_tiles, tc, tt):
    is_ctx = pl.program_id(1) < n_ctx_tiles

    @pl.when(is_ctx)
    def _():
        _att_tile(q_ref, k_ref, v_ref, o_ref, tc)

    @pl.when(jnp.logical_not(is_ctx))
    def _():
        _att_tile(q_ref, k_ref, v_ref, o_ref, tt)


def _attention(aq, ak, av, *, B, TT, tc):
    NT = aq.shape[0]
    tq = 256
    nq = TT // tq
    kv_spec = pl.BlockSpec((TT, LANES), lambda b, i: (b, 0))
    q_spec = pl.BlockSpec((tq, ATT_W), lambda b, i: (b * nq + i, 0))
    return pl.pallas_call(
        functools.partial(_att_kernel, n_ctx_tiles=tc // tq, tc=tc, tt=TT),
        grid=(B, nq),
        in_specs=[q_spec, kv_spec, kv_spec],
        out_specs=q_spec,
        out_shape=jax.ShapeDtypeStruct((NT, ATT_W), BF16),
        compiler_params=_cparams(("parallel", "parallel")),
        name="attention",
    )(aq, ak, av)


S5_BB = 2
GPV = LANES // S5_GH


def _s5_kernel(u0_ref, u1_ref, tg_ref, bp_ref, cp_ref, la_ref, lb_ref, d_ref, gw_ref, gb_ref, o_ref,
               uf_ref, sloc_ref, sin_ref, y0_ref, y1_ref, *, bb, ncc, nct):
    lc = S5_CHUNK
    nr = bb * nct
    w2 = 2 * S5_P
    blk8 = lax.broadcasted_iota(jnp.int32, (1, LANES), 1) // S5_GH
    u_refs = (u0_ref, u1_ref)
    y_refs = (y0_ref, y1_ref)

    def block_transpose(xs):
        xs = list(xs)
        for s in (4, 2, 1):
            keep = (blk8 & s) == 0
            for a in range(GPV):
                if a & s == 0:
                    lo_, hi_ = xs[a], xs[a + s]
                    xs[a] = jnp.where(keep, lo_, pltpu.roll(hi_, s * S5_GH, 1))
                    xs[a + s] = jnp.where(keep, pltpu.roll(lo_, LANES - s * S5_GH, 1), hi_)
        return xs

    rbk = 32
    for a in range(S5_GROUPS // GPV):
        for v in range(lc // GPV):
            for r0 in range(0, nr, rbk):
                xs = [u_refs[a][pl.ds((v * GPV + k) + lc * r0, rbk, stride=lc), :] for k in range(GPV)]
                for k, y_ in enumerate(block_transpose(xs)):
                    uf_ref[a * GPV + k, r0:r0 + rbk, v * LANES:(v + 1) * LANES] = y_
    for g in range(S5_GROUPS):
        sl = _dot(_bf(uf_ref[g]), bp_ref[g])
        sloc_ref[2 * g] = sl[:, 0:w2]
        sloc_ref[2 * g + 1] = sl[:, w2:2 * w2]

    def body(i, carry):
        cb = jnp.where(i < ncc, ncc - 1 - i, nct - 1 - (i - ncc))
        out = []
        for g in range(S5_GROUPS):
            for d in range(2):
                s = carry[2 * g + d]
                rows = pl.ds(i if d == 0 else cb, bb, stride=nct)
                lanes = slice(d * w2, (d + 1) * w2)
                sin_ref[2 * g + d, rows, :] = s
                out.append(s * la_ref[g, :, lanes] + pltpu.roll(s, S5_P, 1) * lb_ref[g, :, lanes]
                           + sloc_ref[2 * g + d, rows, :])
        return tuple(out)

    zero = jnp.zeros((bb, w2), F32)
    lax.fori_loop(0, nct, body, (zero,) * (2 * S5_GROUPS))

    for g in range(S5_GROUPS):
        uf_ref[g] = (_dot(_bf(uf_ref[g]), tg_ref[g])
                     + _dot(_bf(sin_ref[2 * g]), cp_ref[g, 0:w2, :])
                     + _dot(_bf(sin_ref[2 * g + 1]), cp_ref[g, w2:2 * w2, :]))

    for a in range(S5_GROUPS // GPV):
        for v in range(lc // GPV):
            for r0 in range(0, nr, rbk):
                xs = [uf_ref[a * GPV + k, r0:r0 + rbk, v * LANES:(v + 1) * LANES] for k in range(GPV)]
                for k, y_ in enumerate(block_transpose(xs)):
                    y_refs[a][pl.ds((v * GPV + k) + lc * r0, rbk, stride=lc), :] = y_

    y = jnp.concatenate([y0_ref[...], y1_ref[...]], axis=1)
    u = jnp.concatenate([u0_ref[...], u1_ref[...]], axis=1)
    zz = _gelu_tanh(y + d_ref[...] * u)
    o_ref[...] = _bf(zz * _sigmoid(_dot(_bf(zz), gw_ref[...]) + gb_ref[...]))


def _s5(su, tg, bp, cp, la, lb, d, gw, gb, *, B, TT, tc):
    NT, W = su.shape
    bb = S5_BB
    nct = TT // S5_CHUNK
    nr = bb * nct
    kern = functools.partial(_s5_kernel, bb=bb, ncc=tc // S5_CHUNK, nct=nct)
    full = lambda a: pl.BlockSpec(a.shape, lambda i: (0,) * a.ndim)
    return pl.pallas_call(
        kern,
        grid=(B // bb,),
        in_specs=[pl.BlockSpec((bb * TT, LANES), lambda i: (i, 0)),
                  pl.BlockSpec((bb * TT, LANES), lambda i: (i, 1)),
                  full(tg), full(bp), full(cp), full(la), full(lb), full(d), full(gw), full(gb)],
        out_specs=pl.BlockSpec((bb * TT, W), lambda i: (i, 0)),
        out_shape=jax.ShapeDtypeStruct((NT, W), BF16),
        scratch_shapes=[pltpu.VMEM((S5_GROUPS, nr, W), F32), pltpu.VMEM((2 * S5_GROUPS, nr, LANES), F32),
                        pltpu.VMEM((2 * S5_GROUPS, nr, LANES), F32),
                        pltpu.VMEM((bb * TT, LANES), F32), pltpu.VMEM((bb * TT, LANES), F32)],
        compiler_params=_cparams(("parallel",)),
        name="s5",
    )(su, su, tg, bp, cp, la, lb, d, gw, gb)


def _s5_matrices(lam_re, lam_im, log_dt, b_re, b_im, c_re, c_im):
    Lc = S5_CHUNK
    lam = lax.complex(lam_re.astype(F32), lam_im.astype(F32))
    dt = jnp.exp(log_dt.astype(F32))[..., None]
    lam_dt = lam * dt
    lam_bar = jnp.exp(lam_dt)
    b_bar = ((lam_bar - 1.0) / lam)[..., None] * lax.complex(b_re.astype(F32), b_im.astype(F32))
    cc = lax.complex(c_re.astype(F32), c_im.astype(F32))
    ks = jnp.arange(Lc + 1, dtype=F32)
    pw = jnp.exp(lam_dt[..., None, :] * ks[:, None])
    kern = jnp.real(jnp.einsum('dghp,dgkp,dgpi->dgkhi', cc, pw[:, :, :Lc], b_bar))
    i_idx = jnp.arange(Lc)[:, None]
    j_idx = jnp.arange(Lc)[None, :]
    lag = i_idx - j_idx
    kf = kern[0][:, jnp.clip(lag, 0, Lc - 1)]
    kb = kern[1][:, jnp.clip(-lag, 0, Lc - 1)]
    tmat = (jnp.where((lag >= 0)[None, :, :, None, None], kf, 0.0)
            + jnp.where((lag <= 0)[None, :, :, None, None], kb, 0.0))
    tg = tmat.transpose(0, 2, 4, 1, 3).reshape(S5_GROUPS, Lc * S5_GH, Lc * S5_GH)
    pf = pw[0][:, Lc - 1 - jnp.arange(Lc)]
    pb = pw[1][:, jnp.arange(Lc)]
    sf = jnp.einsum('gjp,gph->gjhp', pf, b_bar[0])
    sb = jnp.einsum('gjp,gph->gjhp', pb, b_bar[1])
    bp = jnp.concatenate([jnp.real(sf), jnp.imag(sf), jnp.real(sb), jnp.imag(sb)], axis=-1)
    bp = bp.reshape(S5_GROUPS, Lc * S5_GH, 4 * S5_P)
    qf = jnp.einsum('ghp,gip->gpih', cc[0], pw[0][:, 1 + jnp.arange(Lc)])
    qb = jnp.einsum('ghp,gip->gpih', cc[1], pw[1][:, Lc - jnp.arange(Lc)])
    cp = jnp.concatenate([jnp.real(qf), -jnp.imag(qf), jnp.real(qb), -jnp.imag(qb)], axis=1)
    cp = cp.reshape(S5_GROUPS, 4 * S5_P, Lc * S5_GH)
    lc = pw[:, :, Lc]
    la = jnp.concatenate([jnp.real(lc[0]), jnp.real(lc[0]), jnp.real(lc[1]), jnp.real(lc[1])], axis=-1)
    lb = jnp.concatenate([-jnp.imag(lc[0]), jnp.imag(lc[0]), -jnp.imag(lc[1]), jnp.imag(lc[1])], axis=-1)
    return (tg.astype(BF16), bp.astype(BF16), cp.astype(BF16),
            la.reshape(S5_GROUPS, 1, 4 * S5_P), lb.reshape(S5_GROUPS, 1, 4 * S5_P))


def _gelu_tanh(x):
    return 0.5 * x * (1.0 + jnp.tanh(math.sqrt(2.0 / math.pi) * (x + 0.044715 * (x * x * x))))


def _kout_kernel(x_ref, mod_c_ref, mod_b_ref, go_ref, ao_ref, so_ref,
                 wg_ref, wa_ref, ws_ref, g_ref, b_ref, o_ref, *, tm, tc, tpb):
    rm = functools.partial(_row_mod, mod_c_ref, mod_b_ref, _ctx_rows(tm, tc, tpb))
    y = _dot(go_ref[...], wg_ref[...]) + _dot(ao_ref[...], wa_ref[...]) + _dot(so_ref[...], ws_ref[...])
    z = ALPHA * x_ref[...] + rm(5) * y
    o_ref[...] = _layer_norm(z, g_ref[...], b_ref[...])


def _kout(xa, mod, go, ao, so, wg, wa, ws, g, b, *, B, TT, tc):
    NT, D = xa.shape
    tpb = 4
    tm = TT // tpb
    kern = functools.partial(_kout_kernel, tm=tm, tc=tc, tpb=tpb)
    row = lambda w: pl.BlockSpec((tm, w), lambda i: (i, 0))
    full = lambda a: pl.BlockSpec(a.shape, lambda i: (0,) * a.ndim)
    return pl.pallas_call(
        kern,
        grid=(NT // tm,),
        in_specs=[row(D),
                  pl.BlockSpec((None, N_MOD, D), lambda i: (B, 0, 0)),
                  pl.BlockSpec((None, N_MOD, D), lambda i: (i // tpb, 0, 0)),
                  row(GDN_W), row(ATT_W), row(S5_W),
                  full(wg), full(wa), full(ws), full(g), full(b)],
        out_specs=row(D),
        out_shape=jax.ShapeDtypeStruct((NT, D), F32),
        compiler_params=_cparams(("parallel",)),
        name="mixer_out",
    )(xa, mod, mod, go, ao, so, wg, wa, ws, g, b)


def _perm_in_cols():
    off_z = GDN_QKV
    off_b = off_z + GDN_W
    off_a = off_b + 2 * GDN_HEADS
    off_q = off_a + 2 * GDN_HEADS
    off_k = off_q + ATT_W
    off_v = off_k + ATT_KV_HEADS * ATT_DH
    off_u = off_v + ATT_KV_HEADS * ATT_DH
    cols = []
    for p in range(GDN_PAIRS):
        for part in range(3):
            base = part * GDN_W + 2 * p * GDN_DK
            cols += list(range(base, base + 2 * GDN_DK))
    cols += list(range(off_z, off_z + GDN_W))
    for p in range(GDN_PAIRS):
        blk = []
        for off in (off_b, off_a):
            for d in range(2):
                blk += [off + d * GDN_HEADS + 2 * p, off + d * GDN_HEADS + 2 * p + 1]
        cols += blk + [-1] * (LANES - len(blk))
    for j in range(ATT_GROUP):
        for h in (j, j + ATT_GROUP):
            cols += list(range(off_q + h * ATT_DH, off_q + (h + 1) * ATT_DH))
    cols += list(range(off_k, off_k + 2 * ATT_DH))
    cols += list(range(off_v, off_v + 2 * ATT_DH))
    cols += list(range(off_u, off_u + S5_W))
    return np.asarray(cols, dtype=np.int32)


def _gdn_pair_cols():
    cols = []
    for p in range(GDN_PAIRS):
        for part in range(3):
            base = part * GDN_W + 2 * p * GDN_DK
            cols += list(range(base, base + 2 * GDN_DK))
    return np.asarray(cols, dtype=np.int32)


def _att_out_rows():
    rows = []
    for j in range(ATT_GROUP):
        for h in (j, j + ATT_GROUP):
            rows += list(range(h * ATT_DH, (h + 1) * ATT_DH))
    return np.asarray(rows, dtype=np.int32)


def _take(w, idx, axis):
    idx = [int(i) for i in idx]
    parts = []
    i = 0
    while i < len(idx):
        j = i + 1
        if idx[i] < 0:
            while j < len(idx) and idx[j] < 0:
                j += 1
            shape = list(w.shape)
            shape[axis] = j - i
            parts.append(jnp.zeros(shape, w.dtype))
        else:
            while j < len(idx) and idx[j] == idx[j - 1] + 1:
                j += 1
            parts.append(lax.slice_in_dim(w, idx[i], idx[j - 1] + 1, axis=axis))
        i = j
    return jnp.concatenate(parts, axis=axis)


def _rope_tables(tc, t):
    rows = t // GRID_W
    row = jnp.repeat(jnp.arange(rows), GRID_W)
    col = jnp.tile(jnp.arange(GRID_W), rows)
    inv_freq = ROPE_THETA ** (-jnp.arange(ROPE_PAIRS, dtype=F32) / ROPE_PAIRS)
    ang = jnp.stack([row, col], axis=-1).astype(F32)[..., None] * inv_freq
    cos, sin = jnp.cos(ang), jnp.sin(ang)
    zero = jnp.zeros_like(sin)
    cos_h = jnp.stack([cos, cos], axis=2).reshape(t, ATT_DH)
    sa_h = jnp.stack([-sin, zero], axis=2).reshape(t, ATT_DH)
    sb_h = jnp.stack([zero, sin], axis=2).reshape(t, ATT_DH)
    def full(tab, fill):
        tab = jnp.concatenate([jnp.full((tc, ATT_DH), fill, F32), tab], axis=0)
        return jnp.concatenate([tab, tab], axis=1)
    return full(cos_h, 1.0), full(sa_h, 0.0), full(sb_h, 0.0)


def kernel(x, c, ctx, c_ctx, w_ada, b_ada, ln_g, ln_b, ffn_w1, ffn_w3, ffn_w2, w_in, w_out, gdn_conv_w,
           gdn_a_log, gdn_dt_bias, gdn_norm_w, q_norm_w, k_norm_w, s5_lam_re, s5_lam_im, s5_log_dt,
           s5_b_re, s5_b_im, s5_c_re, s5_c_im, s5_d, glu_w, glu_b):
    B, T, D = x.shape
    tc = ctx.shape[1]
    TT = tc + T
    NT = B * TT
    L = w_ada.shape[0]
    kw = dict(B=B, TT=TT, tc=tc)

    R = ((B + 1 + 7) // 8) * 8
    c_all = jnp.concatenate([c, c_ctx[None, :], jnp.zeros((R - B - 1, D), F32)], axis=0)
    mod_all = _modulation(c_all, w_ada, b_ada).reshape(L, R, N_MOD, D)

    in_cols = _perm_in_cols()
    pair_cols = _gdn_pair_cols()
    att_rows = _att_out_rows()
    cos, sa, sb = _rope_tables(tc, T)

    def head_params(v):
        v = v.astype(F32).reshape(2, GDN_PAIRS, 2).transpose(1, 0, 2).reshape(GDN_PAIRS, 4)
        return jnp.pad(v, ((0, 0), (4, LANES - 8)))

    xa = jnp.concatenate([ctx, x], axis=1).reshape(NT, D)
    for l in range(L):
        mod = mod_all[l]
        w2 = _bf(ffn_w2[l])
        g_l, b_l = ln_g[l], ln_b[l]
        xa = _ffn(xa, mod, _ffn_pack_w13(ffn_w1[l, 0], ffn_w3[l, 0]), w2[0], g_l[0:1], b_l[0:1], sub=0, **kw)

        w_in_p = _take(_bf(w_in[l]), in_cols, 1)
        qw = jnp.tile(q_norm_w[l], 2)[None, :]
        kwt = jnp.tile(k_norm_w[l], 2)[None, :]
        gqkv, gz, gba, aq, ak, av, su = _kin(xa, mod, w_in_p, qw, kwt, cos, sa, sb, **kw)

        conv_p = jnp.pad(_take(gdn_conv_w[l], pair_cols, 1), ((0, 8 - CONV_K), (0, 0)))
        conv_p = conv_p.reshape(8, GDN_PAIRS, 3 * LANES).transpose(1, 0, 2)
        hp = jnp.stack([head_params(-jnp.exp(gdn_a_log[l].astype(F32))), head_params(gdn_dt_bias[l])], axis=1)
        hp = jnp.pad(hp, ((0, 0), (0, 6), (0, 0)))
        nw = jnp.tile(gdn_norm_w[l], 2)[None, :]
        go = _gdn(gqkv, gz, gba, conv_p, hp, nw, **kw)

        ao = _attention(aq, ak, av, **kw)

        tg, bp, cp, la, lb = _s5_matrices(s5_lam_re[l], s5_lam_im[l], s5_log_dt[l], s5_b_re[l], s5_b_im[l],
                                          s5_c_re[l], s5_c_im[l])
        so = _s5(su, tg, bp, cp, la, lb, s5_d[l][None, :], _bf(glu_w[l]), glu_b[l][None, :], **kw)

        wo = _bf(w_out[l])
        wg = wo[0:GDN_W]
        wa = _take(wo[GDN_W:GDN_W + ATT_W], att_rows, 0)
        ws = wo[GDN_W + ATT_W:]
        xa = _kout(xa, mod, go, ao, so, wg, wa, ws, g_l[1:2], b_l[1:2], **kw)
        xa = _ffn(xa, mod, _ffn_pack_w13(ffn_w1[l, 1], ffn_w3[l, 1]), w2[1], g_l[2:3], b_l[2:3], sub=2, **kw)
    return xa.reshape(B, TT, D)[:, tc:, :]
```

```python
import functools
import math

import numpy as np
import jax
import jax.numpy as jnp
from jax import lax
from jax.experimental import pallas as pl
from jax.experimental.pallas import tpu as pltpu

F32 = jnp.float32
BF16 = jnp.bfloat16

D_MODEL = 1024
GRID_W = 64
GDN_HEADS = 6
GDN_DK = 64
GDN_PAIRS = GDN_HEADS // 2
GDN_QKV = 3 * GDN_HEADS * GDN_DK
GDN_W = GDN_HEADS * GDN_DK
CONV_K = 5
CHUNK = 64
UNROLL_GDN = 9
ATT_HEADS = 6
ATT_KV_HEADS = 2
ATT_GROUP = ATT_HEADS // ATT_KV_HEADS
ATT_DH = 64
ATT_W = ATT_HEADS * ATT_DH
ROPE_THETA = 10000.0
ROPE_PAIRS = ATT_DH // 4
S5_GROUPS = 16
S5_GH = 16
S5_P = 64
S5_W = S5_GROUPS * S5_GH
S5_CHUNK = 16
N_MOD = 9
DEPTH = 4
ALPHA = (2.0 * DEPTH) ** 0.25
EPS = 1e-6
LANES = 128
HALF = LANES // 2

C_QKV = 0
C_Z = C_QKV + GDN_QKV
C_BA = C_Z + GDN_W
C_AQ = C_BA + GDN_PAIRS * LANES
C_AK = C_AQ + ATT_W
C_AV = C_AK + LANES
C_SU = C_AV + LANES
C_END = C_SU + S5_W

FFN_FINISH_STEPS = 4
VMEM_LIMIT = 56 * 1024 * 1024


def _cparams(sem):
    return pltpu.CompilerParams(dimension_semantics=sem, vmem_limit_bytes=VMEM_LIMIT)


def _bf(x):
    return x.astype(BF16)


def _dot(a, b):
    return jnp.dot(a, b, preferred_element_type=F32)


def _dot_nt(a, b):
    return lax.dot_general(a, b, (((1,), (1,)), ((), ())), preferred_element_type=F32)


def _dot_tn(a, b):
    return lax.dot_general(a, b, (((0,), (0,)), ((), ())), preferred_element_type=F32)


def _split_dot(x, m):
    hi = _bf(x)
    lo = _bf(x - hi.astype(F32))
    return _dot(hi, m) + _dot(lo, m)


def _split3_dot_left(m, x):
    hi = _bf(x)
    r1 = x - hi.astype(F32)
    mid = _bf(r1)
    lo = _bf(r1 - mid.astype(F32))
    return _dot(m, hi) + _dot(m, mid) + _dot(m, lo)


def _sigmoid(x):
    return 0.5 + 0.5 * jnp.tanh(0.5 * x)


def _silu(x):
    h = 0.5 * x
    return h + h * jnp.tanh(h)


def _layer_norm(z, g, b):
    mu = jnp.mean(z, axis=-1, keepdims=True)
    zc = z - mu
    var = jnp.mean(zc * zc, axis=-1, keepdims=True)
    return zc * lax.rsqrt(var + EPS) * g + b


def _ctx_rows(tm, tc, tiles_per_batch):
    row0 = (pl.program_id(0) % tiles_per_batch) * tm
    return row0 + lax.broadcasted_iota(jnp.int32, (tm, 1), 0) < tc


def _row_mod(mod_c_ref, mod_b_ref, is_ctx, idx):
    return jnp.where(is_ctx, mod_c_ref[idx:idx + 1, :], mod_b_ref[idx:idx + 1, :])


def _mod_kernel(c_ref, w_ref, b_ref, o_ref):
    sc = _silu(c_ref[...])
    o_ref[...] = jnp.dot(sc, w_ref[...], preferred_element_type=F32,
                         precision=lax.Precision.HIGHEST) + b_ref[...]


def _modulation(c_all, w_ada, b_ada):
    L, D, NM = w_ada.shape
    R = c_all.shape[0]
    tn = 1152
    return pl.pallas_call(
        _mod_kernel,
        grid=(L, NM // tn),
        in_specs=[pl.BlockSpec((R, D), lambda l, j: (0, 0)),
                  pl.BlockSpec((None, D, tn), lambda l, j: (l, 0, j)),
                  pl.BlockSpec((None, 1, tn), lambda l, j: (l, 0, j))],
        out_specs=pl.BlockSpec((None, R, tn), lambda l, j: (l, 0, j)),
        out_shape=jax.ShapeDtypeStruct((L, R, NM), F32),
        compiler_params=_cparams(("parallel", "parallel")),
        name="adaln_mod",
    )(c_all, w_ada, b_ada.reshape(L, 1, NM))


def _ffn_kernel(x_ref, mod_c_ref, mod_b_ref, modp_b_ref, w13_ref, w2_ref, g_ref, b_ref, o_ref,
                hm_ref, acc_ref, xk_ref, *, sub, tm, tc, tpb, nf, nt):
    i = pl.program_id(0)
    f = pl.program_id(1)
    slot = i % 2
    row = lax.broadcasted_iota(jnp.int32, (tm, 1), 0)
    rm = functools.partial(_row_mod, mod_c_ref, mod_b_ref, (i % tpb) * tm + row < tc)
    qr = tm // FFN_FINISH_STEPS

    def modulate():
        x = x_ref[...]
        xk_ref[slot] = x
        hm_ref[...] = _bf(x * (1.0 + rm(3 * sub + 1)) + rm(3 * sub))

    def partial_out():
        hm = hm_ref[...]
        h = _dot(hm, w13_ref[...])
        return _dot(_bf(_silu(h[:, 0:FFN_FC]) * h[:, FFN_FC:2 * FFN_FC]), w2_ref[...])

    def finish_previous():
        rg = 32
        for k in range(qr // rg):
            r0 = pl.multiple_of(f * qr + k * rg, 8)
            rows = pl.ds(r0, rg)
            prev_row = ((i + tpb - 1) % tpb) * tm + r0 + lax.broadcasted_iota(jnp.int32, (rg, 1), 0)
            gate = _row_mod(mod_c_ref, modp_b_ref, prev_row < tc, 3 * sub + 2)
            z = ALPHA * xk_ref[1 - slot, rows, :] + 0.5 * gate * acc_ref[1 - slot, rows, :]
            o_ref[rows, :] = _layer_norm(z, g_ref[...], b_ref[...])

    do_mm = i < nt
    do_fin = (i > 0) & (f < FFN_FINISH_STEPS)
    first = f == 0

    @pl.when(first & do_mm & jnp.logical_not(do_fin))
    def _():
        modulate()
        acc_ref[slot] = partial_out()

    @pl.when(first & do_mm & do_fin)
    def _():
        finish_previous()
        modulate()
        acc_ref[slot] = partial_out()

    @pl.when(jnp.logical_not(first) & do_mm & do_fin)
    def _():
        finish_previous()
        acc_ref[slot] += partial_out()

    @pl.when(jnp.logical_not(first) & do_mm & jnp.logical_not(do_fin))
    def _():
        acc_ref[slot] += partial_out()

    @pl.when(jnp.logical_not(do_mm) & do_fin)
    def _():
        finish_previous()


FFN_FC = 256


def _ffn_pack_w13(w1, w3):
    D, F = w1.shape
    nf = F // FFN_FC
    w = jnp.concatenate([_bf(w1).reshape(D, nf, FFN_FC), _bf(w3).reshape(D, nf, FFN_FC)], axis=2)
    return w.transpose(1, 0, 2)


def _ffn(xa, mod, w13, w2, g, b, *, sub, B, TT, tc):
    NT, D = xa.shape
    nf = w13.shape[0]
    fc = FFN_FC
    tpb = 2
    tm = TT // tpb
    nt = NT // tm
    kern = functools.partial(_ffn_kernel, sub=sub, tm=tm, tc=tc, tpb=tpb, nf=nf, nt=nt)
    cur = lambda i: jnp.minimum(i, nt - 1)
    prev = lambda i: jnp.maximum(i - 1, 0)
    return pl.pallas_call(
        kern,
        grid=(nt + 1, nf),
        in_specs=[pl.BlockSpec((tm, D), lambda i, f: (cur(i + jnp.minimum(f, 1)), 0)),
                  pl.BlockSpec((None, N_MOD, D), lambda i, f: (B, 0, 0)),
                  pl.BlockSpec((None, N_MOD, D), lambda i, f: (cur(i) // tpb, 0, 0)),
                  pl.BlockSpec((None, N_MOD, D), lambda i, f: (prev(i) // tpb, 0, 0)),
                  pl.BlockSpec((None, D, 2 * fc), lambda i, f: (f, 0, 0)),
                  pl.BlockSpec((fc, D), lambda i, f: (f, 0)),
                  pl.BlockSpec((1, D), lambda i, f: (0, 0)),
                  pl.BlockSpec((1, D), lambda i, f: (0, 0))],
        out_specs=pl.BlockSpec((tm, D), lambda i, f: (prev(i), 0)),
        out_shape=jax.ShapeDtypeStruct((NT, D), F32),
        scratch_shapes=[pltpu.VMEM((tm, D), BF16), pltpu.VMEM((2, tm, D), F32), pltpu.VMEM((2, tm, D), F32)],
        compiler_params=_cparams(("arbitrary", "arbitrary")),
        name=f"ffn{sub}",
    )(xa, mod, mod, mod, w13, w2, g, b)


def _seg_ones():
    r = lax.broadcasted_iota(jnp.int32, (LANES, LANES), 0) // HALF
    c = lax.broadcasted_iota(jnp.int32, (LANES, LANES), 1) // HALF
    return jnp.where(r == c, 1.0, 0.0).astype(BF16)


def _kin_kernel(x_ref, mod_c_ref, mod_b_ref, w_ref, qw_ref, kw_ref, cos_ref, sa_ref, sb_ref,
                gqkv_ref, gz_ref, gba_ref, aq_ref, ak_ref, av_ref, su_ref, *, tm, tc, tpb):
    rm = functools.partial(_row_mod, mod_c_ref, mod_b_ref, _ctx_rows(tm, tc, tpb))
    hm = _bf(x_ref[...] * (1.0 + rm(4)) + rm(3))
    r = _dot(hm, w_ref[...])
    gqkv_ref[...] = r[:, C_QKV:C_Z]
    gz_ref[...] = r[:, C_Z:C_BA]
    gba_ref[...] = r[:, C_BA:C_AQ]
    av_ref[...] = _bf(r[:, C_AV:C_SU])
    su_ref[...] = r[:, C_SU:C_END]

    seg = _seg_ones()
    cos, sa, sb = cos_ref[...], sa_ref[...], sb_ref[...]

    def norm_rope(blk, w_row):
        ss = _split_dot(blk * blk, seg)
        y = blk * lax.rsqrt(ss * (1.0 / ATT_DH) + EPS) * w_row
        return y * cos + pltpu.roll(y, LANES - ROPE_PAIRS, 1) * sa + pltpu.roll(y, ROPE_PAIRS, 1) * sb

    qw = qw_ref[...] * (ATT_DH ** -0.5)
    for j in range(ATT_W // LANES):
        aq_ref[:, j * LANES:(j + 1) * LANES] = _bf(
            norm_rope(r[:, C_AQ + j * LANES:C_AQ + (j + 1) * LANES], qw))
    ak_ref[...] = _bf(norm_rope(r[:, C_AK:C_AV], kw_ref[...]))


def _kin(xa, mod, w_in_p, qw, kw, cos, sa, sb, *, B, TT, tc):
    NT, D = xa.shape
    tpb = 4
    tm = TT // tpb
    kern = functools.partial(_kin_kernel, tm=tm, tc=tc, tpb=tpb)
    widths = [(GDN_QKV, F32), (GDN_W, F32), (GDN_PAIRS * LANES, F32), (ATT_W, BF16),
              (LANES, BF16), (LANES, BF16), (S5_W, F32)]
    return pl.pallas_call(
        kern,
        grid=(NT // tm,),
        in_specs=[pl.BlockSpec((tm, D), lambda i: (i, 0)),
                  pl.BlockSpec((None, N_MOD, D), lambda i: (B, 0, 0)),
                  pl.BlockSpec((None, N_MOD, D), lambda i: (i // tpb, 0, 0)),
                  pl.BlockSpec((D, C_END), lambda i: (0, 0)),
                  pl.BlockSpec((1, LANES), lambda i: (0, 0)),
                  pl.BlockSpec((1, LANES), lambda i: (0, 0)),
                  pl.BlockSpec((tm, LANES), lambda i: (i % tpb, 0)),
                  pl.BlockSpec((tm, LANES), lambda i: (i % tpb, 0)),
                  pl.BlockSpec((tm, LANES), lambda i: (i % tpb, 0))],
        out_specs=[pl.BlockSpec((tm, w), lambda i: (i, 0)) for w, _ in widths],
        out_shape=[jax.ShapeDtypeStruct((NT, w), dt) for w, dt in widths],
        compiler_params=_cparams(("parallel",)),
        name="mixer_in",
    )(xa, mod, mod, w_in_p, qw, kw, cos, sa, sb)


def _blk(x, lo):
    z = jnp.zeros_like(x)
    return jnp.concatenate([jnp.where(lo, x, z), jnp.where(lo, z, x)], axis=0)


def _gdn_kernel(qkv_in_ref, z_ref, ba_ref, cw_ref, hp_ref, nw_ref, o_ref,
                xp_ref, qkv_ref, bg_ref, of_ref, ob_ref,
                qg_ref, in_ref, w_ref, u_ref, k2_ref, r_ref, el_ref, s_ref, *, tc, t):
    tt = tc + t
    seg = _seg_ones()

    pad = 8
    for off, n in ((0, tc), (tc, t)):
        for part in range(3):
            xp_ref[part, 0:pad, :] = jnp.zeros((pad, LANES), F32)
            xp_ref[part, pad:pad + n, :] = qkv_in_ref[off:off + n, part * LANES:(part + 1) * LANES]
            xp_ref[part, pad + n:2 * pad + n, :] = jnp.zeros((pad, LANES), F32)
        nb = n // 8
        rb = min(nb, 128)
        for r in range(8):
            for i0 in range(0, nb, rb):
                for part in range(3):
                    lanes = slice(part * LANES, (part + 1) * LANES)
                    base = pad - CONV_K // 2 + r + 8 * i0
                    acc = cw_ref[0:1, lanes] * xp_ref[part, pl.ds(base, rb, stride=8), :]
                    for j in range(1, CONV_K):
                        acc = acc + cw_ref[j:j + 1, lanes] * xp_ref[part, pl.ds(base + j, rb, stride=8), :]
                    y = _silu(acc)
                    if part == 0:
                        y = y * lax.rsqrt(_split_dot(y * y, seg) + EPS) * (GDN_DK ** -0.5)
                    elif part == 1:
                        y = y * lax.rsqrt(_split_dot(y * y, seg) + EPS)
                    qkv_ref[part, pl.ds(off + r + 8 * i0, rb, stride=8), :] = y

    ba = ba_ref[...]
    bg_ref[:, 0:LANES] = _sigmoid(ba)
    sp_in = ba + hp_ref[1:2, :]
    softplus = jnp.maximum(sp_in, 0.0) + jnp.log1p(jnp.exp(-jnp.abs(sp_in)))
    bg_ref[:, LANES:2 * LANES] = hp_ref[0:1, :] * softplus

    rowi = lax.broadcasted_iota(jnp.int32, (CHUNK, LANES), 0)
    colj = lax.broadcasted_iota(jnp.int32, (CHUNK, LANES), 1) % HALF
    lo = lax.broadcasted_iota(jnp.int32, (CHUNK, LANES), 1) < HALF
    ti = lax.broadcasted_iota(jnp.int32, (CHUNK, CHUNK), 0)
    tj = lax.broadcasted_iota(jnp.int32, (CHUNK, CHUNK), 1)
    br = lax.broadcasted_iota(jnp.int32, (LANES, LANES), 0) // HALF
    bc = lax.broadcasted_iota(jnp.int32, (LANES, LANES), 1) // HALF
    blockmask = br == bc

    sub8 = rowi % 8
    nv = CHUNK // 8

    def chunk_cumsum(x, backward):
        for s in (1, 2, 4):
            if backward:
                x = x + jnp.where(sub8 < 8 - s, pltpu.roll(x, CHUNK - s, 0), 0.0)
            else:
                x = x + jnp.where(sub8 >= s, pltpu.roll(x, s, 0), 0.0)
        parts = [x[8 * j:8 * j + 8, :] for j in range(nv)]
        order = range(nv - 1, -1, -1) if backward else range(nv)
        edge = 0 if backward else 7
        run = None
        for j in order:
            if run is not None:
                parts[j] = parts[j] + run
            run = parts[j][edge:edge + 1, :]
        return jnp.concatenate(parts, axis=0)

    incl_dir = (colj <= rowi, colj >= rowi)
    strict_dir = (colj < rowi, colj > rowi)

    def each(f, *lists):
        return [f(*a) for a in zip(*lists)]

    def prepare(chains):
        ds = [d for _, d in chains]
        rows = [pl.ds(pl.multiple_of(c * CHUNK, CHUNK), CHUNK) for c, _ in chains]
        srows = [pl.ds(pl.multiple_of(c * LANES, LANES), LANES) for c, _ in chains]
        qn = [qkv_ref[0, r, :] for r in rows]
        kn = [qkv_ref[1, r, :] for r in rows]
        v = [qkv_ref[2, r, :] for r in rows]
        bx = [jnp.where(lo, bg_ref[r, 2 * d:2 * d + 1], bg_ref[r, 2 * d + 1:2 * d + 2]) for r, d in zip(rows, ds)]
        gcum = [chunk_cumsum(bg_ref[r, LANES:2 * LANES], d == 1) for r, d in zip(rows, ds)]
        gx = [jnp.where(lo, g[:, 4 + 2 * d:5 + 2 * d], g[:, 5 + 2 * d:6 + 2 * d]) for g, d in zip(gcum, ds)]
        eg = each(jnp.exp, gx)
        kb = each(jnp.multiply, kn, bx)
        vb = each(jnp.multiply, v, bx)
        kbg = each(jnp.multiply, kb, eg)
        grow = [jnp.sum(jnp.where(rowi == colj, g, 0.0), axis=0, keepdims=True) for g in gx]
        decay = [jnp.where(incl_dir[d], jnp.exp(jnp.where(incl_dir[d], g - gr, 0.0)), 0.0)
                 for g, gr, d in zip(gx, grow, ds)]
        kst = [_blk(_bf(k), lo) for k in kn]
        kq = [_dot_nt(jnp.concatenate([_bf(a), _bf(b)], axis=0), c) for a, b, c in zip(kb, qn, kst)]
        low = [jnp.where(strict_dir[d], x[0:CHUNK] * dc, 0.0) for x, dc, d in zip(kq, decay, ds)]
        intra = [jnp.where(incl_dir[d], x[CHUNK:2 * CHUNK] * dc, 0.0) for x, dc, d in zip(kq, decay, ds)]
        for r, x in zip(rows, zip(ds, qn, eg, intra)):
            d, q_, e_, i_ = x
            qg_ref[d, r, :] = _bf(q_ * e_)
            in_ref[d, r, :] = _bf(i_)
        lb = each(_bf, low)
        p = [_dot(a, _blk(a, lo)) for a in lb]
        nn = [-a for a in low]
        n_sq = int(math.log2(CHUNK)) - 1
        for r_ in range(n_sq):
            pb = each(_bf, p)
            pblk = [_blk(a, lo) for a in pb]
            if r_ < n_sq - 1:
                both = [_dot(jnp.concatenate([_bf(n_), b], axis=0), pk) for n_, b, pk in zip(nn, pb, pblk)]
                nn = [n_ + p_ + x[0:CHUNK] for n_, p_, x in zip(nn, p, both)]
                p = [x[CHUNK:2 * CHUNK] for x in both]
            else:
                nn = [n_ + p_ + _dot(_bf(n_), pk) for n_, p_, pk in zip(nn, p, pblk)]
        nb = each(_bf, nn)
        uw = [_dot(n_, jnp.concatenate([_blk(_bf(a), lo), _blk(_bf(b), lo)], axis=1))
              for n_, a, b in zip(nb, vb, kbg)]
        u = [a + x[:, 0:LANES] for a, x in zip(vb, uw)]
        w = [a + x[:, LANES:2 * LANES] for a, x in zip(kbg, uw)]
        g_last = [g[0:1, :] if d == 1 else g[CHUNK - 1:CHUNK, :] for g, d in zip(gx, ds)]
        kdec = [_bf(k * jnp.exp(gl - g)) for k, gl, g in zip(kn, g_last, gx)]
        wb = each(_bf, w)
        kr = [_dot_tn(a, jnp.concatenate([b, _bf(c)], axis=1)) for a, b, c in zip(kdec, wb, u)]
        for i, (c, d) in enumerate(chains):
            w_ref[d, rows[i], :] = wb[i]
            u_ref[d, rows[i], :] = u[i]
            k2_ref[d, srows[i], :] = _bf(jnp.where(blockmask, kr[i][:, 0:LANES], 0.0))
            r_ref[d, srows[i], :] = jnp.where(blockmask, kr[i][:, LANES:2 * LANES], 0.0)
            el_ref[d, pl.ds(pl.multiple_of(c * 8, 8), 8), :] = jnp.broadcast_to(jnp.exp(g_last[i]), (8, LANES))

    ncc = tc // CHUNK
    nct = tt // CHUNK
    ng = max(g for g in range(1, UNROLL_GDN + 1) if nct % g == 0)

    def group_chains(i):
        return [(i * ng + k, d) for k in range(ng) for d in range(2)]

    def body_a(i, carry):
        prepare(group_chains(i))
        return carry

    lax.fori_loop(0, nct // ng, body_a, 0)

    def advance(c, d, s):
        srows = pl.ds(pl.multiple_of(c * LANES, LANES), LANES)
        s16 = _bf(s)
        s_ref[d, srows, :] = s16
        e = el_ref[d, pl.ds(pl.multiple_of(c * 8, 8), 1), :]
        return s * e - _dot(k2_ref[d, srows, :], s16) + r_ref[d, srows, :]

    def body_b(i, carry):
        sf, sb = carry
        cb = jnp.where(i < ncc, ncc - 1 - i, nct - 1 - (i - ncc))
        return advance(i, 0, sf), advance(cb, 1, sb)

    zero = jnp.zeros((LANES, LANES), F32)
    lax.fori_loop(0, nct, body_b, (zero, zero))

    def emit(chains):
        rows = [pl.ds(pl.multiple_of(c * CHUNK, CHUNK), CHUNK) for c, _ in chains]
        s16 = [s_ref[d, pl.ds(pl.multiple_of(c * LANES, LANES), LANES), :] for c, d in chains]
        ws = [_dot(w_ref[d, r, :], s) for (_, d), r, s in zip(chains, rows, s16)]
        qs = [_dot(qg_ref[d, r, :], s) for (_, d), r, s in zip(chains, rows, s16)]
        v_new = [u_ref[d, r, :] - a for (_, d), r, a in zip(chains, rows, ws)]
        iv = [_dot(in_ref[d, r, :], _blk(_bf(vn), lo)) for (_, d), r, vn in zip(chains, rows, v_new)]
        for (_, d), r, a, b in zip(chains, rows, qs, iv):
            (ob_ref if d == 1 else of_ref)[r, :] = a + b

    def body_c(i, carry):
        emit(group_chains(i))
        return carry

    lax.fori_loop(0, nct // ng, body_c, 0)

    o = of_ref[...] + ob_ref[...]
    ss = _split_dot(o * o, seg)
    o = o * lax.rsqrt(ss * (1.0 / GDN_DK) + EPS) * nw_ref[...]
    o_ref[...] = _bf(o * _silu(z_ref[...]))


def _gdn(gqkv, gz, gba, conv_p, hp, nw, *, B, TT, tc):
    NT = gqkv.shape[0]
    t = TT - tc
    w3 = 3 * LANES
    nch = TT // CHUNK
    kern = functools.partial(_gdn_kernel, tc=tc, t=t)
    return pl.pallas_call(
        kern,
        grid=(B, GDN_PAIRS),
        in_specs=[pl.BlockSpec((TT, w3), lambda b, p: (b, p)),
                  pl.BlockSpec((TT, LANES), lambda b, p: (b, p)),
                  pl.BlockSpec((TT, LANES), lambda b, p: (b, p)),
                  pl.BlockSpec((None, 8, w3), lambda b, p: (p, 0, 0)),
                  pl.BlockSpec((None, 8, LANES), lambda b, p: (p, 0, 0)),
                  pl.BlockSpec((1, LANES), lambda b, p: (0, 0))],
        out_specs=pl.BlockSpec((TT, LANES), lambda b, p: (b, p)),
        out_shape=jax.ShapeDtypeStruct((NT, GDN_W), BF16),
        scratch_shapes=[pltpu.VMEM((3, t + 16, LANES), F32), pltpu.VMEM((3, TT, LANES), F32),
                        pltpu.VMEM((TT, 2 * LANES), F32), pltpu.VMEM((TT, LANES), F32),
                        pltpu.VMEM((TT, LANES), F32),
                        pltpu.VMEM((2, TT, LANES), BF16), pltpu.VMEM((2, TT, LANES), BF16),
                        pltpu.VMEM((2, TT, LANES), BF16), pltpu.VMEM((2, TT, LANES), F32),
                        pltpu.VMEM((2, nch * LANES, LANES), BF16), pltpu.VMEM((2, nch * LANES, LANES), F32),
                        pltpu.VMEM((2, nch * 8, LANES), F32), pltpu.VMEM((2, nch * LANES, LANES), BF16)],
        compiler_params=_cparams(("parallel", "parallel")),
        name="gdn",
    )(gqkv, gz, gba, conv_p, hp, nw)


def _att_tile(q_ref, k_ref, v_ref, o_ref, nk):
    lo = lax.broadcasted_iota(jnp.int32, (1, LANES), 1) < HALF
    k = k_ref[0:nk, :]
    v = v_ref[0:nk, :]

    def scores(h):
        j, half = h // 2, h % 2
        qj = q_ref[:, j * LANES:(j + 1) * LANES]
        zq = jnp.zeros_like(qj)
        return _dot_nt(jnp.where(lo, qj, zq) if half == 0 else jnp.where(lo, zq, qj), k)

    outs = []
    s = scores(0)
    for h in range(ATT_HEADS):
        s_next = scores(h + 1) if h + 1 < ATT_HEADS else None
        m = jnp.max(s, axis=-1, keepdims=True)
        p = jnp.exp(s - m)
        l = jnp.sum(p, axis=-1, keepdims=True)
        outs.append(_dot(_bf(p), v) / l)
        s = s_next
    for j in range(ATT_GROUP):
        o_ref[:, j * LANES:(j + 1) * LANES] = _bf(jnp.where(lo, outs[2 * j], outs[2 * j + 1]))


def _att_kernel(q_ref, k_ref, v_ref, o_ref, *, n_ctx_tiles, tc, tt):
    is_ctx = pl.program_id(1) < n_ctx_tiles

    @pl.when(is_ctx)
    def _():
        _att_tile(q_ref, k_ref, v_ref, o_ref, tc)

    @pl.when(jnp.logical_not(is_ctx))
    def _():
        _att_tile(q_ref, k_ref, v_ref, o_ref, tt)


def _attention(aq, ak, av, *, B, TT, tc):
    NT = aq.shape[0]
    tq = 256
    nq = TT // tq
    kv_spec = pl.BlockSpec((TT, LANES), lambda b, i: (b, 0))
    q_spec = pl.BlockSpec((tq, ATT_W), lambda b, i: (b * nq + i, 0))
    return pl.pallas_call(
        functools.partial(_att_kernel, n_ctx_tiles=tc // tq, tc=tc, tt=TT),
        grid=(B, nq),
        in_specs=[q_spec, kv_spec, kv_spec],
        out_specs=q_spec,
        out_shape=jax.ShapeDtypeStruct((NT, ATT_W), BF16),
        compiler_params=_cparams(("parallel", "parallel")),
        name="attention",
    )(aq, ak, av)


S5_BB = 2
GPV = LANES // S5_GH


def _s5_kernel(u0_ref, u1_ref, tg_ref, bp_ref, cp_ref, la_ref, lb_ref, d_ref, gw_ref, gb_ref, o_ref,
               uf_ref, sloc_ref, sin_ref, y0_ref, y1_ref, *, bb, ncc, nct):
    lc = S5_CHUNK
    nr = bb * nct
    w2 = 2 * S5_P
    blk8 = lax.broadcasted_iota(jnp.int32, (1, LANES), 1) // S5_GH
    u_refs = (u0_ref, u1_ref)
    y_refs = (y0_ref, y1_ref)

    def block_transpose(xs):
        xs = list(xs)
        for s in (4, 2, 1):
            keep = (blk8 & s) == 0
            for a in range(GPV):
                if a & s == 0:
                    lo_, hi_ = xs[a], xs[a + s]
                    xs[a] = jnp.where(keep, lo_, pltpu.roll(hi_, s * S5_GH, 1))
                    xs[a + s] = jnp.where(keep, pltpu.roll(lo_, LANES - s * S5_GH, 1), hi_)
        return xs

    rbk = 32
    for a in range(S5_GROUPS // GPV):
        for v in range(lc // GPV):
            for r0 in range(0, nr, rbk):
                xs = [u_refs[a][pl.ds((v * GPV + k) + lc * r0, rbk, stride=lc), :] for k in range(GPV)]
                for k, y_ in enumerate(block_transpose(xs)):
                    uf_ref[a * GPV + k, r0:r0 + rbk, v * LANES:(v + 1) * LANES] = y_
    for g in range(S5_GROUPS):
        sl = _dot(_bf(uf_ref[g]), bp_ref[g])
        sloc_ref[2 * g] = sl[:, 0:w2]
        sloc_ref[2 * g + 1] = sl[:, w2:2 * w2]

    def body(i, carry):
        cb = jnp.where(i < ncc, ncc - 1 - i, nct - 1 - (i - ncc))
        out = []
        for g in range(S5_GROUPS):
            for d in range(2):
                s = carry[2 * g + d]
                rows = pl.ds(i if d == 0 else cb, bb, stride=nct)
                lanes = slice(d * w2, (d + 1) * w2)
                sin_ref[2 * g + d, rows, :] = s
                out.append(s * la_ref[g, :, lanes] + pltpu.roll(s, S5_P, 1) * lb_ref[g, :, lanes]
                           + sloc_ref[2 * g + d, rows, :])
        return tuple(out)

    zero = jnp.zeros((bb, w2), F32)
    lax.fori_loop(0, nct, body, (zero,) * (2 * S5_GROUPS))

    for g in range(S5_GROUPS):
        uf_ref[g] = (_dot(_bf(uf_ref[g]), tg_ref[g])
                     + _dot(_bf(sin_ref[2 * g]), cp_ref[g, 0:w2, :])
                     + _dot(_bf(sin_ref[2 * g + 1]), cp_ref[g, w2:2 * w2, :]))

    for a in range(S5_GROUPS // GPV):
        for v in range(lc // GPV):
            for r0 in range(0, nr, rbk):
                xs = [uf_ref[a * GPV + k, r0:r0 + rbk, v * LANES:(v + 1) * LANES] for k in range(GPV)]
                for k, y_ in enumerate(block_transpose(xs)):
                    y_refs[a][pl.ds((v * GPV + k) + lc * r0, rbk, stride=lc), :] = y_

    y = jnp.concatenate([y0_ref[...], y1_ref[...]], axis=1)
    u = jnp.concatenate([u0_ref[...], u1_ref[...]], axis=1)
    zz = _gelu_tanh(y + d_ref[...] * u)
    o_ref[...] = _bf(zz * _sigmoid(_dot(_bf(zz), gw_ref[...]) + gb_ref[...]))


def _s5(su, tg, bp, cp, la, lb, d, gw, gb, *, B, TT, tc):
    NT, W = su.shape
    bb = S5_BB
    nct = TT // S5_CHUNK
    nr = bb * nct
    kern = functools.partial(_s5_kernel, bb=bb, ncc=tc // S5_CHUNK, nct=nct)
    full = lambda a: pl.BlockSpec(a.shape, lambda i: (0,) * a.ndim)
    return pl.pallas_call(
        kern,
        grid=(B // bb,),
        in_specs=[pl.BlockSpec((bb * TT, LANES), lambda i: (i, 0)),
                  pl.BlockSpec((bb * TT, LANES), lambda i: (i, 1)),
                  full(tg), full(bp), full(cp), full(la), full(lb), full(d), full(gw), full(gb)],
        out_specs=pl.BlockSpec((bb * TT, W), lambda i: (i, 0)),
        out_shape=jax.ShapeDtypeStruct((NT, W), BF16),
        scratch_shapes=[pltpu.VMEM((S5_GROUPS, nr, W), F32), pltpu.VMEM((2 * S5_GROUPS, nr, LANES), F32),
                        pltpu.VMEM((2 * S5_GROUPS, nr, LANES), F32),
                        pltpu.VMEM((bb * TT, LANES), F32), pltpu.VMEM((bb * TT, LANES), F32)],
        compiler_params=_cparams(("parallel",)),
        name="s5",
    )(su, su, tg, bp, cp, la, lb, d, gw, gb)


def _s5_matrices(lam_re, lam_im, log_dt, b_re, b_im, c_re, c_im):
    Lc = S5_CHUNK
    lam = lax.complex(lam_re.astype(F32), lam_im.astype(F32))
    dt = jnp.exp(log_dt.astype(F32))[..., None]
    lam_dt = lam * dt
    lam_bar = jnp.exp(lam_dt)
    b_bar = ((lam_bar - 1.0) / lam)[..., None] * lax.complex(b_re.astype(F32), b_im.astype(F32))
    cc = lax.complex(c_re.astype(F32), c_im.astype(F32))
    ks = jnp.arange(Lc + 1, dtype=F32)
    pw = jnp.exp(lam_dt[..., None, :] * ks[:, None])
    kern = jnp.real(jnp.einsum('dghp,dgkp,dgpi->dgkhi', cc, pw[:, :, :Lc], b_bar))
    i_idx = jnp.arange(Lc)[:, None]
    j_idx = jnp.arange(Lc)[None, :]
    lag = i_idx - j_idx
    kf = kern[0][:, jnp.clip(lag, 0, Lc - 1)]
    kb = kern[1][:, jnp.clip(-lag, 0, Lc - 1)]
    tmat = (jnp.where((lag >= 0)[None, :, :, None, None], kf, 0.0)
            + jnp.where((lag <= 0)[None, :, :, None, None], kb, 0.0))
    tg = tmat.transpose(0, 2, 4, 1, 3).reshape(S5_GROUPS, Lc * S5_GH, Lc * S5_GH)
    pf = pw[0][:, Lc - 1 - jnp.arange(Lc)]
    pb = pw[1][:, jnp.arange(Lc)]
    sf = jnp.einsum('gjp,gph->gjhp', pf, b_bar[0])
    sb = jnp.einsum('gjp,gph->gjhp', pb, b_bar[1])
    bp = jnp.concatenate([jnp.real(sf), jnp.imag(sf), jnp.real(sb), jnp.imag(sb)], axis=-1)
    bp = bp.reshape(S5_GROUPS, Lc * S5_GH, 4 * S5_P)
    qf = jnp.einsum('ghp,gip->gpih', cc[0], pw[0][:, 1 + jnp.arange(Lc)])
    qb = jnp.einsum('ghp,gip->gpih', cc[1], pw[1][:, Lc - jnp.arange(Lc)])
    cp = jnp.concatenate([jnp.real(qf), -jnp.imag(qf), jnp.real(qb), -jnp.imag(qb)], axis=1)
    cp = cp.reshape(S5_GROUPS, 4 * S5_P, Lc * S5_GH)
    lc = pw[:, :, Lc]
    la = jnp.concatenate([jnp.real(lc[0]), jnp.real(lc[0]), jnp.real(lc[1]), jnp.real(lc[1])], axis=-1)
    lb = jnp.concatenate([-jnp.imag(lc[0]), jnp.imag(lc[0]), -jnp.imag(lc[1]), jnp.imag(lc[1])], axis=-1)
    return (tg.astype(BF16), bp.astype(BF16), cp.astype(BF16),
            la.reshape(S5_GROUPS, 1, 4 * S5_P), lb.reshape(S5_GROUPS, 1, 4 * S5_P))


def _gelu_tanh(x):
    return 0.5 * x * (1.0 + jnp.tanh(math.sqrt(2.0 / math.pi) * (x + 0.044715 * (x * x * x))))


def _kout_kernel(x_ref, mod_c_ref, mod_b_ref, go_ref, ao_ref, so_ref,
                 wg_ref, wa_ref, ws_ref, g_ref, b_ref, o_ref, *, tm, tc, tpb):
    rm = functools.partial(_row_mod, mod_c_ref, mod_b_ref, _ctx_rows(tm, tc, tpb))
    y = _dot(go_ref[...], wg_ref[...]) + _dot(ao_ref[...], wa_ref[...]) + _dot(so_ref[...], ws_ref[...])
    z = ALPHA * x_ref[...] + rm(5) * y
    o_ref[...] = _layer_norm(z, g_ref[...], b_ref[...])


def _kout(xa, mod, go, ao, so, wg, wa, ws, g, b, *, B, TT, tc):
    NT, D = xa.shape
    tpb = 4
    tm = TT // tpb
    kern = functools.partial(_kout_kernel, tm=tm, tc=tc, tpb=tpb)
    row = lambda w: pl.BlockSpec((tm, w), lambda i: (i, 0))
    full = lambda a: pl.BlockSpec(a.shape, lambda i: (0,) * a.ndim)
    return pl.pallas_call(
        kern,
        grid=(NT // tm,),
        in_specs=[row(D),
                  pl.BlockSpec((None, N_MOD, D), lambda i: (B, 0, 0)),
                  pl.BlockSpec((None, N_MOD, D), lambda i: (i // tpb, 0, 0)),
                  row(GDN_W), row(ATT_W), row(S5_W),
                  full(wg), full(wa), full(ws), full(g), full(b)],
        out_specs=row(D),
        out_shape=jax.ShapeDtypeStruct((NT, D), F32),
        compiler_params=_cparams(("parallel",)),
        name="mixer_out",
    )(xa, mod, mod, go, ao, so, wg, wa, ws, g, b)


def _perm_in_cols():
    off_z = GDN_QKV
    off_b = off_z + GDN_W
    off_a = off_b + 2 * GDN_HEADS
    off_q = off_a + 2 * GDN_HEADS
    off_k = off_q + ATT_W
    off_v = off_k + ATT_KV_HEADS * ATT_DH
    off_u = off_v + ATT_KV_HEADS * ATT_DH
    cols = []
    for p in range(GDN_PAIRS):
        for part in range(3):
            base = part * GDN_W + 2 * p * GDN_DK
            cols += list(range(base, base + 2 * GDN_DK))
    cols += list(range(off_z, off_z + GDN_W))
    for p in range(GDN_PAIRS):
        blk = []
        for off in (off_b, off_a):
            for d in range(2):
                blk += [off + d * GDN_HEADS + 2 * p, off + d * GDN_HEADS + 2 * p + 1]
        cols += blk + [-1] * (LANES - len(blk))
    for j in range(ATT_GROUP):
        for h in (j, j + ATT_GROUP):
            cols += list(range(off_q + h * ATT_DH, off_q + (h + 1) * ATT_DH))
    cols += list(range(off_k, off_k + 2 * ATT_DH))
    cols += list(range(off_v, off_v + 2 * ATT_DH))
    cols += list(range(off_u, off_u + S5_W))
    return np.asarray(cols, dtype=np.int32)


def _gdn_pair_cols():
    cols = []
    for p in range(GDN_PAIRS):
        for part in range(3):
            base = part * GDN_W + 2 * p * GDN_DK
            cols += list(range(base, base + 2 * GDN_DK))
    return np.asarray(cols, dtype=np.int32)


def _att_out_rows():
    rows = []
    for j in range(ATT_GROUP):
        for h in (j, j + ATT_GROUP):
            rows += list(range(h * ATT_DH, (h + 1) * ATT_DH))
    return np.asarray(rows, dtype=np.int32)


def _take(w, idx, axis):
    idx = [int(i) for i in idx]
    parts = []
    i = 0
    while i < len(idx):
        j = i + 1
        if idx[i] < 0:
            while j < len(idx) and idx[j] < 0:
                j += 1
            shape = list(w.shape)
            shape[axis] = j - i
            parts.append(jnp.zeros(shape, w.dtype))
        else:
            while j < len(idx) and idx[j] == idx[j - 1] + 1:
                j += 1
            parts.append(lax.slice_in_dim(w, idx[i], idx[j - 1] + 1, axis=axis))
        i = j
    return jnp.concatenate(parts, axis=axis)


def _rope_tables(tc, t):
    rows = t // GRID_W
    row = jnp.repeat(jnp.arange(rows), GRID_W)
    col = jnp.tile(jnp.arange(GRID_W), rows)
    inv_freq = ROPE_THETA ** (-jnp.arange(ROPE_PAIRS, dtype=F32) / ROPE_PAIRS)
    ang = jnp.stack([row, col], axis=-1).astype(F32)[..., None] * inv_freq
    cos, sin = jnp.cos(ang), jnp.sin(ang)
    zero = jnp.zeros_like(sin)
    cos_h = jnp.stack([cos, cos], axis=2).reshape(t, ATT_DH)
    sa_h = jnp.stack([-sin, zero], axis=2).reshape(t, ATT_DH)
    sb_h = jnp.stack([zero, sin], axis=2).reshape(t, ATT_DH)
    def full(tab, fill):
        tab = jnp.concatenate([jnp.full((tc, ATT_DH), fill, F32), tab], axis=0)
        return jnp.concatenate([tab, tab], axis=1)
    return full(cos_h, 1.0), full(sa_h, 0.0), full(sb_h, 0.0)


def kernel(x, c, ctx, c_ctx, w_ada, b_ada, ln_g, ln_b, ffn_w1, ffn_w3, ffn_w2, w_in, w_out, gdn_conv_w,
           gdn_a_log, gdn_dt_bias, gdn_norm_w, q_norm_w, k_norm_w, s5_lam_re, s5_lam_im, s5_log_dt,
           s5_b_re, s5_b_im, s5_c_re, s5_c_im, s5_d, glu_w, glu_b):
    B, T, D = x.shape
    tc = ctx.shape[1]
    TT = tc + T
    NT = B * TT
    L = w_ada.shape[0]
    kw = dict(B=B, TT=TT, tc=tc)

    R = ((B + 1 + 7) // 8) * 8
    c_all = jnp.concatenate([c, c_ctx[None, :], jnp.zeros((R - B - 1, D), F32)], axis=0)
    mod_all = _modulation(c_all, w_ada, b_ada).reshape(L, R, N_MOD, D)

    in_cols = _perm_in_cols()
    pair_cols = _gdn_pair_cols()
    att_rows = _att_out_rows()
    cos, sa, sb = _rope_tables(tc, T)

    def head_params(v):
        v = v.astype(F32).reshape(2, GDN_PAIRS, 2).transpose(1, 0, 2).reshape(GDN_PAIRS, 4)
        return jnp.pad(v, ((0, 0), (4, LANES - 8)))

    xa = jnp.concatenate([ctx, x], axis=1).reshape(NT, D)
    for l in range(L):
        mod = mod_all[l]
        w2 = _bf(ffn_w2[l])
        g_l, b_l = ln_g[l], ln_b[l]
        xa = _ffn(xa, mod, _ffn_pack_w13(ffn_w1[l, 0], ffn_w3[l, 0]), w2[0], g_l[0:1], b_l[0:1], sub=0, **kw)

        w_in_p = _take(_bf(w_in[l]), in_cols, 1)
        qw = jnp.tile(q_norm_w[l], 2)[None, :]
        kwt = jnp.tile(k_norm_w[l], 2)[None, :]
        gqkv, gz, gba, aq, ak, av, su = _kin(xa, mod, w_in_p, qw, kwt, cos, sa, sb, **kw)

        conv_p = jnp.pad(_take(gdn_conv_w[l], pair_cols, 1), ((0, 8 - CONV_K), (0, 0)))
        conv_p = conv_p.reshape(8, GDN_PAIRS, 3 * LANES).transpose(1, 0, 2)
        hp = jnp.stack([head_params(-jnp.exp(gdn_a_log[l].astype(F32))), head_params(gdn_dt_bias[l])], axis=1)
        hp = jnp.pad(hp, ((0, 0), (0, 6), (0, 0)))
        nw = jnp.tile(gdn_norm_w[l], 2)[None, :]
        go = _gdn(gqkv, gz, gba, conv_p, hp, nw, **kw)

        ao = _attention(aq, ak, av, **kw)

        tg, bp, cp, la, lb = _s5_matrices(s5_lam_re[l], s5_lam_im[l], s5_log_dt[l], s5_b_re[l], s5_b_im[l],
                                          s5_c_re[l], s5_c_im[l])
        so = _s5(su, tg, bp, cp, la, lb, s5_d[l][None, :], _bf(glu_w[l]), glu_b[l][None, :], **kw)

        wo = _bf(w_out[l])
        wg = wo[0:GDN_W]
        wa = _take(wo[GDN_W:GDN_W + ATT_W], att_rows, 0)
        ws = wo[GDN_W + ATT_W:]
        xa = _kout(xa, mod, go, ao, so, wg, wa, ws, g_l[1:2], b_l[1:2], **kw)
        xa = _ffn(xa, mod, _ffn_pack_w13(ffn_w1[l, 1], ffn_w3[l, 1]), w2[1], g_l[2:3], b_l[2:3], sub=2, **kw)
    return xa.reshape(B, TT, D)[:, tc:, :]
```

```python
import functools
import math

import numpy as np
import jax
import jax.numpy as jnp
from jax import lax
from jax.experimental import pallas as pl
from jax.experimental.pallas import tpu as pltpu

F32 = jnp.float32
BF16 = jnp.bfloat16

D_MODEL = 1024
GRID_W = 64
GDN_HEADS = 6
GDN_DK = 64
GDN_PAIRS = GDN_HEADS // 2
GDN_QKV = 3 * GDN_HEADS * GDN_DK
GDN_W = GDN_HEADS * GDN_DK
CONV_K = 5
CHUNK = 64
UNROLL_GDN = 9
ATT_HEADS = 6
ATT_KV_HEADS = 2
ATT_GROUP = ATT_HEADS // ATT_KV_HEADS
ATT_DH = 64
ATT_W = ATT_HEADS * ATT_DH
ROPE_THETA = 10000.0
ROPE_PAIRS = ATT_DH // 4
S5_GROUPS = 16
S5_GH = 16
S5_P = 64
S5_W = S5_GROUPS * S5_GH
S5_CHUNK = 16
N_MOD = 9
DEPTH = 4
ALPHA = (2.0 * DEPTH) ** 0.25
EPS = 1e-6
LANES = 128
HALF = LANES // 2

C_QKV = 0
C_Z = C_QKV + GDN_QKV
C_BA = C_Z + GDN_W
C_AQ = C_BA + GDN_PAIRS * LANES
C_AK = C_AQ + ATT_W
C_AV = C_AK + LANES
C_SU = C_AV + LANES
C_END = C_SU + S5_W

VMEM_LIMIT = 56 * 1024 * 1024


def _cparams(sem):
    return pltpu.CompilerParams(dimension_semantics=sem, vmem_limit_bytes=VMEM_LIMIT)


def _bf(x):
    return x.astype(BF16)


def _dot(a, b):
    return jnp.dot(a, b, preferred_element_type=F32)


def _dot_nt(a, b):
    return lax.dot_general(a, b, (((1,), (1,)), ((), ())), preferred_element_type=F32)


def _dot_tn(a, b):
    return lax.dot_general(a, b, (((0,), (0,)), ((), ())), preferred_element_type=F32)


def _split_dot(x, m):
    hi = _bf(x)
    lo = _bf(x - hi.astype(F32))
    return _dot(hi, m) + _dot(lo, m)


def _split3_dot_left(m, x):
    hi = _bf(x)
    r1 = x - hi.astype(F32)
    mid = _bf(r1)
    lo = _bf(r1 - mid.astype(F32))
    return _dot(m, hi) + _dot(m, mid) + _dot(m, lo)


def _sigmoid(x):
    return 0.5 + 0.5 * jnp.tanh(0.5 * x)


def _silu(x):
    h = 0.5 * x
    return h + h * jnp.tanh(h)


def _layer_norm(z, g, b):
    mu = jnp.mean(z, axis=-1, keepdims=True)
    zc = z - mu
    var = jnp.mean(zc * zc, axis=-1, keepdims=True)
    return zc * lax.rsqrt(var + EPS) * g + b


def _ctx_rows(tm, tc, tiles_per_batch):
    row0 = (pl.program_id(0) % tiles_per_batch) * tm
    return row0 + lax.broadcasted_iota(jnp.int32, (tm, 1), 0) < tc


def _row_mod(mod_c_ref, mod_b_ref, is_ctx, idx):
    return jnp.where(is_ctx, mod_c_ref[idx:idx + 1, :], mod_b_ref[idx:idx + 1, :])


def _mod_kernel(c_ref, w_ref, b_ref, o_ref):
    sc = _silu(c_ref[...])
    o_ref[...] = jnp.dot(sc, w_ref[...], preferred_element_type=F32,
                         precision=lax.Precision.HIGHEST) + b_ref[...]


def _modulation(c_all, w_ada, b_ada):
    L, D, NM = w_ada.shape
    R = c_all.shape[0]
    tn = 1152
    return pl.pallas_call(
        _mod_kernel,
        grid=(L, NM // tn),
        in_specs=[pl.BlockSpec((R, D), lambda l, j: (0, 0)),
                  pl.BlockSpec((None, D, tn), lambda l, j: (l, 0, j)),
                  pl.BlockSpec((None, 1, tn), lambda l, j: (l, 0, j))],
        out_specs=pl.BlockSpec((None, R, tn), lambda l, j: (l, 0, j)),
        out_shape=jax.ShapeDtypeStruct((L, R, NM), F32),
        compiler_params=_cparams(("parallel", "parallel")),
        name="adaln_mod",
    )(c_all, w_ada, b_ada.reshape(L, 1, NM))


def _ffn_kernel(x_ref, mod_c_ref, mod_b_ref, modp_b_ref, w13_ref, w2_ref, g_ref, b_ref, o_ref,
                hm_ref, acc_ref, xk_ref, *, sub, tm, tc, tpb, nf, nt):
    i = pl.program_id(0)
    f = pl.program_id(1)
    slot = i % 2
    row = lax.broadcasted_iota(jnp.int32, (tm, 1), 0)
    rm = functools.partial(_row_mod, mod_c_ref, mod_b_ref, (i % tpb) * tm + row < tc)
    nsteps = -(-nf // FFN_GROUP)
    qr = tm // nsteps

    def modulate():
        x = x_ref[...]
        xk_ref[slot] = x
        hm_ref[...] = _bf(x * (1.0 + rm(3 * sub + 1)) + rm(3 * sub))

    def accumulate(step):
        hm = hm_ref[...]
        for c in range(min(FFN_GROUP, nf - step * FFN_GROUP)):
            h = _dot(hm, w13_ref[c])
            part = _dot(_bf(_silu(h[:, 0:FFN_FC]) * h[:, FFN_FC:2 * FFN_FC]), w2_ref[c])
            if step == 0 and c == 0:
                acc_ref[slot] = part
            else:
                acc_ref[slot] += part

    def finish_previous():
        rg = 32
        for k in range(qr // rg):
            r0 = pl.multiple_of(f * qr + k * rg, 8)
            rows = pl.ds(r0, rg)
            prev_row = ((i + tpb - 1) % tpb) * tm + r0 + lax.broadcasted_iota(jnp.int32, (rg, 1), 0)
            gate = _row_mod(mod_c_ref, modp_b_ref, prev_row < tc, 3 * sub + 2)
            z = ALPHA * xk_ref[1 - slot, rows, :] + 0.5 * gate * acc_ref[1 - slot, rows, :]
            o_ref[rows, :] = _layer_norm(z, g_ref[...], b_ref[...])

    do_mm = i < nt
    do_fin = i > 0
    for step in range(nsteps):
        here = f == step

        @pl.when(here & do_mm & do_fin)
        def _(step=step):
            finish_previous()
            if step == 0:
                modulate()
            accumulate(step)

        @pl.when(here & do_mm & jnp.logical_not(do_fin))
        def _(step=step):
            if step == 0:
                modulate()
            accumulate(step)

    @pl.when(jnp.logical_not(do_mm) & do_fin)
    def _():
        finish_previous()


FFN_FC = 256
FFN_GROUP = 6
FFN_TPB = 3


def _ffn_pack(w1, w3, w2):
    D, F = w1.shape
    nf = F // FFN_FC
    pad = -nf % FFN_GROUP
    w13 = jnp.concatenate([_bf(w1).reshape(D, nf, FFN_FC), _bf(w3).reshape(D, nf, FFN_FC)], axis=2)
    w13 = jnp.pad(w13.transpose(1, 0, 2), ((0, pad), (0, 0), (0, 0)))
    w2p = jnp.pad(_bf(w2).reshape(nf, FFN_FC, D), ((0, pad), (0, 0), (0, 0)))
    return w13, w2p, nf


def _ffn(xa, mod, packed, g, b, *, sub, B, TT, tc):
    NT, D = xa.shape
    w13, w2, nf = packed
    fc = FFN_FC
    tpb = FFN_TPB
    tm = TT // tpb
    nt = NT // tm
    nsteps = w13.shape[0] // FFN_GROUP
    kern = functools.partial(_ffn_kernel, sub=sub, tm=tm, tc=tc, tpb=tpb, nf=nf, nt=nt)
    cur = lambda i: jnp.minimum(i, nt - 1)
    prev = lambda i: jnp.maximum(i - 1, 0)
    return pl.pallas_call(
        kern,
        grid=(nt + 1, nsteps),
        in_specs=[pl.BlockSpec((tm, D), lambda i, f: (cur(i + jnp.minimum(f, 1)), 0)),
                  pl.BlockSpec((None, N_MOD, D), lambda i, f: (B, 0, 0)),
                  pl.BlockSpec((None, N_MOD, D), lambda i, f: (cur(i) // tpb, 0, 0)),
                  pl.BlockSpec((None, N_MOD, D), lambda i, f: (prev(i) // tpb, 0, 0)),
                  pl.BlockSpec((FFN_GROUP, D, 2 * fc), lambda i, f: (f, 0, 0)),
                  pl.BlockSpec((FFN_GROUP, fc, D), lambda i, f: (f, 0, 0)),
                  pl.BlockSpec((1, D), lambda i, f: (0, 0)),
                  pl.BlockSpec((1, D), lambda i, f: (0, 0))],
        out_specs=pl.BlockSpec((tm, D), lambda i, f: (prev(i), 0)),
        out_shape=jax.ShapeDtypeStruct((NT, D), F32),
        scratch_shapes=[pltpu.VMEM((tm, D), BF16), pltpu.VMEM((2, tm, D), F32), pltpu.VMEM((2, tm, D), F32)],
        compiler_params=_cparams(("arbitrary", "arbitrary")),
        name=f"ffn{sub}",
    )(xa, mod, mod, mod, w13, w2, g, b)


def _seg_ones():
    r = lax.broadcasted_iota(jnp.int32, (LANES, LANES), 0) // HALF
    c = lax.broadcasted_iota(jnp.int32, (LANES, LANES), 1) // HALF
    return jnp.where(r == c, 1.0, 0.0).astype(BF16)


def _kin_kernel(x_ref, mod_c_ref, mod_b_ref, w_ref, qw_ref, kw_ref, cos_ref, sa_ref, sb_ref,
                gqkv_ref, gz_ref, gba_ref, aq_ref, ak_ref, av_ref, su_ref, *, tm, tc, tpb):
    rm = functools.partial(_row_mod, mod_c_ref, mod_b_ref, _ctx_rows(tm, tc, tpb))
    hm = _bf(x_ref[...] * (1.0 + rm(4)) + rm(3))
    r = _dot(hm, w_ref[...])
    gqkv_ref[...] = r[:, C_QKV:C_Z]
    gz_ref[...] = r[:, C_Z:C_BA]
    gba_ref[...] = r[:, C_BA:C_AQ]
    av_ref[...] = _bf(r[:, C_AV:C_SU])
    su_ref[...] = r[:, C_SU:C_END]

    seg = _seg_ones()
    cos, sa, sb = cos_ref[...], sa_ref[...], sb_ref[...]

    def norm_rope(blk, w_row):
        ss = _split_dot(blk * blk, seg)
        y = blk * lax.rsqrt(ss * (1.0 / ATT_DH) + EPS) * w_row
        return y * cos + pltpu.roll(y, LANES - ROPE_PAIRS, 1) * sa + pltpu.roll(y, ROPE_PAIRS, 1) * sb

    qw = qw_ref[...] * (ATT_DH ** -0.5)
    for j in range(ATT_W // LANES):
        aq_ref[:, j * LANES:(j + 1) * LANES] = _bf(
            norm_rope(r[:, C_AQ + j * LANES:C_AQ + (j + 1) * LANES], qw))
    ak_ref[...] = _bf(norm_rope(r[:, C_AK:C_AV], kw_ref[...]))


def _kin(xa, mod, w_in_p, qw, kw, cos, sa, sb, *, B, TT, tc):
    NT, D = xa.shape
    tpb = 4
    tm = TT // tpb
    kern = functools.partial(_kin_kernel, tm=tm, tc=tc, tpb=tpb)
    widths = [(GDN_QKV, F32), (GDN_W, F32), (GDN_PAIRS * LANES, F32), (ATT_W, BF16),
              (LANES, BF16), (LANES, BF16), (S5_W, F32)]
    return pl.pallas_call(
        kern,
        grid=(NT // tm,),
        in_specs=[pl.BlockSpec((tm, D), lambda i: (i, 0)),
                  pl.BlockSpec((None, N_MOD, D), lambda i: (B, 0, 0)),
                  pl.BlockSpec((None, N_MOD, D), lambda i: (i // tpb, 0, 0)),
                  pl.BlockSpec((D, C_END), lambda i: (0, 0)),
                  pl.BlockSpec((1, LANES), lambda i: (0, 0)),
                  pl.BlockSpec((1, LANES), lambda i: (0, 0)),
                  pl.BlockSpec((tm, LANES), lambda i: (i % tpb, 0)),
                  pl.BlockSpec((tm, LANES), lambda i: (i % tpb, 0)),
                  pl.BlockSpec((tm, LANES), lambda i: (i % tpb, 0))],
        out_specs=[pl.BlockSpec((tm, w), lambda i: (i, 0)) for w, _ in widths],
        out_shape=[jax.ShapeDtypeStruct((NT, w), dt) for w, dt in widths],
        compiler_params=_cparams(("parallel",)),
        name="mixer_in",
    )(xa, mod, mod, w_in_p, qw, kw, cos, sa, sb)


def _blk(x, lo):
    z = jnp.zeros_like(x)
    return jnp.concatenate([jnp.where(lo, x, z), jnp.where(lo, z, x)], axis=0)


def _gdn_kernel(qkv_in_ref, z_ref, ba_ref, cw_ref, hp_ref, nw_ref, o_ref,
                xp_ref, qkv_ref, bg_ref, of_ref, ob_ref,
                qg_ref, in_ref, w_ref, u_ref, k2_ref, r_ref, el_ref, s_ref, *, tc, t):
    tt = tc + t
    seg = _seg_ones()

    pad = 8
    for off, n in ((0, tc), (tc, t)):
        for part in range(3):
            xp_ref[part, 0:pad, :] = jnp.zeros((pad, LANES), F32)
            xp_ref[part, pad:pad + n, :] = qkv_in_ref[off:off + n, part * LANES:(part + 1) * LANES]
            xp_ref[part, pad + n:2 * pad + n, :] = jnp.zeros((pad, LANES), F32)
        nb = n // 8
        rb = min(nb, 128)
        for r in range(8):
            for i0 in range(0, nb, rb):
                for part in range(3):
                    lanes = slice(part * LANES, (part + 1) * LANES)
                    base = pad - CONV_K // 2 + r + 8 * i0
                    acc = cw_ref[0:1, lanes] * xp_ref[part, pl.ds(base, rb, stride=8), :]
                    for j in range(1, CONV_K):
                        acc = acc + cw_ref[j:j + 1, lanes] * xp_ref[part, pl.ds(base + j, rb, stride=8), :]
                    y = _silu(acc)
                    if part == 0:
                        y = y * lax.rsqrt(_split_dot(y * y, seg) + EPS) * (GDN_DK ** -0.5)
                    elif part == 1:
                        y = y * lax.rsqrt(_split_dot(y * y, seg) + EPS)
                    qkv_ref[part, pl.ds(off + r + 8 * i0, rb, stride=8), :] = y

    ba = ba_ref[...]
    bg_ref[:, 0:LANES] = _sigmoid(ba)
    sp_in = ba + hp_ref[1:2, :]
    softplus = jnp.maximum(sp_in, 0.0) + jnp.log1p(jnp.exp(-jnp.abs(sp_in)))
    bg_ref[:, LANES:2 * LANES] = hp_ref[0:1, :] * softplus

    rowi = lax.broadcasted_iota(jnp.int32, (CHUNK, LANES), 0)
    colj = lax.broadcasted_iota(jnp.int32, (CHUNK, LANES), 1) % HALF
    lo = lax.broadcasted_iota(jnp.int32, (CHUNK, LANES), 1) < HALF
    ti = lax.broadcasted_iota(jnp.int32, (CHUNK, CHUNK), 0)
    tj = lax.broadcasted_iota(jnp.int32, (CHUNK, CHUNK), 1)
    br = lax.broadcasted_iota(jnp.int32, (LANES, LANES), 0) // HALF
    bc = lax.broadcasted_iota(jnp.int32, (LANES, LANES), 1) // HALF
    blockmask = br == bc

    sub8 = rowi % 8
    nv = CHUNK // 8

    def chunk_cumsum(x, backward):
        for s in (1, 2, 4):
            if backward:
                x = x + jnp.where(sub8 < 8 - s, pltpu.roll(x, CHUNK - s, 0), 0.0)
            else:
                x = x + jnp.where(sub8 >= s, pltpu.roll(x, s, 0), 0.0)
        parts = [x[8 * j:8 * j + 8, :] for j in range(nv)]
        order = range(nv - 1, -1, -1) if backward else range(nv)
        edge = 0 if backward else 7
        run = None
        for j in order:
            if run is not None:
                parts[j] = parts[j] + run
            run = parts[j][edge:edge + 1, :]
        return jnp.concatenate(parts, axis=0)

    incl_dir = (colj <= rowi, colj >= rowi)
    strict_dir = (colj < rowi, colj > rowi)

    def each(f, *lists):
        return [f(*a) for a in zip(*lists)]

    def prepare(chains):
        ds = [d for _, d in chains]
        rows = [pl.ds(pl.multiple_of(c * CHUNK, CHUNK), CHUNK) for c, _ in chains]
        srows = [pl.ds(pl.multiple_of(c * LANES, LANES), LANES) for c, _ in chains]
        qn = [qkv_ref[0, r, :] for r in rows]
        kn = [qkv_ref[1, r, :] for r in rows]
        v = [qkv_ref[2, r, :] for r in rows]
        bx = [jnp.where(lo, bg_ref[r, 2 * d:2 * d + 1], bg_ref[r, 2 * d + 1:2 * d + 2]) for r, d in zip(rows, ds)]
        gcum = [chunk_cumsum(bg_ref[r, LANES:2 * LANES], d == 1) for r, d in zip(rows, ds)]
        gx = [jnp.where(lo, g[:, 4 + 2 * d:5 + 2 * d], g[:, 5 + 2 * d:6 + 2 * d]) for g, d in zip(gcum, ds)]
        eg = each(jnp.exp, gx)
        kb = each(jnp.multiply, kn, bx)
        vb = each(jnp.multiply, v, bx)
        kbg = each(jnp.multiply, kb, eg)
        grow = [jnp.sum(jnp.where(rowi == colj, g, 0.0), axis=0, keepdims=True) for g in gx]
        decay = [jnp.where(incl_dir[d], jnp.exp(jnp.where(incl_dir[d], g - gr, 0.0)), 0.0)
                 for g, gr, d in zip(gx, grow, ds)]
        kst = [_blk(_bf(k), lo) for k in kn]
        kq = [_dot_nt(jnp.concatenate([_bf(a), _bf(b)], axis=0), c) for a, b, c in zip(kb, qn, kst)]
        low = [jnp.where(strict_dir[d], x[0:CHUNK] * dc, 0.0) for x, dc, d in zip(kq, decay, ds)]
        intra = [jnp.where(incl_dir[d], x[CHUNK:2 * CHUNK] * dc, 0.0) for x, dc, d in zip(kq, decay, ds)]
        for r, x in zip(rows, zip(ds, qn, eg, intra)):
            d, q_, e_, i_ = x
            qg_ref[d, r, :] = _bf(q_ * e_)
            in_ref[d, r, :] = _bf(i_)
        lb = each(_bf, low)
        p = [_dot(a, _blk(a, lo)) for a in lb]
        nn = [-a for a in low]
        n_sq = int(math.log2(CHUNK)) - 1
        for r_ in range(n_sq):
            pb = each(_bf, p)
            pblk = [_blk(a, lo) for a in pb]
            if r_ < n_sq - 1:
                both = [_dot(jnp.concatenate([_bf(n_), b], axis=0), pk) for n_, b, pk in zip(nn, pb, pblk)]
                nn = [n_ + p_ + x[0:CHUNK] for n_, p_, x in zip(nn, p, both)]
                p = [x[CHUNK:2 * CHUNK] for x in both]
            else:
                nn = [n_ + p_ + _dot(_bf(n_), pk) for n_, p_, pk in zip(nn, p, pblk)]
        nb = each(_bf, nn)
        uw = [_dot(n_, jnp.concatenate([_blk(_bf(a), lo), _blk(_bf(b), lo)], axis=1))
              for n_, a, b in zip(nb, vb, kbg)]
        u = [a + x[:, 0:LANES] for a, x in zip(vb, uw)]
        w = [a + x[:, LANES:2 * LANES] for a, x in zip(kbg, uw)]
        g_last = [g[0:1, :] if d == 1 else g[CHUNK - 1:CHUNK, :] for g, d in zip(gx, ds)]
        kdec = [_bf(k * jnp.exp(gl - g)) for k, gl, g in zip(kn, g_last, gx)]
        wb = each(_bf, w)
        kr = [_dot_tn(a, jnp.concatenate([b, _bf(c)], axis=1)) for a, b, c in zip(kdec, wb, u)]
        for i, (c, d) in enumerate(chains):
            w_ref[d, rows[i], :] = wb[i]
            u_ref[d, rows[i], :] = u[i]
            k2_ref[d, srows[i], :] = _bf(jnp.where(blockmask, kr[i][:, 0:LANES], 0.0))
            r_ref[d, srows[i], :] = jnp.where(blockmask, kr[i][:, LANES:2 * LANES], 0.0)
            el_ref[d, pl.ds(pl.multiple_of(c * 8, 8), 8), :] = jnp.broadcast_to(jnp.exp(g_last[i]), (8, LANES))

    ncc = tc // CHUNK
    nct = tt // CHUNK
    ng = max(g for g in range(1, UNROLL_GDN + 1) if nct % g == 0)

    def group_chains(i):
        return [(i * ng + k, d) for k in range(ng) for d in range(2)]

    def body_a(i, carry):
        prepare(group_chains(i))
        return carry

    lax.fori_loop(0, nct // ng, body_a, 0)

    def advance(c, d, s):
        srows = pl.ds(pl.multiple_of(c * LANES, LANES), LANES)
        s16 = _bf(s)
        s_ref[d, srows, :] = s16
        e = el_ref[d, pl.ds(pl.multiple_of(c * 8, 8), 1), :]
        return s * e - _dot(k2_ref[d, srows, :], s16) + r_ref[d, srows, :]

    def body_b(i, carry):
        sf, sb = carry
        cb = jnp.where(i < ncc, ncc - 1 - i, nct - 1 - (i - ncc))
        return advance(i, 0, sf), advance(cb, 1, sb)

    zero = jnp.zeros((LANES, LANES), F32)
    lax.fori_loop(0, nct, body_b, (zero, zero))

    def emit(chains):
        rows = [pl.ds(pl.multiple_of(c * CHUNK, CHUNK), CHUNK) for c, _ in chains]
        s16 = [s_ref[d, pl.ds(pl.multiple_of(c * LANES, LANES), LANES), :] for c, d in chains]
        ws = [_dot(w_ref[d, r, :], s) for (_, d), r, s in zip(chains, rows, s16)]
        qs = [_dot(qg_ref[d, r, :], s) for (_, d), r, s in zip(chains, rows, s16)]
        v_new = [u_ref[d, r, :] - a for (_, d), r, a in zip(chains, rows, ws)]
        iv = [_dot(in_ref[d, r, :], _blk(_bf(vn), lo)) for (_, d), r, vn in zip(chains, rows, v_new)]
        for (_, d), r, a, b in zip(chains, rows, qs, iv):
            (ob_ref if d == 1 else of_ref)[r, :] = a + b

    def body_c(i, carry):
        emit(group_chains(i))
        return carry

    lax.fori_loop(0, nct // ng, body_c, 0)

    o = of_ref[...] + ob_ref[...]
    ss = _split_dot(o * o, seg)
    o = o * lax.rsqrt(ss * (1.0 / GDN_DK) + EPS) * nw_ref[...]
    o_ref[...] = _bf(o * _silu(z_ref[...]))


def _gdn(gqkv, gz, gba, conv_p, hp, nw, *, B, TT, tc):
    NT = gqkv.shape[0]
    t = TT - tc
    w3 = 3 * LANES
    nch = TT // CHUNK
    kern = functools.partial(_gdn_kernel, tc=tc, t=t)
    return pl.pallas_call(
        kern,
        grid=(B, GDN_PAIRS),
        in_specs=[pl.BlockSpec((TT, w3), lambda b, p: (b, p)),
                  pl.BlockSpec((TT, LANES), lambda b, p: (b, p)),
                  pl.BlockSpec((TT, LANES), lambda b, p: (b, p)),
                  pl.BlockSpec((None, 8, w3), lambda b, p: (p, 0, 0)),
                  pl.BlockSpec((None, 8, LANES), lambda b, p: (p, 0, 0)),
                  pl.BlockSpec((1, LANES), lambda b, p: (0, 0))],
        out_specs=pl.BlockSpec((TT, LANES), lambda b, p: (b, p)),
        out_shape=jax.ShapeDtypeStruct((NT, GDN_W), BF16),
        scratch_shapes=[pltpu.VMEM((3, t + 16, LANES), F32), pltpu.VMEM((3, TT, LANES), F32),
                        pltpu.VMEM((TT, 2 * LANES), F32), pltpu.VMEM((TT, LANES), F32),
                        pltpu.VMEM((TT, LANES), F32),
                        pltpu.VMEM((2, TT, LANES), BF16), pltpu.VMEM((2, TT, LANES), BF16),
                        pltpu.VMEM((2, TT, LANES), BF16), pltpu.VMEM((2, TT, LANES), F32),
                        pltpu.VMEM((2, nch * LANES, LANES), BF16), pltpu.VMEM((2, nch * LANES, LANES), F32),
                        pltpu.VMEM((2, nch * 8, LANES), F32), pltpu.VMEM((2, nch * LANES, LANES), BF16)],
        compiler_params=_cparams(("parallel", "parallel")),
        name="gdn",
    )(gqkv, gz, gba, conv_p, hp, nw)


def _att_tile(q_ref, k_ref, v_ref, o_ref, nk):
    lo = lax.broadcasted_iota(jnp.int32, (1, LANES), 1) < HALF
    k = k_ref[0:nk, :]
    v = v_ref[0:nk, :]

    def scores(h):
        j, half = h // 2, h % 2
        qj = q_ref[:, j * LANES:(j + 1) * LANES]
        zq = jnp.zeros_like(qj)
        return _dot_nt(jnp.where(lo, qj, zq) if half == 0 else jnp.where(lo, zq, qj), k)

    outs = []
    s = scores(0)
    for h in range(ATT_HEADS):
        s_next = scores(h + 1) if h + 1 < ATT_HEADS else None
        m = jnp.max(s, axis=-1, keepdims=True)
        p = jnp.exp(s - m)
        l = jnp.sum(p, axis=-1, keepdims=True)
        outs.append(_dot(_bf(p), v) / l)
        s = s_next
    for j in range(ATT_GROUP):
        o_ref[:, j * LANES:(j + 1) * LANES] = _bf(jnp.where(lo, outs[2 * j], outs[2 * j + 1]))


def _att_kernel(q_ref, k_ref, v_ref, o_ref, *, n_ctx_tiles, tc, tt):
    is_ctx = pl.program_id(1) < n_ctx_tiles

    @pl.when(is_ctx)
    def _():
        _att_tile(q_ref, k_ref, v_ref, o_ref, tc)

    @pl.when(jnp.logical_not(is_ctx))
    def _():
        _att_tile(q_ref, k_ref, v_ref, o_ref, tt)


def _attention(aq, ak, av, *, B, TT, tc):
    NT = aq.shape[0]
    tq = 256
    nq = TT // tq
    kv_spec = pl.BlockSpec((TT, LANES), lambda b, i: (b, 0))
    q_spec = pl.BlockSpec((tq, ATT_W), lambda b, i: (b * nq + i, 0))
    return pl.pallas_call(
        functools.partial(_att_kernel, n_ctx_tiles=tc // tq, tc=tc, tt=TT),
        grid=(B, nq),
        in_specs=[q_spec, kv_spec, kv_spec],
        out_specs=q_spec,
        out_shape=jax.ShapeDtypeStruct((NT, ATT_W), BF16),
        compiler_params=_cparams(("parallel", "parallel")),
        name="attention",
    )(aq, ak, av)


S5_BB = 2
GPV = LANES // S5_GH


def _s5_kernel(u0_ref, u1_ref, tg_ref, bp_ref, cp_ref, la_ref, lb_ref, d_ref, gw_ref, gb_ref, o_ref,
               uf_ref, sloc_ref, sin_ref, y0_ref, y1_ref, *, bb, ncc, nct):
    lc = S5_CHUNK
    nr = bb * nct
    w2 = 2 * S5_P
    blk8 = lax.broadcasted_iota(jnp.int32, (1, LANES), 1) // S5_GH
    u_refs = (u0_ref, u1_ref)
    y_refs = (y0_ref, y1_ref)

    def block_transpose(xs):
        xs = list(xs)
        for s in (4, 2, 1):
            keep = (blk8 & s) == 0
            for a in range(GPV):
                if a & s == 0:
                    lo_, hi_ = xs[a], xs[a + s]
                    xs[a] = jnp.where(keep, lo_, pltpu.roll(hi_, s * S5_GH, 1))
                    xs[a + s] = jnp.where(keep, pltpu.roll(lo_, LANES - s * S5_GH, 1), hi_)
        return xs

    rbk = 32
    for a in range(S5_GROUPS // GPV):
        for v in range(lc // GPV):
            for r0 in range(0, nr, rbk):
                xs = [u_refs[a][pl.ds((v * GPV + k) + lc * r0, rbk, stride=lc), :] for k in range(GPV)]
                for k, y_ in enumerate(block_transpose(xs)):
                    uf_ref[a * GPV + k, r0:r0 + rbk, v * LANES:(v + 1) * LANES] = y_
    for g in range(S5_GROUPS):
        sl = _dot(_bf(uf_ref[g]), bp_ref[g])
        sloc_ref[2 * g] = sl[:, 0:w2]
        sloc_ref[2 * g + 1] = sl[:, w2:2 * w2]

    def body(i, carry):
        cb = jnp.where(i < ncc, ncc - 1 - i, nct - 1 - (i - ncc))
        out = []
        for g in range(S5_GROUPS):
            for d in range(2):
                s = carry[2 * g + d]
                rows = pl.ds(i if d == 0 else cb, bb, stride=nct)
                lanes = slice(d * w2, (d + 1) * w2)
                sin_ref[2 * g + d, rows, :] = s
                out.append(s * la_ref[g, :, lanes] + pltpu.roll(s, S5_P, 1) * lb_ref[g, :, lanes]
                           + sloc_ref[2 * g + d, rows, :])
        return tuple(out)

    zero = jnp.zeros((bb, w2), F32)
    lax.fori_loop(0, nct, body, (zero,) * (2 * S5_GROUPS))

    for g in range(S5_GROUPS):
        uf_ref[g] = (_dot(_bf(uf_ref[g]), tg_ref[g])
                     + _dot(_bf(sin_ref[2 * g]), cp_ref[g, 0:w2, :])
                     + _dot(_bf(sin_ref[2 * g + 1]), cp_ref[g, w2:2 * w2, :]))

    for a in range(S5_GROUPS // GPV):
        for v in range(lc // GPV):
            for r0 in range(0, nr, rbk):
                xs = [uf_ref[a * GPV + k, r0:r0 + rbk, v * LANES:(v + 1) * LANES] for k in range(GPV)]
                for k, y_ in enumerate(block_transpose(xs)):
                    y_refs[a][pl.ds((v * GPV + k) + lc * r0, rbk, stride=lc), :] = y_

    y = jnp.concatenate([y0_ref[...], y1_ref[...]], axis=1)
    u = jnp.concatenate([u0_ref[...], u1_ref[...]], axis=1)
    zz = _gelu_tanh(y + d_ref[...] * u)
    o_ref[...] = _bf(zz * _sigmoid(_dot(_bf(zz), gw_ref[...]) + gb_ref[...]))


def _s5(su, tg, bp, cp, la, lb, d, gw, gb, *, B, TT, tc):
    NT, W = su.shape
    bb = S5_BB
    nct = TT // S5_CHUNK
    nr = bb * nct
    kern = functools.partial(_s5_kernel, bb=bb, ncc=tc // S5_CHUNK, nct=nct)
    full = lambda a: pl.BlockSpec(a.shape, lambda i: (0,) * a.ndim)
    return pl.pallas_call(
        kern,
        grid=(B // bb,),
        in_specs=[pl.BlockSpec((bb * TT, LANES), lambda i: (i, 0)),
                  pl.BlockSpec((bb * TT, LANES), lambda i: (i, 1)),
                  full(tg), full(bp), full(cp), full(la), full(lb), full(d), full(gw), full(gb)],
        out_specs=pl.BlockSpec((bb * TT, W), lambda i: (i, 0)),
        out_shape=jax.ShapeDtypeStruct((NT, W), BF16),
        scratch_shapes=[pltpu.VMEM((S5_GROUPS, nr, W), F32), pltpu.VMEM((2 * S5_GROUPS, nr, LANES), F32),
                        pltpu.VMEM((2 * S5_GROUPS, nr, LANES), F32),
                        pltpu.VMEM((bb * TT, LANES), F32), pltpu.VMEM((bb * TT, LANES), F32)],
        compiler_params=_cparams(("parallel",)),
        name="s5",
    )(su, su, tg, bp, cp, la, lb, d, gw, gb)


def _s5_matrices(lam_re, lam_im, log_dt, b_re, b_im, c_re, c_im):
    Lc = S5_CHUNK
    lam = lax.complex(lam_re.astype(F32), lam_im.astype(F32))
    dt = jnp.exp(log_dt.astype(F32))[..., None]
    lam_dt = lam * dt
    lam_bar = jnp.exp(lam_dt)
    b_bar = ((lam_bar - 1.0) / lam)[..., None] * lax.complex(b_re.astype(F32), b_im.astype(F32))
    cc = lax.complex(c_re.astype(F32), c_im.astype(F32))
    ks = jnp.arange(Lc + 1, dtype=F32)
    pw = jnp.exp(lam_dt[..., None, :] * ks[:, None])
    kern = jnp.real(jnp.einsum('dghp,dgkp,dgpi->dgkhi', cc, pw[:, :, :Lc], b_bar))
    i_idx = jnp.arange(Lc)[:, None]
    j_idx = jnp.arange(Lc)[None, :]
    lag = i_idx - j_idx
    kf = kern[0][:, jnp.clip(lag, 0, Lc - 1)]
    kb = kern[1][:, jnp.clip(-lag, 0, Lc - 1)]
    tmat = (jnp.where((lag >= 0)[None, :, :, None, None], kf, 0.0)
            + jnp.where((lag <= 0)[None, :, :, None, None], kb, 0.0))
    tg = tmat.transpose(0, 2, 4, 1, 3).reshape(S5_GROUPS, Lc * S5_GH, Lc * S5_GH)
    pf = pw[0][:, Lc - 1 - jnp.arange(Lc)]
    pb = pw[1][:, jnp.arange(Lc)]
    sf = jnp.einsum('gjp,gph->gjhp', pf, b_bar[0])
    sb = jnp.einsum('gjp,gph->gjhp', pb, b_bar[1])
    bp = jnp.concatenate([jnp.real(sf), jnp.imag(sf), jnp.real(sb), jnp.imag(sb)], axis=-1)
    bp = bp.reshape(S5_GROUPS, Lc * S5_GH, 4 * S5_P)
    qf = jnp.einsum('ghp,gip->gpih', cc[0], pw[0][:, 1 + jnp.arange(Lc)])
    qb = jnp.einsum('ghp,gip->gpih', cc[1], pw[1][:, Lc - jnp.arange(Lc)])
    cp = jnp.concatenate([jnp.real(qf), -jnp.imag(qf), jnp.real(qb), -jnp.imag(qb)], axis=1)
    cp = cp.reshape(S5_GROUPS, 4 * S5_P, Lc * S5_GH)
    lc = pw[:, :, Lc]
    la = jnp.concatenate([jnp.real(lc[0]), jnp.real(lc[0]), jnp.real(lc[1]), jnp.real(lc[1])], axis=-1)
    lb = jnp.concatenate([-jnp.imag(lc[0]), jnp.imag(lc[0]), -jnp.imag(lc[1]), jnp.imag(lc[1])], axis=-1)
    return (tg.astype(BF16), bp.astype(BF16), cp.astype(BF16),
            la.reshape(S5_GROUPS, 1, 4 * S5_P), lb.reshape(S5_GROUPS, 1, 4 * S5_P))


def _gelu_tanh(x):
    return 0.5 * x * (1.0 + jnp.tanh(math.sqrt(2.0 / math.pi) * (x + 0.044715 * (x * x * x))))


def _kout_kernel(x_ref, mod_c_ref, mod_b_ref, go_ref, ao_ref, so_ref,
                 wg_ref, wa_ref, ws_ref, g_ref, b_ref, o_ref, *, tm, tc, tpb):
    rm = functools.partial(_row_mod, mod_c_ref, mod_b_ref, _ctx_rows(tm, tc, tpb))
    y = _dot(go_ref[...], wg_ref[...]) + _dot(ao_ref[...], wa_ref[...]) + _dot(so_ref[...], ws_ref[...])
    z = ALPHA * x_ref[...] + rm(5) * y
    o_ref[...] = _layer_norm(z, g_ref[...], b_ref[...])


def _kout(xa, mod, go, ao, so, wg, wa, ws, g, b, *, B, TT, tc):
    NT, D = xa.shape
    tpb = 4
    tm = TT // tpb
    kern = functools.partial(_kout_kernel, tm=tm, tc=tc, tpb=tpb)
    row = lambda w: pl.BlockSpec((tm, w), lambda i: (i, 0))
    full = lambda a: pl.BlockSpec(a.shape, lambda i: (0,) * a.ndim)
    return pl.pallas_call(
        kern,
        grid=(NT // tm,),
        in_specs=[row(D),
                  pl.BlockSpec((None, N_MOD, D), lambda i: (B, 0, 0)),
                  pl.BlockSpec((None, N_MOD, D), lambda i: (i // tpb, 0, 0)),
                  row(GDN_W), row(ATT_W), row(S5_W),
                  full(wg), full(wa), full(ws), full(g), full(b)],
        out_specs=row(D),
        out_shape=jax.ShapeDtypeStruct((NT, D), F32),
        compiler_params=_cparams(("parallel",)),
        name="mixer_out",
    )(xa, mod, mod, go, ao, so, wg, wa, ws, g, b)


def _perm_in_cols():
    off_z = GDN_QKV
    off_b = off_z + GDN_W
    off_a = off_b + 2 * GDN_HEADS
    off_q = off_a + 2 * GDN_HEADS
    off_k = off_q + ATT_W
    off_v = off_k + ATT_KV_HEADS * ATT_DH
    off_u = off_v + ATT_KV_HEADS * ATT_DH
    cols = []
    for p in range(GDN_PAIRS):
        for part in range(3):
            base = part * GDN_W + 2 * p * GDN_DK
            cols += list(range(base, base + 2 * GDN_DK))
    cols += list(range(off_z, off_z + GDN_W))
    for p in range(GDN_PAIRS):
        blk = []
        for off in (off_b, off_a):
            for d in range(2):
                blk += [off + d * GDN_HEADS + 2 * p, off + d * GDN_HEADS + 2 * p + 1]
        cols += blk + [-1] * (LANES - len(blk))
    for j in range(ATT_GROUP):
        for h in (j, j + ATT_GROUP):
            cols += list(range(off_q + h * ATT_DH, off_q + (h + 1) * ATT_DH))
    cols += list(range(off_k, off_k + 2 * ATT_DH))
    cols += list(range(off_v, off_v + 2 * ATT_DH))
    cols += list(range(off_u, off_u + S5_W))
    return np.asarray(cols, dtype=np.int32)


def _gdn_pair_cols():
    cols = []
    for p in range(GDN_PAIRS):
        for part in range(3):
            base = part * GDN_W + 2 * p * GDN_DK
            cols += list(range(base, base + 2 * GDN_DK))
    return np.asarray(cols, dtype=np.int32)


def _att_out_rows():
    rows = []
    for j in range(ATT_GROUP):
        for h in (j, j + ATT_GROUP):
            rows += list(range(h * ATT_DH, (h + 1) * ATT_DH))
    return np.asarray(rows, dtype=np.int32)


def _take(w, idx, axis):
    idx = [int(i) for i in idx]
    parts = []
    i = 0
    while i < len(idx):
        j = i + 1
        if idx[i] < 0:
            while j < len(idx) and idx[j] < 0:
                j += 1
            shape = list(w.shape)
            shape[axis] = j - i
            parts.append(jnp.zeros(shape, w.dtype))
        else:
            while j < len(idx) and idx[j] == idx[j - 1] + 1:
                j += 1
            parts.append(lax.slice_in_dim(w, idx[i], idx[j - 1] + 1, axis=axis))
        i = j
    return jnp.concatenate(parts, axis=axis)


def _rope_tables(tc, t):
    rows = t // GRID_W
    row = jnp.repeat(jnp.arange(rows), GRID_W)
    col = jnp.tile(jnp.arange(GRID_W), rows)
    inv_freq = ROPE_THETA ** (-jnp.arange(ROPE_PAIRS, dtype=F32) / ROPE_PAIRS)
    ang = jnp.stack([row, col], axis=-1).astype(F32)[..., None] * inv_freq
    cos, sin = jnp.cos(ang), jnp.sin(ang)
    zero = jnp.zeros_like(sin)
    cos_h = jnp.stack([cos, cos], axis=2).reshape(t, ATT_DH)
    sa_h = jnp.stack([-sin, zero], axis=2).reshape(t, ATT_DH)
    sb_h = jnp.stack([zero, sin], axis=2).reshape(t, ATT_DH)
    def full(tab, fill):
        tab = jnp.concatenate([jnp.full((tc, ATT_DH), fill, F32), tab], axis=0)
        return jnp.concatenate([tab, tab], axis=1)
    return full(cos_h, 1.0), full(sa_h, 0.0), full(sb_h, 0.0)


def kernel(x, c, ctx, c_ctx, w_ada, b_ada, ln_g, ln_b, ffn_w1, ffn_w3, ffn_w2, w_in, w_out, gdn_conv_w,
           gdn_a_log, gdn_dt_bias, gdn_norm_w, q_norm_w, k_norm_w, s5_lam_re, s5_lam_im, s5_log_dt,
           s5_b_re, s5_b_im, s5_c_re, s5_c_im, s5_d, glu_w, glu_b):
    B, T, D = x.shape
    tc = ctx.shape[1]
    TT = tc + T
    NT = B * TT
    L = w_ada.shape[0]
    kw = dict(B=B, TT=TT, tc=tc)

    R = ((B + 1 + 7) // 8) * 8
    c_all = jnp.concatenate([c, c_ctx[None, :], jnp.zeros((R - B - 1, D), F32)], axis=0)
    mod_all = _modulation(c_all, w_ada, b_ada).reshape(L, R, N_MOD, D)

    in_cols = _perm_in_cols()
    pair_cols = _gdn_pair_cols()
    att_rows = _att_out_rows()
    cos, sa, sb = _rope_tables(tc, T)

    def head_params(v):
        v = v.astype(F32).reshape(2, GDN_PAIRS, 2).transpose(1, 0, 2).reshape(GDN_PAIRS, 4)
        return jnp.pad(v, ((0, 0), (4, LANES - 8)))

    xa = jnp.concatenate([ctx, x], axis=1).reshape(NT, D)
    for l in range(L):
        mod = mod_all[l]
        g_l, b_l = ln_g[l], ln_b[l]
        xa = _ffn(xa, mod, _ffn_pack(ffn_w1[l, 0], ffn_w3[l, 0], ffn_w2[l, 0]), g_l[0:1], b_l[0:1], sub=0, **kw)

        w_in_p = _take(_bf(w_in[l]), in_cols, 1)
        qw = jnp.tile(q_norm_w[l], 2)[None, :]
        kwt = jnp.tile(k_norm_w[l], 2)[None, :]
        gqkv, gz, gba, aq, ak, av, su = _kin(xa, mod, w_in_p, qw, kwt, cos, sa, sb, **kw)

        conv_p = jnp.pad(_take(gdn_conv_w[l], pair_cols, 1), ((0, 8 - CONV_K), (0, 0)))
        conv_p = conv_p.reshape(8, GDN_PAIRS, 3 * LANES).transpose(1, 0, 2)
        hp = jnp.stack([head_params(-jnp.exp(gdn_a_log[l].astype(F32))), head_params(gdn_dt_bias[l])], axis=1)
        hp = jnp.pad(hp, ((0, 0), (0, 6), (0, 0)))
        nw = jnp.tile(gdn_norm_w[l], 2)[None, :]
        go = _gdn(gqkv, gz, gba, conv_p, hp, nw, **kw)

        ao = _attention(aq, ak, av, **kw)

        tg, bp, cp, la, lb = _s5_matrices(s5_lam_re[l], s5_lam_im[l], s5_log_dt[l], s5_b_re[l], s5_b_im[l],
                                          s5_c_re[l], s5_c_im[l])
        so = _s5(su, tg, bp, cp, la, lb, s5_d[l][None, :], _bf(glu_w[l]), glu_b[l][None, :], **kw)

        wo = _bf(w_out[l])
        wg = wo[0:GDN_W]
        wa = _take(wo[GDN_W:GDN_W + ATT_W], att_rows, 0)
        ws = wo[GDN_W + ATT_W:]
        xa = _kout(xa, mod, go, ao, so, wg, wa, ws, g_l[1:2], b_l[1:2], **kw)
        xa = _ffn(xa, mod, _ffn_pack(ffn_w1[l, 1], ffn_w3[l, 1], ffn_w2[l, 1]), g_l[2:3], b_l[2:3], sub=2, **kw)
    return xa.reshape(B, TT, D)[:, tc:, :]
```

```python
import functools
import math

import numpy as np
import jax
import jax.numpy as jnp
from jax import lax
from jax.experimental import pallas as pl
from jax.experimental.pallas import tpu as pltpu

F32 = jnp.float32
BF16 = jnp.bfloat16

D_MODEL = 1024
GRID_W = 64
GDN_HEADS = 6
GDN_DK = 64
GDN_PAIRS = GDN_HEADS // 2
GDN_QKV = 3 * GDN_HEADS * GDN_DK
GDN_W = GDN_HEADS * GDN_DK
CONV_K = 5
CHUNK = 64
UNROLL_GDN = 9
ATT_HEADS = 6
ATT_KV_HEADS = 2
ATT_GROUP = ATT_HEADS // ATT_KV_HEADS
ATT_DH = 64
ATT_W = ATT_HEADS * ATT_DH
ROPE_THETA = 10000.0
ROPE_PAIRS = ATT_DH // 4
S5_GROUPS = 16
S5_GH = 16
S5_P = 64
S5_W = S5_GROUPS * S5_GH
S5_CHUNK = 16
N_MOD = 9
DEPTH = 4
ALPHA = (2.0 * DEPTH) ** 0.25
EPS = 1e-6
LANES = 128
HALF = LANES // 2

C_QKV = 0
C_Z = C_QKV + GDN_QKV
C_BA = C_Z + GDN_W
C_AQ = C_BA + GDN_PAIRS * LANES
C_AK = C_AQ + ATT_W
C_AV = C_AK + LANES
C_SU = C_AV + LANES
C_END = C_SU + S5_W

VMEM_LIMIT = 56 * 1024 * 1024


def _cparams(sem):
    return pltpu.CompilerParams(dimension_semantics=sem, vmem_limit_bytes=VMEM_LIMIT)


def _bf(x):
    return x.astype(BF16)


def _dot(a, b):
    return jnp.dot(a, b, preferred_element_type=F32)


def _dot_nt(a, b):
    return lax.dot_general(a, b, (((1,), (1,)), ((), ())), preferred_element_type=F32)


def _dot_tn(a, b):
    return lax.dot_general(a, b, (((0,), (0,)), ((), ())), preferred_element_type=F32)


def _split_dot(x, m):
    hi = _bf(x)
    lo = _bf(x - hi.astype(F32))
    return _dot(hi, m) + _dot(lo, m)


def _split3_dot_left(m, x):
    hi = _bf(x)
    r1 = x - hi.astype(F32)
    mid = _bf(r1)
    lo = _bf(r1 - mid.astype(F32))
    return _dot(m, hi) + _dot(m, mid) + _dot(m, lo)


def _sigmoid(x):
    return 0.5 + 0.5 * jnp.tanh(0.5 * x)


def _silu(x):
    h = 0.5 * x
    return h + h * jnp.tanh(h)


def _layer_norm(z, g, b):
    mu = jnp.mean(z, axis=-1, keepdims=True)
    zc = z - mu
    var = jnp.mean(zc * zc, axis=-1, keepdims=True)
    return zc * lax.rsqrt(var + EPS) * g + b


def _ctx_rows(tm, tc, tiles_per_batch):
    row0 = (pl.program_id(0) % tiles_per_batch) * tm
    return row0 + lax.broadcasted_iota(jnp.int32, (tm, 1), 0) < tc


def _row_mod(mod_c_ref, mod_b_ref, is_ctx, idx):
    return jnp.where(is_ctx, mod_c_ref[idx:idx + 1, :], mod_b_ref[idx:idx + 1, :])


def _mod_kernel(c_ref, w_ref, b_ref, o_ref):
    sc = _silu(c_ref[...])
    o_ref[...] = jnp.dot(sc, w_ref[...], preferred_element_type=F32,
                         precision=lax.Precision.HIGHEST) + b_ref[...]


def _modulation(c_all, w_ada, b_ada):
    L, D, NM = w_ada.shape
    R = c_all.shape[0]
    tn = 1152
    return pl.pallas_call(
        _mod_kernel,
        grid=(L, NM // tn),
        in_specs=[pl.BlockSpec((R, D), lambda l, j: (0, 0)),
                  pl.BlockSpec((None, D, tn), lambda l, j: (l, 0, j)),
                  pl.BlockSpec((None, 1, tn), lambda l, j: (l, 0, j))],
        out_specs=pl.BlockSpec((None, R, tn), lambda l, j: (l, 0, j)),
        out_shape=jax.ShapeDtypeStruct((L, R, NM), F32),
        compiler_params=_cparams(("parallel", "parallel")),
        name="adaln_mod",
    )(c_all, w_ada, b_ada.reshape(L, 1, NM))


def _ffn_kernel(x_ref, mod_c_ref, mod_b_ref, modp_b_ref, w13_ref, w2_ref, g_ref, b_ref, o_ref,
                hm_ref, acc_ref, xk_ref, *, sub, tm, tc, tpb, nf, nt):
    i = pl.program_id(0)
    f = pl.program_id(1)
    slot = i % 2
    row = lax.broadcasted_iota(jnp.int32, (tm, 1), 0)
    rm = functools.partial(_row_mod, mod_c_ref, mod_b_ref, (i % tpb) * tm + row < tc)
    nsteps = -(-nf // FFN_GROUP)
    qr = tm // nsteps

    def modulate():
        x = x_ref[...]
        xk_ref[slot] = x
        hm_ref[...] = _bf(x * (1.0 + rm(3 * sub + 1)) + rm(3 * sub))

    def accumulate(step):
        hm = hm_ref[...]
        for c in range(min(FFN_GROUP, nf - step * FFN_GROUP)):
            h = _dot(hm, w13_ref[c])
            part = _dot(_bf(_silu(h[:, 0:FFN_FC]) * h[:, FFN_FC:2 * FFN_FC]), w2_ref[c])
            if step == 0 and c == 0:
                acc_ref[slot] = part
            else:
                acc_ref[slot] += part

    def finish_previous():
        rg = 32
        for k in range(qr // rg):
            r0 = pl.multiple_of(f * qr + k * rg, 8)
            rows = pl.ds(r0, rg)
            prev_row = ((i + tpb - 1) % tpb) * tm + r0 + lax.broadcasted_iota(jnp.int32, (rg, 1), 0)
            gate = _row_mod(mod_c_ref, modp_b_ref, prev_row < tc, 3 * sub + 2)
            z = ALPHA * xk_ref[1 - slot, rows, :] + 0.5 * gate * acc_ref[1 - slot, rows, :]
            o_ref[rows, :] = _layer_norm(z, g_ref[...], b_ref[...])

    do_mm = i < nt
    do_fin = i > 0
    for step in range(nsteps):
        here = f == step

        @pl.when(here & do_mm & do_fin)
        def _(step=step):
            finish_previous()
            if step == 0:
                modulate()
            accumulate(step)

        @pl.when(here & do_mm & jnp.logical_not(do_fin))
        def _(step=step):
            if step == 0:
                modulate()
            accumulate(step)

    @pl.when(jnp.logical_not(do_mm) & do_fin)
    def _():
        finish_previous()


FFN_FC = 256
FFN_GROUP = 6
FFN_TPB = 3


def _ffn_pack(w1, w3, w2):
    D, F = w1.shape
    nf = F // FFN_FC
    pad = -nf % FFN_GROUP
    w13 = jnp.concatenate([_bf(w1).reshape(D, nf, FFN_FC), _bf(w3).reshape(D, nf, FFN_FC)], axis=2)
    w13 = jnp.pad(w13.transpose(1, 0, 2), ((0, pad), (0, 0), (0, 0)))
    w2p = jnp.pad(_bf(w2).reshape(nf, FFN_FC, D), ((0, pad), (0, 0), (0, 0)))
    return w13, w2p, nf


def _ffn(xa, mod, packed, g, b, *, sub, B, TT, tc):
    NT, D = xa.shape
    w13, w2, nf = packed
    fc = FFN_FC
    tpb = FFN_TPB
    tm = TT // tpb
    nt = NT // tm
    nsteps = w13.shape[0] // FFN_GROUP
    kern = functools.partial(_ffn_kernel, sub=sub, tm=tm, tc=tc, tpb=tpb, nf=nf, nt=nt)
    cur = lambda i: jnp.minimum(i, nt - 1)
    prev = lambda i: jnp.maximum(i - 1, 0)
    return pl.pallas_call(
        kern,
        grid=(nt + 1, nsteps),
        in_specs=[pl.BlockSpec((tm, D), lambda i, f: (cur(i + jnp.minimum(f, 1)), 0)),
                  pl.BlockSpec((None, N_MOD, D), lambda i, f: (B, 0, 0)),
                  pl.BlockSpec((None, N_MOD, D), lambda i, f: (cur(i) // tpb, 0, 0)),
                  pl.BlockSpec((None, N_MOD, D), lambda i, f: (prev(i) // tpb, 0, 0)),
                  pl.BlockSpec((FFN_GROUP, D, 2 * fc), lambda i, f: (f, 0, 0)),
                  pl.BlockSpec((FFN_GROUP, fc, D), lambda i, f: (f, 0, 0)),
                  pl.BlockSpec((1, D), lambda i, f: (0, 0)),
                  pl.BlockSpec((1, D), lambda i, f: (0, 0))],
        out_specs=pl.BlockSpec((tm, D), lambda i, f: (prev(i), 0)),
        out_shape=jax.ShapeDtypeStruct((NT, D), F32),
        scratch_shapes=[pltpu.VMEM((tm, D), BF16), pltpu.VMEM((2, tm, D), F32), pltpu.VMEM((2, tm, D), F32)],
        compiler_params=_cparams(("arbitrary", "arbitrary")),
        name=f"ffn{sub}",
    )(xa, mod, mod, mod, w13, w2, g, b)


def _seg_ones():
    r = lax.broadcasted_iota(jnp.int32, (LANES, LANES), 0) // HALF
    c = lax.broadcasted_iota(jnp.int32, (LANES, LANES), 1) // HALF
    return jnp.where(r == c, 1.0, 0.0).astype(BF16)


def _kin_kernel(x_ref, mod_c_ref, mod_b_ref, w_ref, qw_ref, kw_ref, cos_ref, sa_ref, sb_ref,
                gqkv_ref, gz_ref, gba_ref, aq_ref, ak_ref, av_ref, su_ref, *, tm, tc, tpb):
    rm = functools.partial(_row_mod, mod_c_ref, mod_b_ref, _ctx_rows(tm, tc, tpb))
    hm = _bf(x_ref[...] * (1.0 + rm(4)) + rm(3))
    r = _dot(hm, w_ref[...])
    gqkv_ref[...] = r[:, C_QKV:C_Z]
    gz_ref[...] = r[:, C_Z:C_BA]
    gba_ref[...] = r[:, C_BA:C_AQ]
    av_ref[...] = _bf(r[:, C_AV:C_SU])
    su_ref[...] = r[:, C_SU:C_END]

    seg = _seg_ones()
    cos, sa, sb = cos_ref[...], sa_ref[...], sb_ref[...]

    def norm_rope(blk, w_row):
        ss = _split_dot(blk * blk, seg)
        y = blk * lax.rsqrt(ss * (1.0 / ATT_DH) + EPS) * w_row
        return y * cos + pltpu.roll(y, LANES - ROPE_PAIRS, 1) * sa + pltpu.roll(y, ROPE_PAIRS, 1) * sb

    qw = qw_ref[...] * (ATT_DH ** -0.5)
    for j in range(ATT_W // LANES):
        aq_ref[:, j * LANES:(j + 1) * LANES] = _bf(
            norm_rope(r[:, C_AQ + j * LANES:C_AQ + (j + 1) * LANES], qw))
    ak_ref[...] = _bf(norm_rope(r[:, C_AK:C_AV], kw_ref[...]))


def _kin(xa, mod, w_in_p, qw, kw, cos, sa, sb, *, B, TT, tc):
    NT, D = xa.shape
    tpb = 4
    tm = TT // tpb
    kern = functools.partial(_kin_kernel, tm=tm, tc=tc, tpb=tpb)
    widths = [(GDN_QKV, F32), (GDN_W, F32), (GDN_PAIRS * LANES, F32), (ATT_W, BF16),
              (LANES, BF16), (LANES, BF16), (S5_W, F32)]
    return pl.pallas_call(
        kern,
        grid=(NT // tm,),
        in_specs=[pl.BlockSpec((tm, D), lambda i: (i, 0)),
                  pl.BlockSpec((None, N_MOD, D), lambda i: (B, 0, 0)),
                  pl.BlockSpec((None, N_MOD, D), lambda i: (i // tpb, 0, 0)),
                  pl.BlockSpec((D, C_END), lambda i: (0, 0)),
                  pl.BlockSpec((1, LANES), lambda i: (0, 0)),
                  pl.BlockSpec((1, LANES), lambda i: (0, 0)),
                  pl.BlockSpec((tm, LANES), lambda i: (i % tpb, 0)),
                  pl.BlockSpec((tm, LANES), lambda i: (i % tpb, 0)),
                  pl.BlockSpec((tm, LANES), lambda i: (i % tpb, 0))],
        out_specs=[pl.BlockSpec((tm, w), lambda i: (i, 0)) for w, _ in widths],
        out_shape=[jax.ShapeDtypeStruct((NT, w), dt) for w, dt in widths],
        compiler_params=_cparams(("parallel",)),
        name="mixer_in",
    )(xa, mod, mod, w_in_p, qw, kw, cos, sa, sb)


def _blk(x, lo):
    z = jnp.zeros_like(x)
    return jnp.concatenate([jnp.where(lo, x, z), jnp.where(lo, z, x)], axis=0)


def _gdn_kernel(qkv_in_ref, z_ref, ba_ref, cw_ref, hp_ref, nw_ref, o_ref,
                xp_ref, qkv_ref, bg_ref, of_ref, ob_ref,
                qg_ref, in_ref, w_ref, u_ref, k2_ref, r_ref, el_ref, s_ref, *, tc, t):
    tt = tc + t
    seg = _seg_ones()

    pad = 8
    for off, n in ((0, tc), (tc, t)):
        for part in range(3):
            xp_ref[part, 0:pad, :] = jnp.zeros((pad, LANES), F32)
            xp_ref[part, pad:pad + n, :] = qkv_in_ref[off:off + n, part * LANES:(part + 1) * LANES]
            xp_ref[part, pad + n:2 * pad + n, :] = jnp.zeros((pad, LANES), F32)
        nb = n // 8
        rb = min(nb, 128)
        for r in range(8):
            for i0 in range(0, nb, rb):
                for part in range(3):
                    lanes = slice(part * LANES, (part + 1) * LANES)
                    base = pad - CONV_K // 2 + r + 8 * i0
                    acc = cw_ref[0:1, lanes] * xp_ref[part, pl.ds(base, rb, stride=8), :]
                    for j in range(1, CONV_K):
                        acc = acc + cw_ref[j:j + 1, lanes] * xp_ref[part, pl.ds(base + j, rb, stride=8), :]
                    y = _silu(acc)
                    if part == 0:
                        y = y * lax.rsqrt(_split_dot(y * y, seg) + EPS) * (GDN_DK ** -0.5)
                    elif part == 1:
                        y = y * lax.rsqrt(_split_dot(y * y, seg) + EPS)
                    qkv_ref[part, pl.ds(off + r + 8 * i0, rb, stride=8), :] = y

    ba = ba_ref[...]
    bg_ref[:, 0:LANES] = _sigmoid(ba)
    sp_in = ba + hp_ref[1:2, :]
    softplus = jnp.maximum(sp_in, 0.0) + jnp.log1p(jnp.exp(-jnp.abs(sp_in)))
    bg_ref[:, LANES:2 * LANES] = hp_ref[0:1, :] * softplus

    rowi = lax.broadcasted_iota(jnp.int32, (CHUNK, LANES), 0)
    colj = lax.broadcasted_iota(jnp.int32, (CHUNK, LANES), 1) % HALF
    lo = lax.broadcasted_iota(jnp.int32, (CHUNK, LANES), 1) < HALF
    ti = lax.broadcasted_iota(jnp.int32, (CHUNK, CHUNK), 0)
    tj = lax.broadcasted_iota(jnp.int32, (CHUNK, CHUNK), 1)
    br = lax.broadcasted_iota(jnp.int32, (LANES, LANES), 0) // HALF
    bc = lax.broadcasted_iota(jnp.int32, (LANES, LANES), 1) // HALF
    blockmask = br == bc

    sub8 = rowi % 8
    nv = CHUNK // 8

    def chunk_cumsum(x, backward):
        for s in (1, 2, 4):
            if backward:
                x = x + jnp.where(sub8 < 8 - s, pltpu.roll(x, CHUNK - s, 0), 0.0)
            else:
                x = x + jnp.where(sub8 >= s, pltpu.roll(x, s, 0), 0.0)
        parts = [x[8 * j:8 * j + 8, :] for j in range(nv)]
        order = range(nv - 1, -1, -1) if backward else range(nv)
        edge = 0 if backward else 7
        run = None
        for j in order:
            if run is not None:
                parts[j] = parts[j] + run
            run = parts[j][edge:edge + 1, :]
        return jnp.concatenate(parts, axis=0)

    incl_dir = (colj <= rowi, colj >= rowi)
    strict_dir = (colj < rowi, colj > rowi)

    def each(f, *lists):
        return [f(*a) for a in zip(*lists)]

    def prepare(chains):
        ds = [d for _, d in chains]
        rows = [pl.ds(pl.multiple_of(c * CHUNK, CHUNK), CHUNK) for c, _ in chains]
        srows = [pl.ds(pl.multiple_of(c * LANES, LANES), LANES) for c, _ in chains]
        qn = [qkv_ref[0, r, :] for r in rows]
        kn = [qkv_ref[1, r, :] for r in rows]
        v = [qkv_ref[2, r, :] for r in rows]
        bx = [jnp.where(lo, bg_ref[r, 2 * d:2 * d + 1], bg_ref[r, 2 * d + 1:2 * d + 2]) for r, d in zip(rows, ds)]
        gcum = [chunk_cumsum(bg_ref[r, LANES:2 * LANES], d == 1) for r, d in zip(rows, ds)]
        gx = [jnp.where(lo, g[:, 4 + 2 * d:5 + 2 * d], g[:, 5 + 2 * d:6 + 2 * d]) for g, d in zip(gcum, ds)]
        eg = each(jnp.exp, gx)
        kb = each(jnp.multiply, kn, bx)
        vb = each(jnp.multiply, v, bx)
        kbg = each(jnp.multiply, kb, eg)
        grow = [jnp.sum(jnp.where(rowi == colj, g, 0.0), axis=0, keepdims=True) for g in gx]
        decay = [jnp.where(incl_dir[d], jnp.exp(jnp.where(incl_dir[d], g - gr, 0.0)), 0.0)
                 for g, gr, d in zip(gx, grow, ds)]
        kst = [_blk(_bf(k), lo) for k in kn]
        kq = [_dot_nt(jnp.concatenate([_bf(a), _bf(b)], axis=0), c) for a, b, c in zip(kb, qn, kst)]
        low = [jnp.where(strict_dir[d], x[0:CHUNK] * dc, 0.0) for x, dc, d in zip(kq, decay, ds)]
        intra = [jnp.where(incl_dir[d], x[CHUNK:2 * CHUNK] * dc, 0.0) for x, dc, d in zip(kq, decay, ds)]
        for r, x in zip(rows, zip(ds, qn, eg, intra)):
            d, q_, e_, i_ = x
            qg_ref[d, r, :] = _bf(q_ * e_)
            in_ref[d, r, :] = _bf(i_)
        lb = each(_bf, low)
        p = [_dot(a, _blk(a, lo)) for a in lb]
        nn = [-a for a in low]
        n_sq = int(math.log2(CHUNK)) - 1
        for r_ in range(n_sq):
            pb = each(_bf, p)
            pblk = [_blk(a, lo) for a in pb]
            if r_ < n_sq - 1:
                both = [_dot(jnp.concatenate([_bf(n_), b], axis=0), pk) for n_, b, pk in zip(nn, pb, pblk)]
                nn = [n_ + p_ + x[0:CHUNK] for n_, p_, x in zip(nn, p, both)]
                p = [x[CHUNK:2 * CHUNK] for x in both]
            else:
                nn = [n_ + p_ + _dot(_bf(n_), pk) for n_, p_, pk in zip(nn, p, pblk)]
        nb = each(_bf, nn)
        uw = [_dot(n_, jnp.concatenate([_blk(_bf(a), lo), _blk(_bf(b), lo)], axis=1))
              for n_, a, b in zip(nb, vb, kbg)]
        u = [a + x[:, 0:LANES] for a, x in zip(vb, uw)]
        w = [a + x[:, LANES:2 * LANES] for a, x in zip(kbg, uw)]
        g_last = [g[0:1, :] if d == 1 else g[CHUNK - 1:CHUNK, :] for g, d in zip(gx, ds)]
        kdec = [_bf(k * jnp.exp(gl - g)) for k, gl, g in zip(kn, g_last, gx)]
        wb = each(_bf, w)
        kr = [_dot_tn(a, jnp.concatenate([b, _bf(c)], axis=1)) for a, b, c in zip(kdec, wb, u)]
        for i, (c, d) in enumerate(chains):
            w_ref[d, rows[i], :] = wb[i]
            u_ref[d, rows[i], :] = u[i]
            k2_ref[d, srows[i], :] = _bf(jnp.where(blockmask, kr[i][:, 0:LANES], 0.0))
            r_ref[d, srows[i], :] = jnp.where(blockmask, kr[i][:, LANES:2 * LANES], 0.0)
            el_ref[d, pl.ds(pl.multiple_of(c * 8, 8), 8), :] = jnp.broadcast_to(jnp.exp(g_last[i]), (8, LANES))

    ncc = tc // CHUNK
    nct = tt // CHUNK
    ng = max(g for g in range(1, UNROLL_GDN + 1) if nct % g == 0)

    def group_chains(i):
        return [(i * ng + k, d) for k in range(ng) for d in range(2)]

    def body_a(i, carry):
        prepare(group_chains(i))
        return carry

    lax.fori_loop(0, nct // ng, body_a, 0)

    def advance(c, d, s):
        srows = pl.ds(pl.multiple_of(c * LANES, LANES), LANES)
        s16 = _bf(s)
        s_ref[d, srows, :] = s16
        e = el_ref[d, pl.ds(pl.multiple_of(c * 8, 8), 1), :]
        return s * e - _dot(k2_ref[d, srows, :], s16) + r_ref[d, srows, :]

    def body_b(i, carry):
        sf, sb = carry
        cb = jnp.where(i < ncc, ncc - 1 - i, nct - 1 - (i - ncc))
        return advance(i, 0, sf), advance(cb, 1, sb)

    zero = jnp.zeros((LANES, LANES), F32)
    lax.fori_loop(0, nct, body_b, (zero, zero))

    def emit(chains):
        rows = [pl.ds(pl.multiple_of(c * CHUNK, CHUNK), CHUNK) for c, _ in chains]
        s16 = [s_ref[d, pl.ds(pl.multiple_of(c * LANES, LANES), LANES), :] for c, d in chains]
        ws = [_dot(w_ref[d, r, :], s) for (_, d), r, s in zip(chains, rows, s16)]
        qs = [_dot(qg_ref[d, r, :], s) for (_, d), r, s in zip(chains, rows, s16)]
        v_new = [u_ref[d, r, :] - a for (_, d), r, a in zip(chains, rows, ws)]
        iv = [_dot(in_ref[d, r, :], _blk(_bf(vn), lo)) for (_, d), r, vn in zip(chains, rows, v_new)]
        for (_, d), r, a, b in zip(chains, rows, qs, iv):
            (ob_ref if d == 1 else of_ref)[r, :] = a + b

    def body_c(i, carry):
        emit(group_chains(i))
        return carry

    lax.fori_loop(0, nct // ng, body_c, 0)

    o = of_ref[...] + ob_ref[...]
    ss = _split_dot(o * o, seg)
    o = o * lax.rsqrt(ss * (1.0 / GDN_DK) + EPS) * nw_ref[...]
    o_ref[...] = _bf(o * _silu(z_ref[...]))


def _gdn(gqkv, gz, gba, conv_p, hp, nw, *, B, TT, tc):
    NT = gqkv.shape[0]
    t = TT - tc
    w3 = 3 * LANES
    nch = TT // CHUNK
    kern = functools.partial(_gdn_kernel, tc=tc, t=t)
    return pl.pallas_call(
        kern,
        grid=(B, GDN_PAIRS),
        in_specs=[pl.BlockSpec((TT, w3), lambda b, p: (b, p)),
                  pl.BlockSpec((TT, LANES), lambda b, p: (b, p)),
                  pl.BlockSpec((TT, LANES), lambda b, p: (b, p)),
                  pl.BlockSpec((None, 8, w3), lambda b, p: (p, 0, 0)),
                  pl.BlockSpec((None, 8, LANES), lambda b, p: (p, 0, 0)),
                  pl.BlockSpec((1, LANES), lambda b, p: (0, 0))],
        out_specs=pl.BlockSpec((TT, LANES), lambda b, p: (b, p)),
        out_shape=jax.ShapeDtypeStruct((NT, GDN_W), BF16),
        scratch_shapes=[pltpu.VMEM((3, t + 16, LANES), F32), pltpu.VMEM((3, TT, LANES), F32),
                        pltpu.VMEM((TT, 2 * LANES), F32), pltpu.VMEM((TT, LANES), F32),
                        pltpu.VMEM((TT, LANES), F32),
                        pltpu.VMEM((2, TT, LANES), BF16), pltpu.VMEM((2, TT, LANES), BF16),
                        pltpu.VMEM((2, TT, LANES), BF16), pltpu.VMEM((2, TT, LANES), F32),
                        pltpu.VMEM((2, nch * LANES, LANES), BF16), pltpu.VMEM((2, nch * LANES, LANES), F32),
                        pltpu.VMEM((2, nch * 8, LANES), F32), pltpu.VMEM((2, nch * LANES, LANES), BF16)],
        compiler_params=_cparams(("parallel", "parallel")),
        name="gdn",
    )(gqkv, gz, gba, conv_p, hp, nw)


def _att_tile(q_ref, k_ref, v_ref, o_ref, nk):
    lo = lax.broadcasted_iota(jnp.int32, (1, LANES), 1) < HALF
    k = k_ref[0:nk, :]
    v = v_ref[0:nk, :]

    def scores(h):
        j, half = h // 2, h % 2
        qj = q_ref[:, j * LANES:(j + 1) * LANES]
        zq = jnp.zeros_like(qj)
        return _dot_nt(jnp.where(lo, qj, zq) if half == 0 else jnp.where(lo, zq, qj), k)

    outs = []
    s = scores(0)
    for h in range(ATT_HEADS):
        s_next = scores(h + 1) if h + 1 < ATT_HEADS else None
        m = jnp.max(s, axis=-1, keepdims=True)
        p = jnp.exp(s - m)
        l = jnp.sum(p, axis=-1, keepdims=True)
        outs.append(_dot(_bf(p), v) / l)
        s = s_next
    for j in range(ATT_GROUP):
        o_ref[:, j * LANES:(j + 1) * LANES] = _bf(jnp.where(lo, outs[2 * j], outs[2 * j + 1]))


def _att_kernel(q_ref, k_ref, v_ref, o_ref, *, n_ctx_tiles, tc, tt):
    is_ctx = pl.program_id(1) < n_ctx_tiles

    @pl.when(is_ctx)
    def _():
        _att_tile(q_ref, k_ref, v_ref, o_ref, tc)

    @pl.when(jnp.logical_not(is_ctx))
    def _():
        _att_tile(q_ref, k_ref, v_ref, o_ref, tt)


def _attention(aq, ak, av, *, B, TT, tc):
    NT = aq.shape[0]
    tq = 256
    nq = TT // tq
    kv_spec = pl.BlockSpec((TT, LANES), lambda b, i: (b, 0))
    q_spec = pl.BlockSpec((tq, ATT_W), lambda b, i: (b * nq + i, 0))
    return pl.pallas_call(
        functools.partial(_att_kernel, n_ctx_tiles=tc // tq, tc=tc, tt=TT),
        grid=(B, nq),
        in_specs=[q_spec, kv_spec, kv_spec],
        out_specs=q_spec,
        out_shape=jax.ShapeDtypeStruct((NT, ATT_W), BF16),
        compiler_params=_cparams(("parallel", "parallel")),
        name="attention",
    )(aq, ak, av)


S5_BB = 2
GPV = LANES // S5_GH


def _s5_kernel(u0_ref, u1_ref, tg_ref, bp_ref, cp_ref, la_ref, lb_ref, d_ref, gw_ref, gb_ref, o_ref,
               uf_ref, sloc_ref, sin_ref, sinb_ref, y0_ref, y1_ref, *, bb, ncc, nct):
    lc = S5_CHUNK
    nr = bb * nct
    w2 = 2 * S5_P
    blk8 = lax.broadcasted_iota(jnp.int32, (1, LANES), 1) // S5_GH
    u_refs = (u0_ref, u1_ref)
    y_refs = (y0_ref, y1_ref)

    def block_transpose(xs):
        xs = list(xs)
        for s in (4, 2, 1):
            keep = (blk8 & s) == 0
            for a in range(GPV):
                if a & s == 0:
                    lo_, hi_ = xs[a], xs[a + s]
                    xs[a] = jnp.where(keep, lo_, pltpu.roll(hi_, s * S5_GH, 1))
                    xs[a + s] = jnp.where(keep, pltpu.roll(lo_, LANES - s * S5_GH, 1), hi_)
        return xs

    rbk = 32
    for a in range(S5_GROUPS // GPV):
        for v in range(lc // GPV):
            for r0 in range(0, nr, rbk):
                xs = [u_refs[a][pl.ds((v * GPV + k) + lc * r0, rbk, stride=lc), :] for k in range(GPV)]
                for k, y_ in enumerate(block_transpose(xs)):
                    uf_ref[a * GPV + k, r0:r0 + rbk, v * LANES:(v + 1) * LANES] = y_
    for g in range(S5_GROUPS):
        sl = _dot(_bf(uf_ref[g]), bp_ref[g])
        sloc_ref[2 * g] = sl[:, 0:w2]
        sloc_ref[2 * g + 1] = sl[:, w2:2 * w2]

    lo = lax.broadcasted_iota(jnp.int32, (1, LANES), 1) < S5_P

    def body(i, carry):
        cb = jnp.where(i < ncc, ncc - 1 - i, nct - 1 - (i - ncc))
        rows_f = pl.ds(i, bb, stride=nct)
        rows_b = pl.ds(cb, bb, stride=nct)
        out = []
        for g in range(S5_GROUPS):
            re, im = carry[2 * g], carry[2 * g + 1]
            sin_ref[2 * g, rows_f, :] = re
            sin_ref[2 * g + 1, rows_f, :] = im
            sinb_ref[2 * g, rows_b, :] = re
            sinb_ref[2 * g + 1, rows_b, :] = im
            b_re = jnp.where(lo, sloc_ref[2 * g, rows_f, :], sloc_ref[2 * g, rows_b, :])
            b_im = jnp.where(lo, sloc_ref[2 * g + 1, rows_f, :], sloc_ref[2 * g + 1, rows_b, :])
            lr, li = la_ref[g], lb_ref[g]
            out.append(re * lr - im * li + b_re)
            out.append(im * lr + re * li + b_im)
        return tuple(out)

    zero = jnp.zeros((bb, w2), F32)
    lax.fori_loop(0, nct, body, (zero,) * (2 * S5_GROUPS))

    for g in range(S5_GROUPS):
        uf_ref[g] = (_dot(_bf(uf_ref[g]), tg_ref[g])
                     + _dot(_bf(sin_ref[2 * g]), cp_ref[g, 0])
                     + _dot(_bf(sin_ref[2 * g + 1]), cp_ref[g, 1])
                     + _dot(_bf(sinb_ref[2 * g]), cp_ref[g, 2])
                     + _dot(_bf(sinb_ref[2 * g + 1]), cp_ref[g, 3]))

    for a in range(S5_GROUPS // GPV):
        for v in range(lc // GPV):
            for r0 in range(0, nr, rbk):
                xs = [uf_ref[a * GPV + k, r0:r0 + rbk, v * LANES:(v + 1) * LANES] for k in range(GPV)]
                for k, y_ in enumerate(block_transpose(xs)):
                    y_refs[a][pl.ds((v * GPV + k) + lc * r0, rbk, stride=lc), :] = y_

    y = jnp.concatenate([y0_ref[...], y1_ref[...]], axis=1)
    u = jnp.concatenate([u0_ref[...], u1_ref[...]], axis=1)
    zz = _gelu_tanh(y + d_ref[...] * u)
    o_ref[...] = _bf(zz * _sigmoid(_dot(_bf(zz), gw_ref[...]) + gb_ref[...]))


def _s5(su, tg, bp, cp, la, lb, d, gw, gb, *, B, TT, tc):
    NT, W = su.shape
    bb = S5_BB
    nct = TT // S5_CHUNK
    nr = bb * nct
    kern = functools.partial(_s5_kernel, bb=bb, ncc=tc // S5_CHUNK, nct=nct)
    full = lambda a: pl.BlockSpec(a.shape, lambda i: (0,) * a.ndim)
    return pl.pallas_call(
        kern,
        grid=(B // bb,),
        in_specs=[pl.BlockSpec((bb * TT, LANES), lambda i: (i, 0)),
                  pl.BlockSpec((bb * TT, LANES), lambda i: (i, 1)),
                  full(tg), full(bp), full(cp), full(la), full(lb), full(d), full(gw), full(gb)],
        out_specs=pl.BlockSpec((bb * TT, W), lambda i: (i, 0)),
        out_shape=jax.ShapeDtypeStruct((NT, W), BF16),
        scratch_shapes=[pltpu.VMEM((S5_GROUPS, nr, W), F32), pltpu.VMEM((2 * S5_GROUPS, nr, LANES), F32),
                        pltpu.VMEM((2 * S5_GROUPS, nr, LANES), F32), pltpu.VMEM((2 * S5_GROUPS, nr, LANES), F32),
                        pltpu.VMEM((bb * TT, LANES), F32), pltpu.VMEM((bb * TT, LANES), F32)],
        compiler_params=_cparams(("parallel",)),
        name="s5",
    )(su, su, tg, bp, cp, la, lb, d, gw, gb)


def _s5_matrices(lam_re, lam_im, log_dt, b_re, b_im, c_re, c_im):
    Lc = S5_CHUNK
    lam = lax.complex(lam_re.astype(F32), lam_im.astype(F32))
    dt = jnp.exp(log_dt.astype(F32))[..., None]
    lam_dt = lam * dt
    lam_bar = jnp.exp(lam_dt)
    b_bar = ((lam_bar - 1.0) / lam)[..., None] * lax.complex(b_re.astype(F32), b_im.astype(F32))
    cc = lax.complex(c_re.astype(F32), c_im.astype(F32))
    ks = jnp.arange(Lc + 1, dtype=F32)
    pw = jnp.exp(lam_dt[..., None, :] * ks[:, None])
    kern = jnp.real(jnp.einsum('dghp,dgkp,dgpi->dgkhi', cc, pw[:, :, :Lc], b_bar))
    i_idx = jnp.arange(Lc)[:, None]
    j_idx = jnp.arange(Lc)[None, :]
    lag = i_idx - j_idx
    kf = kern[0][:, jnp.clip(lag, 0, Lc - 1)]
    kb = kern[1][:, jnp.clip(-lag, 0, Lc - 1)]
    tmat = (jnp.where((lag >= 0)[None, :, :, None, None], kf, 0.0)
            + jnp.where((lag <= 0)[None, :, :, None, None], kb, 0.0))
    tg = tmat.transpose(0, 2, 4, 1, 3).reshape(S5_GROUPS, Lc * S5_GH, Lc * S5_GH)
    pf = pw[0][:, Lc - 1 - jnp.arange(Lc)]
    pb = pw[1][:, jnp.arange(Lc)]
    sf = jnp.einsum('gjp,gph->gjhp', pf, b_bar[0])
    sb = jnp.einsum('gjp,gph->gjhp', pb, b_bar[1])
    bp = jnp.concatenate([jnp.real(sf), jnp.real(sb), jnp.imag(sf), jnp.imag(sb)], axis=-1)
    bp = bp.reshape(S5_GROUPS, Lc * S5_GH, 4 * S5_P)
    qf = jnp.einsum('ghp,gip->gpih', cc[0], pw[0][:, 1 + jnp.arange(Lc)])
    qb = jnp.einsum('ghp,gip->gpih', cc[1], pw[1][:, Lc - jnp.arange(Lc)])
    zq = jnp.zeros(qf.shape, F32)
    cp = jnp.stack([jnp.concatenate([jnp.real(qf), zq], axis=1), jnp.concatenate([-jnp.imag(qf), zq], axis=1),
                    jnp.concatenate([zq, jnp.real(qb)], axis=1), jnp.concatenate([zq, -jnp.imag(qb)], axis=1)],
                   axis=1)
    cp = cp.reshape(S5_GROUPS, 4, 2 * S5_P, Lc * S5_GH)
    lc = pw[:, :, Lc]
    la = jnp.concatenate([jnp.real(lc[0]), jnp.real(lc[1])], axis=-1)
    lb = jnp.concatenate([jnp.imag(lc[0]), jnp.imag(lc[1])], axis=-1)
    return (tg.astype(BF16), bp.astype(BF16), cp.astype(BF16),
            la.reshape(S5_GROUPS, 1, 2 * S5_P), lb.reshape(S5_GROUPS, 1, 2 * S5_P))


def _gelu_tanh(x):
    return 0.5 * x * (1.0 + jnp.tanh(math.sqrt(2.0 / math.pi) * (x + 0.044715 * (x * x * x))))


def _kout_kernel(x_ref, mod_c_ref, mod_b_ref, go_ref, ao_ref, so_ref,
                 wg_ref, wa_ref, ws_ref, g_ref, b_ref, o_ref, *, tm, tc, tpb):
    rm = functools.partial(_row_mod, mod_c_ref, mod_b_ref, _ctx_rows(tm, tc, tpb))
    y = _dot(go_ref[...], wg_ref[...]) + _dot(ao_ref[...], wa_ref[...]) + _dot(so_ref[...], ws_ref[...])
    z = ALPHA * x_ref[...] + rm(5) * y
    o_ref[...] = _layer_norm(z, g_ref[...], b_ref[...])


def _kout(xa, mod, go, ao, so, wg, wa, ws, g, b, *, B, TT, tc):
    NT, D = xa.shape
    tpb = 4
    tm = TT // tpb
    kern = functools.partial(_kout_kernel, tm=tm, tc=tc, tpb=tpb)
    row = lambda w: pl.BlockSpec((tm, w), lambda i: (i, 0))
    full = lambda a: pl.BlockSpec(a.shape, lambda i: (0,) * a.ndim)
    return pl.pallas_call(
        kern,
        grid=(NT // tm,),
        in_specs=[row(D),
                  pl.BlockSpec((None, N_MOD, D), lambda i: (B, 0, 0)),
                  pl.BlockSpec((None, N_MOD, D), lambda i: (i // tpb, 0, 0)),
                  row(GDN_W), row(ATT_W), row(S5_W),
                  full(wg), full(wa), full(ws), full(g), full(b)],
        out_specs=row(D),
        out_shape=jax.ShapeDtypeStruct((NT, D), F32),
        compiler_params=_cparams(("parallel",)),
        name="mixer_out",
    )(xa, mod, mod, go, ao, so, wg, wa, ws, g, b)


def _perm_in_cols():
    off_z = GDN_QKV
    off_b = off_z + GDN_W
    off_a = off_b + 2 * GDN_HEADS
    off_q = off_a + 2 * GDN_HEADS
    off_k = off_q + ATT_W
    off_v = off_k + ATT_KV_HEADS * ATT_DH
    off_u = off_v + ATT_KV_HEADS * ATT_DH
    cols = []
    for p in range(GDN_PAIRS):
        for part in range(3):
            base = part * GDN_W + 2 * p * GDN_DK
            cols += list(range(base, base + 2 * GDN_DK))
    cols += list(range(off_z, off_z + GDN_W))
    for p in range(GDN_PAIRS):
        blk = []
        for off in (off_b, off_a):
            for d in range(2):
                blk += [off + d * GDN_HEADS + 2 * p, off + d * GDN_HEADS + 2 * p + 1]
        cols += blk + [-1] * (LANES - len(blk))
    for j in range(ATT_GROUP):
        for h in (j, j + ATT_GROUP):
            cols += list(range(off_q + h * ATT_DH, off_q + (h + 1) * ATT_DH))
    cols += list(range(off_k, off_k + 2 * ATT_DH))
    cols += list(range(off_v, off_v + 2 * ATT_DH))
    cols += list(range(off_u, off_u + S5_W))
    return np.asarray(cols, dtype=np.int32)


def _gdn_pair_cols():
    cols = []
    for p in range(GDN_PAIRS):
        for part in range(3):
            base = part * GDN_W + 2 * p * GDN_DK
            cols += list(range(base, base + 2 * GDN_DK))
    return np.asarray(cols, dtype=np.int32)


def _att_out_rows():
    rows = []
    for j in range(ATT_GROUP):
        for h in (j, j + ATT_GROUP):
            rows += list(range(h * ATT_DH, (h + 1) * ATT_DH))
    return np.asarray(rows, dtype=np.int32)


def _take(w, idx, axis):
    idx = [int(i) for i in idx]
    parts = []
    i = 0
    while i < len(idx):
        j = i + 1
        if idx[i] < 0:
            while j < len(idx) and idx[j] < 0:
                j += 1
            shape = list(w.shape)
            shape[axis] = j - i
            parts.append(jnp.zeros(shape, w.dtype))
        else:
            while j < len(idx) and idx[j] == idx[j - 1] + 1:
                j += 1
            parts.append(lax.slice_in_dim(w, idx[i], idx[j - 1] + 1, axis=axis))
        i = j
    return jnp.concatenate(parts, axis=axis)


def _rope_tables(tc, t):
    rows = t // GRID_W
    row = jnp.repeat(jnp.arange(rows), GRID_W)
    col = jnp.tile(jnp.arange(GRID_W), rows)
    inv_freq = ROPE_THETA ** (-jnp.arange(ROPE_PAIRS, dtype=F32) / ROPE_PAIRS)
    ang = jnp.stack([row, col], axis=-1).astype(F32)[..., None] * inv_freq
    cos, sin = jnp.cos(ang), jnp.sin(ang)
    zero = jnp.zeros_like(sin)
    cos_h = jnp.stack([cos, cos], axis=2).reshape(t, ATT_DH)
    sa_h = jnp.stack([-sin, zero], axis=2).reshape(t, ATT_DH)
    sb_h = jnp.stack([zero, sin], axis=2).reshape(t, ATT_DH)
    def full(tab, fill):
        tab = jnp.concatenate([jnp.full((tc, ATT_DH), fill, F32), tab], axis=0)
        return jnp.concatenate([tab, tab], axis=1)
    return full(cos_h, 1.0), full(sa_h, 0.0), full(sb_h, 0.0)


def kernel(x, c, ctx, c_ctx, w_ada, b_ada, ln_g, ln_b, ffn_w1, ffn_w3, ffn_w2, w_in, w_out, gdn_conv_w,
           gdn_a_log, gdn_dt_bias, gdn_norm_w, q_norm_w, k_norm_w, s5_lam_re, s5_lam_im, s5_log_dt,
           s5_b_re, s5_b_im, s5_c_re, s5_c_im, s5_d, glu_w, glu_b):
    B, T, D = x.shape
    tc = ctx.shape[1]
    TT = tc + T
    NT = B * TT
    L = w_ada.shape[0]
    kw = dict(B=B, TT=TT, tc=tc)

    R = ((B + 1 + 7) // 8) * 8
    c_all = jnp.concatenate([c, c_ctx[None, :], jnp.zeros((R - B - 1, D), F32)], axis=0)
    mod_all = _modulation(c_all, w_ada, b_ada).reshape(L, R, N_MOD, D)

    in_cols = _perm_in_cols()
    pair_cols = _gdn_pair_cols()
    att_rows = _att_out_rows()
    cos, sa, sb = _rope_tables(tc, T)

    def head_params(v):
        v = v.astype(F32).reshape(2, GDN_PAIRS, 2).transpose(1, 0, 2).reshape(GDN_PAIRS, 4)
        return jnp.pad(v, ((0, 0), (4, LANES - 8)))

    xa = jnp.concatenate([ctx, x], axis=1).reshape(NT, D)
    for l in range(L):
        mod = mod_all[l]
        g_l, b_l = ln_g[l], ln_b[l]
        xa = _ffn(xa, mod, _ffn_pack(ffn_w1[l, 0], ffn_w3[l, 0], ffn_w2[l, 0]), g_l[0:1], b_l[0:1], sub=0, **kw)

        w_in_p = _take(_bf(w_in[l]), in_cols, 1)
        qw = jnp.tile(q_norm_w[l], 2)[None, :]
        kwt = jnp.tile(k_norm_w[l], 2)[None, :]
        gqkv, gz, gba, aq, ak, av, su = _kin(xa, mod, w_in_p, qw, kwt, cos, sa, sb, **kw)

        conv_p = jnp.pad(_take(gdn_conv_w[l], pair_cols, 1), ((0, 8 - CONV_K), (0, 0)))
        conv_p = conv_p.reshape(8, GDN_PAIRS, 3 * LANES).transpose(1, 0, 2)
        hp = jnp.stack([head_params(-jnp.exp(gdn_a_log[l].astype(F32))), head_params(gdn_dt_bias[l])], axis=1)
        hp = jnp.pad(hp, ((0, 0), (0, 6), (0, 0)))
        nw = jnp.tile(gdn_norm_w[l], 2)[None, :]
        go = _gdn(gqkv, gz, gba, conv_p, hp, nw, **kw)

        ao = _attention(aq, ak, av, **kw)

        tg, bp, cp, la, lb = _s5_matrices(s5_lam_re[l], s5_lam_im[l], s5_log_dt[l], s5_b_re[l], s5_b_im[l],
                                          s5_c_re[l], s5_c_im[l])
        so = _s5(su, tg, bp, cp, la, lb, s5_d[l][None, :], _bf(glu_w[l]), glu_b[l][None, :], **kw)

        wo = _bf(w_out[l])
        wg = wo[0:GDN_W]
        wa = _take(wo[GDN_W:GDN_W + ATT_W], att_rows, 0)
        ws = wo[GDN_W + ATT_W:]
        xa = _kout(xa, mod, go, ao, so, wg, wa, ws, g_l[1:2], b_l[1:2], **kw)
        xa = _ffn(xa, mod, _ffn_pack(ffn_w1[l, 1], ffn_w3[l, 1], ffn_w2[l, 1]), g_l[2:3], b_l[2:3], sub=2, **kw)
    return xa.reshape(B, TT, D)[:, tc:, :]
```

```python
import functools
import math

import numpy as np
import jax
import jax.numpy as jnp
from jax import lax
from jax.experimental import pallas as pl
from jax.experimental.pallas import tpu as pltpu

F32 = jnp.float32
BF16 = jnp.bfloat16

D_MODEL = 1024
GRID_W = 64
GDN_HEADS = 6
GDN_DK = 64
GDN_PAIRS = GDN_HEADS // 2
GDN_QKV = 3 * GDN_HEADS * GDN_DK
GDN_W = GDN_HEADS * GDN_DK
CONV_K = 5
CHUNK = 64
UNROLL_GDN = 9
ATT_HEADS = 6
ATT_KV_HEADS = 2
ATT_GROUP = ATT_HEADS // ATT_KV_HEADS
ATT_DH = 64
ATT_W = ATT_HEADS * ATT_DH
ROPE_THETA = 10000.0
ROPE_PAIRS = ATT_DH // 4
S5_GROUPS = 16
S5_GH = 16
S5_P = 64
S5_W = S5_GROUPS * S5_GH
S5_CHUNK = 16
N_MOD = 9
DEPTH = 4
ALPHA = (2.0 * DEPTH) ** 0.25
EPS = 1e-6
LANES = 128
HALF = LANES // 2

C_QKV = 0
C_Z = C_QKV + GDN_QKV
C_BA = C_Z + GDN_W
C_AQ = C_BA + GDN_PAIRS * LANES
C_AK = C_AQ + ATT_W
C_AV = C_AK + LANES
C_SU = C_AV + LANES
C_END = C_SU + S5_W

VMEM_LIMIT = 56 * 1024 * 1024


def _cparams(sem):
    return pltpu.CompilerParams(dimension_semantics=sem, vmem_limit_bytes=VMEM_LIMIT)


def _bf(x):
    return x.astype(BF16)


def _dot(a, b):
    return jnp.dot(a, b, preferred_element_type=F32)


def _dot_nt(a, b):
    return lax.dot_general(a, b, (((1,), (1,)), ((), ())), preferred_element_type=F32)


def _dot_tn(a, b):
    return lax.dot_general(a, b, (((0,), (0,)), ((), ())), preferred_element_type=F32)


def _split_dot(x, m):
    hi = _bf(x)
    lo = _bf(x - hi.astype(F32))
    return _dot(hi, m) + _dot(lo, m)


def _sigmoid(x):
    return 0.5 + 0.5 * jnp.tanh(0.5 * x)


def _silu(x):
    h = 0.5 * x
    return h + h * jnp.tanh(h)


def _layer_norm(z, g, b):
    mu = jnp.mean(z, axis=-1, keepdims=True)
    zc = z - mu
    var = jnp.mean(zc * zc, axis=-1, keepdims=True)
    return zc * lax.rsqrt(var + EPS) * g + b


def _ctx_rows(tm, tc, tiles_per_batch):
    row0 = (pl.program_id(0) % tiles_per_batch) * tm
    return row0 + lax.broadcasted_iota(jnp.int32, (tm, 1), 0) < tc


def _row_mod(mod_c_ref, mod_b_ref, is_ctx, idx):
    return jnp.where(is_ctx, mod_c_ref[idx:idx + 1, :], mod_b_ref[idx:idx + 1, :])


def _mod_kernel(c_ref, w_ref, b_ref, o_ref):
    sc = _silu(c_ref[...])
    o_ref[...] = jnp.dot(sc, w_ref[...], preferred_element_type=F32,
                         precision=lax.Precision.HIGHEST) + b_ref[...]


def _modulation(c_all, w_ada, b_ada):
    L, D, NM = w_ada.shape
    R = c_all.shape[0]
    tn = 1152
    return pl.pallas_call(
        _mod_kernel,
        grid=(L, NM // tn),
        in_specs=[pl.BlockSpec((R, D), lambda l, j: (0, 0)),
                  pl.BlockSpec((None, D, tn), lambda l, j: (l, 0, j)),
                  pl.BlockSpec((None, 1, tn), lambda l, j: (l, 0, j))],
        out_specs=pl.BlockSpec((None, R, tn), lambda l, j: (l, 0, j)),
        out_shape=jax.ShapeDtypeStruct((L, R, NM), F32),
        compiler_params=_cparams(("parallel", "parallel")),
        name="adaln_mod",
    )(c_all, w_ada, b_ada.reshape(L, 1, NM))


def _ffn_kernel(x_ref, mod_c_ref, mod_b_ref, modp_b_ref, w13_ref, w2_ref, g_ref, b_ref, o_ref,
                hm_ref, acc_ref, xk_ref, *, sub, tm, tc, tpb, nf, nt):
    i = pl.program_id(0)
    f = pl.program_id(1)
    slot = i % 2
    row = lax.broadcasted_iota(jnp.int32, (tm, 1), 0)
    rm = functools.partial(_row_mod, mod_c_ref, mod_b_ref, (i % tpb) * tm + row < tc)
    nsteps = -(-nf // FFN_GROUP)
    qr = tm // nsteps

    def modulate():
        x = x_ref[...]
        xk_ref[slot] = x
        hm_ref[...] = _bf(x * (1.0 + rm(3 * sub + 1)) + rm(3 * sub))

    def accumulate(step):
        hm = hm_ref[...]
        for c in range(min(FFN_GROUP, nf - step * FFN_GROUP)):
            h = _dot(hm, w13_ref[c])
            part = _dot(_bf(_silu(h[:, 0:FFN_FC]) * h[:, FFN_FC:2 * FFN_FC]), w2_ref[c])
            if step == 0 and c == 0:
                acc_ref[slot] = part
            else:
                acc_ref[slot] += part

    def finish_previous():
        rg = 32
        for k in range(qr // rg):
            r0 = pl.multiple_of(f * qr + k * rg, 8)
            rows = pl.ds(r0, rg)
            prev_row = ((i + tpb - 1) % tpb) * tm + r0 + lax.broadcasted_iota(jnp.int32, (rg, 1), 0)
            gate = _row_mod(mod_c_ref, modp_b_ref, prev_row < tc, 3 * sub + 2)
            z = ALPHA * xk_ref[1 - slot, rows, :] + 0.5 * gate * acc_ref[1 - slot, rows, :]
            o_ref[rows, :] = _layer_norm(z, g_ref[...], b_ref[...])

    do_mm = i < nt
    do_fin = i > 0
    for step in range(nsteps):
        here = f == step

        @pl.when(here & do_mm & do_fin)
        def _(step=step):
            finish_previous()
            if step == 0:
                modulate()
            accumulate(step)

        @pl.when(here & do_mm & jnp.logical_not(do_fin))
        def _(step=step):
            if step == 0:
                modulate()
            accumulate(step)

    @pl.when(jnp.logical_not(do_mm) & do_fin)
    def _():
        finish_previous()


FFN_FC = 256
FFN_GROUP = 6
FFN_TPB = 3


def _ffn_pack(w1, w3, w2):
    D, F = w1.shape
    nf = F // FFN_FC
    pad = -nf % FFN_GROUP
    w13 = jnp.concatenate([_bf(w1).reshape(D, nf, FFN_FC), _bf(w3).reshape(D, nf, FFN_FC)], axis=2)
    w13 = jnp.pad(w13.transpose(1, 0, 2), ((0, pad), (0, 0), (0, 0)))
    w2p = jnp.pad(_bf(w2).reshape(nf, FFN_FC, D), ((0, pad), (0, 0), (0, 0)))
    return w13, w2p, nf


def _ffn(xa, mod, packed, g, b, *, sub, B, TT, tc):
    NT, D = xa.shape
    w13, w2, nf = packed
    fc = FFN_FC
    tpb = FFN_TPB
    tm = TT // tpb
    nt = NT // tm
    nsteps = w13.shape[0] // FFN_GROUP
    kern = functools.partial(_ffn_kernel, sub=sub, tm=tm, tc=tc, tpb=tpb, nf=nf, nt=nt)
    cur = lambda i: jnp.minimum(i, nt - 1)
    prev = lambda i: jnp.maximum(i - 1, 0)
    return pl.pallas_call(
        kern,
        grid=(nt + 1, nsteps),
        in_specs=[pl.BlockSpec((tm, D), lambda i, f: (cur(i + jnp.minimum(f, 1)), 0)),
                  pl.BlockSpec((None, N_MOD, D), lambda i, f: (B, 0, 0)),
                  pl.BlockSpec((None, N_MOD, D), lambda i, f: (cur(i) // tpb, 0, 0)),
                  pl.BlockSpec((None, N_MOD, D), lambda i, f: (prev(i) // tpb, 0, 0)),
                  pl.BlockSpec((FFN_GROUP, D, 2 * fc), lambda i, f: (f, 0, 0)),
                  pl.BlockSpec((FFN_GROUP, fc, D), lambda i, f: (f, 0, 0)),
                  pl.BlockSpec((1, D), lambda i, f: (0, 0)),
                  pl.BlockSpec((1, D), lambda i, f: (0, 0))],
        out_specs=pl.BlockSpec((tm, D), lambda i, f: (prev(i), 0)),
        out_shape=jax.ShapeDtypeStruct((NT, D), F32),
        scratch_shapes=[pltpu.VMEM((tm, D), BF16), pltpu.VMEM((2, tm, D), F32), pltpu.VMEM((2, tm, D), F32)],
        compiler_params=_cparams(("arbitrary", "arbitrary")),
        name=f"ffn{sub}",
    )(xa, mod, mod, mod, w13, w2, g, b)


def _seg_ones():
    r = lax.broadcasted_iota(jnp.int32, (LANES, LANES), 0) // HALF
    c = lax.broadcasted_iota(jnp.int32, (LANES, LANES), 1) // HALF
    return jnp.where(r == c, 1.0, 0.0).astype(BF16)


def _kin_kernel(x_ref, mod_c_ref, mod_b_ref, w_ref, qw_ref, kw_ref, cos_ref, sa_ref, sb_ref,
                gqkv_ref, gz_ref, gba_ref, aq_ref, ak_ref, av_ref, su_ref, *, tm, tc, tpb):
    rm = functools.partial(_row_mod, mod_c_ref, mod_b_ref, _ctx_rows(tm, tc, tpb))
    hm = _bf(x_ref[...] * (1.0 + rm(4)) + rm(3))
    r = _dot(hm, w_ref[...])
    gqkv_ref[...] = r[:, C_QKV:C_Z]
    gz_ref[...] = r[:, C_Z:C_BA]
    gba_ref[...] = r[:, C_BA:C_AQ]
    av_ref[...] = _bf(r[:, C_AV:C_SU])
    su_ref[...] = r[:, C_SU:C_END]

    seg = _seg_ones()
    cos, sa, sb = cos_ref[...], sa_ref[...], sb_ref[...]

    def norm_rope(blk, w_row):
        ss = _split_dot(blk * blk, seg)
        y = blk * lax.rsqrt(ss * (1.0 / ATT_DH) + EPS) * w_row
        return y * cos + pltpu.roll(y, LANES - ROPE_PAIRS, 1) * sa + pltpu.roll(y, ROPE_PAIRS, 1) * sb

    qw = qw_ref[...] * (ATT_DH ** -0.5)
    for j in range(ATT_W // LANES):
        aq_ref[:, j * LANES:(j + 1) * LANES] = _bf(
            norm_rope(r[:, C_AQ + j * LANES:C_AQ + (j + 1) * LANES], qw))
    ak_ref[...] = _bf(norm_rope(r[:, C_AK:C_AV], kw_ref[...]))


def _kin(xa, mod, w_in_p, qw, kw, cos, sa, sb, *, B, TT, tc):
    NT, D = xa.shape
    tpb = 4
    tm = TT // tpb
    kern = functools.partial(_kin_kernel, tm=tm, tc=tc, tpb=tpb)
    widths = [(GDN_QKV, F32), (GDN_W, F32), (GDN_PAIRS * LANES, F32), (ATT_W, BF16),
              (LANES, BF16), (LANES, BF16), (S5_W, F32)]
    return pl.pallas_call(
        kern,
        grid=(NT // tm,),
        in_specs=[pl.BlockSpec((tm, D), lambda i: (i, 0)),
                  pl.BlockSpec((None, N_MOD, D), lambda i: (B, 0, 0)),
                  pl.BlockSpec((None, N_MOD, D), lambda i: (i // tpb, 0, 0)),
                  pl.BlockSpec((D, C_END), lambda i: (0, 0)),
                  pl.BlockSpec((1, LANES), lambda i: (0, 0)),
                  pl.BlockSpec((1, LANES), lambda i: (0, 0)),
                  pl.BlockSpec((tm, LANES), lambda i: (i % tpb, 0)),
                  pl.BlockSpec((tm, LANES), lambda i: (i % tpb, 0)),
                  pl.BlockSpec((tm, LANES), lambda i: (i % tpb, 0))],
        out_specs=[pl.BlockSpec((tm, w), lambda i: (i, 0)) for w, _ in widths],
        out_shape=[jax.ShapeDtypeStruct((NT, w), dt) for w, dt in widths],
        compiler_params=_cparams(("parallel",)),
        name="mixer_in",
    )(xa, mod, mod, w_in_p, qw, kw, cos, sa, sb)


def _blk(x, lo):
    z = jnp.zeros_like(x)
    return jnp.concatenate([jnp.where(lo, x, z), jnp.where(lo, z, x)], axis=0)


def _gdn_kernel(qkv_in_ref, z_ref, ba_ref, cw_ref, hp_ref, nw_ref, o_ref,
                xp_ref, qkv_ref, bg_ref, of_ref, ob_ref,
                qg_ref, in_ref, w_ref, u_ref, k2_ref, r_ref, el_ref, s_ref, *, tc, t):
    tt = tc + t
    seg = _seg_ones()

    pad = 8
    for off, n in ((0, tc), (tc, t)):
        for part in range(3):
            xp_ref[part, 0:pad, :] = jnp.zeros((pad, LANES), F32)
            xp_ref[part, pad:pad + n, :] = qkv_in_ref[off:off + n, part * LANES:(part + 1) * LANES]
            xp_ref[part, pad + n:2 * pad + n, :] = jnp.zeros((pad, LANES), F32)
        nb = n // 8
        rb = min(nb, 128)
        for r in range(8):
            for i0 in range(0, nb, rb):
                for part in range(3):
                    lanes = slice(part * LANES, (part + 1) * LANES)
                    base = pad - CONV_K // 2 + r + 8 * i0
                    acc = cw_ref[0:1, lanes] * xp_ref[part, pl.ds(base, rb, stride=8), :]
                    for j in range(1, CONV_K):
                        acc = acc + cw_ref[j:j + 1, lanes] * xp_ref[part, pl.ds(base + j, rb, stride=8), :]
                    y = _silu(acc)
                    if part == 0:
                        y = y * lax.rsqrt(_split_dot(y * y, seg) + EPS) * (GDN_DK ** -0.5)
                    elif part == 1:
                        y = y * lax.rsqrt(_split_dot(y * y, seg) + EPS)
                    qkv_ref[part, pl.ds(off + r + 8 * i0, rb, stride=8), :] = y

    ba = ba_ref[...]
    bg_ref[:, 0:LANES] = _sigmoid(ba)
    sp_in = ba + hp_ref[1:2, :]
    softplus = jnp.maximum(sp_in, 0.0) + jnp.log1p(jnp.exp(-jnp.abs(sp_in)))
    bg_ref[:, LANES:2 * LANES] = hp_ref[0:1, :] * softplus

    rowi = lax.broadcasted_iota(jnp.int32, (CHUNK, LANES), 0)
    colj = lax.broadcasted_iota(jnp.int32, (CHUNK, LANES), 1) % HALF
    lo = lax.broadcasted_iota(jnp.int32, (CHUNK, LANES), 1) < HALF
    br = lax.broadcasted_iota(jnp.int32, (LANES, LANES), 0) // HALF
    bc = lax.broadcasted_iota(jnp.int32, (LANES, LANES), 1) // HALF
    blockmask = br == bc

    sub8 = rowi % 8
    nv = CHUNK // 8

    def chunk_cumsum(x, backward):
        for s in (1, 2, 4):
            if backward:
                x = x + jnp.where(sub8 < 8 - s, pltpu.roll(x, CHUNK - s, 0), 0.0)
            else:
                x = x + jnp.where(sub8 >= s, pltpu.roll(x, s, 0), 0.0)
        parts = [x[8 * j:8 * j + 8, :] for j in range(nv)]
        order = range(nv - 1, -1, -1) if backward else range(nv)
        edge = 0 if backward else 7
        run = None
        for j in order:
            if run is not None:
                parts[j] = parts[j] + run
            run = parts[j][edge:edge + 1, :]
        return jnp.concatenate(parts, axis=0)

    incl_dir = (colj <= rowi, colj >= rowi)
    strict_dir = (colj < rowi, colj > rowi)

    def each(f, *lists):
        return [f(*a) for a in zip(*lists)]

    def prepare(chains):
        ds = [d for _, d in chains]
        rows = [pl.ds(pl.multiple_of(c * CHUNK, CHUNK), CHUNK) for c, _ in chains]
        srows = [pl.ds(pl.multiple_of(c * LANES, LANES), LANES) for c, _ in chains]
        qn = [qkv_ref[0, r, :] for r in rows]
        kn = [qkv_ref[1, r, :] for r in rows]
        v = [qkv_ref[2, r, :] for r in rows]
        bx = [jnp.where(lo, bg_ref[r, 2 * d:2 * d + 1], bg_ref[r, 2 * d + 1:2 * d + 2]) for r, d in zip(rows, ds)]
        gcum = [chunk_cumsum(bg_ref[r, LANES:2 * LANES], d == 1) for r, d in zip(rows, ds)]
        gx = [jnp.where(lo, g[:, 4 + 2 * d:5 + 2 * d], g[:, 5 + 2 * d:6 + 2 * d]) for g, d in zip(gcum, ds)]
        eg = each(jnp.exp, gx)
        kb = each(jnp.multiply, kn, bx)
        vb = each(jnp.multiply, v, bx)
        kbg = each(jnp.multiply, kb, eg)
        grow = [jnp.sum(jnp.where(rowi == colj, g, 0.0), axis=0, keepdims=True) for g in gx]
        decay = [jnp.where(incl_dir[d], jnp.exp(jnp.where(incl_dir[d], g - gr, 0.0)), 0.0)
                 for g, gr, d in zip(gx, grow, ds)]
        kst = [_blk(_bf(k), lo) for k in kn]
        kq = [_dot_nt(jnp.concatenate([_bf(a), _bf(b)], axis=0), c) for a, b, c in zip(kb, qn, kst)]
        low = [jnp.where(strict_dir[d], x[0:CHUNK] * dc, 0.0) for x, dc, d in zip(kq, decay, ds)]
        intra = [jnp.where(incl_dir[d], x[CHUNK:2 * CHUNK] * dc, 0.0) for x, dc, d in zip(kq, decay, ds)]
        for r, x in zip(rows, zip(ds, qn, eg, intra)):
            d, q_, e_, i_ = x
            qg_ref[d, r, :] = _bf(q_ * e_)
            in_ref[d, r, :] = _bf(i_)
        lb = each(_bf, low)
        p = [_dot(a, _blk(a, lo)) for a in lb]
        nn = [-a for a in low]
        n_sq = int(math.log2(CHUNK)) - 1
        for r_ in range(n_sq):
            pb = each(_bf, p)
            pblk = [_blk(a, lo) for a in pb]
            if r_ < n_sq - 1:
                both = [_dot(jnp.concatenate([_bf(n_), b], axis=0), pk) for n_, b, pk in zip(nn, pb, pblk)]
                nn = [n_ + p_ + x[0:CHUNK] for n_, p_, x in zip(nn, p, both)]
                p = [x[CHUNK:2 * CHUNK] for x in both]
            else:
                nn = [n_ + p_ + _dot(_bf(n_), pk) for n_, p_, pk in zip(nn, p, pblk)]
        nb = each(_bf, nn)
        uw = [_dot(n_, jnp.concatenate([_blk(_bf(a), lo), _blk(_bf(b), lo)], axis=1))
              for n_, a, b in zip(nb, vb, kbg)]
        u = [a + x[:, 0:LANES] for a, x in zip(vb, uw)]
        w = [a + x[:, LANES:2 * LANES] for a, x in zip(kbg, uw)]
        g_last = [g[0:1, :] if d == 1 else g[CHUNK - 1:CHUNK, :] for g, d in zip(gx, ds)]
        kdec = [_bf(k * jnp.exp(gl - g)) for k, gl, g in zip(kn, g_last, gx)]
        wb = each(_bf, w)
        kr = [_dot_tn(a, jnp.concatenate([b, _bf(c)], axis=1)) for a, b, c in zip(kdec, wb, u)]
        for i, (c, d) in enumerate(chains):
            w_ref[d, rows[i], :] = wb[i]
            u_ref[d, rows[i], :] = u[i]
            k2_ref[d, srows[i], :] = _bf(jnp.where(blockmask, kr[i][:, 0:LANES], 0.0))
            r_ref[d, srows[i], :] = jnp.where(blockmask, kr[i][:, LANES:2 * LANES], 0.0)
            el_ref[d, pl.ds(pl.multiple_of(c * 8, 8), 8), :] = jnp.broadcast_to(jnp.exp(g_last[i]), (8, LANES))

    ncc = tc // CHUNK
    nct = tt // CHUNK
    ng = max(g for g in range(1, UNROLL_GDN + 1) if nct % g == 0)

    def group_chains(i):
        return [(i * ng + k, d) for k in range(ng) for d in range(2)]

    def body_a(i, carry):
        prepare(group_chains(i))
        return carry

    lax.fori_loop(0, nct // ng, body_a, 0)

    def advance(c, d, s):
        srows = pl.ds(pl.multiple_of(c * LANES, LANES), LANES)
        s16 = _bf(s)
        s_ref[d, srows, :] = s16
        e = el_ref[d, pl.ds(pl.multiple_of(c * 8, 8), 1), :]
        return s * e - _dot(k2_ref[d, srows, :], s16) + r_ref[d, srows, :]

    def body_b(i, carry):
        sf, sb = carry
        cb = jnp.where(i < ncc, ncc - 1 - i, nct - 1 - (i - ncc))
        return advance(i, 0, sf), advance(cb, 1, sb)

    zero = jnp.zeros((LANES, LANES), F32)
    lax.fori_loop(0, nct, body_b, (zero, zero))

    def emit(chains):
        rows = [pl.ds(pl.multiple_of(c * CHUNK, CHUNK), CHUNK) for c, _ in chains]
        s16 = [s_ref[d, pl.ds(pl.multiple_of(c * LANES, LANES), LANES), :] for c, d in chains]
        ws = [_dot(w_ref[d, r, :], s) for (_, d), r, s in zip(chains, rows, s16)]
        qs = [_dot(qg_ref[d, r, :], s) for (_, d), r, s in zip(chains, rows, s16)]
        v_new = [u_ref[d, r, :] - a for (_, d), r, a in zip(chains, rows, ws)]
        iv = [_dot(in_ref[d, r, :], _blk(_bf(vn), lo)) for (_, d), r, vn in zip(chains, rows, v_new)]
        for (_, d), r, a, b in zip(chains, rows, qs, iv):
            (ob_ref if d == 1 else of_ref)[r, :] = a + b

    def body_c(i, carry):
        emit(group_chains(i))
        return carry

    lax.fori_loop(0, nct // ng, body_c, 0)

    o = of_ref[...] + ob_ref[...]
    ss = _split_dot(o * o, seg)
    o = o * lax.rsqrt(ss * (1.0 / GDN_DK) + EPS) * nw_ref[...]
    o_ref[...] = _bf(o * _silu(z_ref[...]))


def _gdn(gqkv, gz, gba, conv_p, hp, nw, *, B, TT, tc):
    NT = gqkv.shape[0]
    t = TT - tc
    w3 = 3 * LANES
    nch = TT // CHUNK
    kern = functools.partial(_gdn_kernel, tc=tc, t=t)
    return pl.pallas_call(
        kern,
        grid=(B, GDN_PAIRS),
        in_specs=[pl.BlockSpec((TT, w3), lambda b, p: (b, p)),
                  pl.BlockSpec((TT, LANES), lambda b, p: (b, p)),
                  pl.BlockSpec((TT, LANES), lambda b, p: (b, p)),
                  pl.BlockSpec((None, 8, w3), lambda b, p: (p, 0, 0)),
                  pl.BlockSpec((None, 8, LANES), lambda b, p: (p, 0, 0)),
                  pl.BlockSpec((1, LANES), lambda b, p: (0, 0))],
        out_specs=pl.BlockSpec((TT, LANES), lambda b, p: (b, p)),
        out_shape=jax.ShapeDtypeStruct((NT, GDN_W), BF16),
        scratch_shapes=[pltpu.VMEM((3, t + 16, LANES), F32), pltpu.VMEM((3, TT, LANES), F32),
                        pltpu.VMEM((TT, 2 * LANES), F32), pltpu.VMEM((TT, LANES), F32),
                        pltpu.VMEM((TT, LANES), F32),
                        pltpu.VMEM((2, TT, LANES), BF16), pltpu.VMEM((2, TT, LANES), BF16),
                        pltpu.VMEM((2, TT, LANES), BF16), pltpu.VMEM((2, TT, LANES), F32),
                        pltpu.VMEM((2, nch * LANES, LANES), BF16), pltpu.VMEM((2, nch * LANES, LANES), F32),
                        pltpu.VMEM((2, nch * 8, LANES), F32), pltpu.VMEM((2, nch * LANES, LANES), BF16)],
        compiler_params=_cparams(("parallel", "parallel")),
        name="gdn",
    )(gqkv, gz, gba, conv_p, hp, nw)


def _att_tile(q_ref, k_ref, v_ref, o_ref, nk):
    lo = lax.broadcasted_iota(jnp.int32, (1, LANES), 1) < HALF
    k = k_ref[0:nk, :]
    v = v_ref[0:nk, :]

    def scores(h):
        j, half = h // 2, h % 2
        qj = q_ref[:, j * LANES:(j + 1) * LANES]
        zq = jnp.zeros_like(qj)
        return _dot_nt(jnp.where(lo, qj, zq) if half == 0 else jnp.where(lo, zq, qj), k)

    outs = []
    s = scores(0)
    for h in range(ATT_HEADS):
        s_next = scores(h + 1) if h + 1 < ATT_HEADS else None
        m = jnp.max(s, axis=-1, keepdims=True)
        p = jnp.exp(s - m)
        l = jnp.sum(p, axis=-1, keepdims=True)
        outs.append(_dot(_bf(p), v) / l)
        s = s_next
    for j in range(ATT_GROUP):
        o_ref[:, j * LANES:(j + 1) * LANES] = _bf(jnp.where(lo, outs[2 * j], outs[2 * j + 1]))


def _att_kernel(q_ref, k_ref, v_ref, o_ref, *, n_ctx_tiles, tc, tt):
    is_ctx = pl.program_id(1) < n_ctx_tiles

    @pl.when(is_ctx)
    def _():
        _att_tile(q_ref, k_ref, v_ref, o_ref, tc)

    @pl.when(jnp.logical_not(is_ctx))
    def _():
        _att_tile(q_ref, k_ref, v_ref, o_ref, tt)


def _attention(aq, ak, av, *, B, TT, tc):
    NT = aq.shape[0]
    tq = 256
    nq = TT // tq
    kv_spec = pl.BlockSpec((TT, LANES), lambda b, i: (b, 0))
    q_spec = pl.BlockSpec((tq, ATT_W), lambda b, i: (b * nq + i, 0))
    return pl.pallas_call(
        functools.partial(_att_kernel, n_ctx_tiles=tc // tq, tc=tc, tt=TT),
        grid=(B, nq),
        in_specs=[q_spec, kv_spec, kv_spec],
        out_specs=q_spec,
        out_shape=jax.ShapeDtypeStruct((NT, ATT_W), BF16),
        compiler_params=_cparams(("parallel", "parallel")),
        name="attention",
    )(aq, ak, av)


S5_BB = 2
GPV = LANES // S5_GH


def _s5_kernel(u0_ref, u1_ref, tg_ref, bp_ref, cp_ref, la_ref, lb_ref, d_ref, gw_ref, gb_ref, o_ref,
               uf_ref, sloc_ref, sin_ref, sinb_ref, y0_ref, y1_ref, *, bb, ncc, nct):
    lc = S5_CHUNK
    nr = bb * nct
    w2 = 2 * S5_P
    blk8 = lax.broadcasted_iota(jnp.int32, (1, LANES), 1) // S5_GH
    u_refs = (u0_ref, u1_ref)
    y_refs = (y0_ref, y1_ref)

    def block_transpose(xs):
        xs = list(xs)
        for s in (4, 2, 1):
            keep = (blk8 & s) == 0
            for a in range(GPV):
                if a & s == 0:
                    lo_, hi_ = xs[a], xs[a + s]
                    xs[a] = jnp.where(keep, lo_, pltpu.roll(hi_, s * S5_GH, 1))
                    xs[a + s] = jnp.where(keep, pltpu.roll(lo_, LANES - s * S5_GH, 1), hi_)
        return xs

    rbk = 32
    for a in range(S5_GROUPS // GPV):
        for v in range(lc // GPV):
            for r0 in range(0, nr, rbk):
                xs = [u_refs[a][pl.ds((v * GPV + k) + lc * r0, rbk, stride=lc), :] for k in range(GPV)]
                for k, y_ in enumerate(block_transpose(xs)):
                    uf_ref[a * GPV + k, r0:r0 + rbk, v * LANES:(v + 1) * LANES] = y_
    for g in range(S5_GROUPS):
        sl = _dot(_bf(uf_ref[g]), bp_ref[g])
        sloc_ref[2 * g] = sl[:, 0:w2]
        sloc_ref[2 * g + 1] = sl[:, w2:2 * w2]

    lo = lax.broadcasted_iota(jnp.int32, (1, LANES), 1) < S5_P

    def body(i, carry):
        cb = jnp.where(i < ncc, ncc - 1 - i, nct - 1 - (i - ncc))
        rows_f = pl.ds(i, bb, stride=nct)
        rows_b = pl.ds(cb, bb, stride=nct)
        out = []
        for g in range(S5_GROUPS):
            re, im = carry[2 * g], carry[2 * g + 1]
            sin_ref[2 * g, rows_f, :] = re
            sin_ref[2 * g + 1, rows_f, :] = im
            sinb_ref[2 * g, rows_b, :] = re
            sinb_ref[2 * g + 1, rows_b, :] = im
            b_re = jnp.where(lo, sloc_ref[2 * g, rows_f, :], sloc_ref[2 * g, rows_b, :])
            b_im = jnp.where(lo, sloc_ref[2 * g + 1, rows_f, :], sloc_ref[2 * g + 1, rows_b, :])
            lr, li = la_ref[g], lb_ref[g]
            out.append(re * lr - im * li + b_re)
            out.append(im * lr + re * li + b_im)
        return tuple(out)

    zero = jnp.zeros((bb, w2), F32)
    lax.fori_loop(0, nct, body, (zero,) * (2 * S5_GROUPS))

    for g in range(S5_GROUPS):
        uf_ref[g] = (_dot(_bf(uf_ref[g]), tg_ref[g])
                     + _dot(_bf(sin_ref[2 * g]), cp_ref[g, 0])
                     + _dot(_bf(sin_ref[2 * g + 1]), cp_ref[g, 1])
                     + _dot(_bf(sinb_ref[2 * g]), cp_ref[g, 2])
                     + _dot(_bf(sinb_ref[2 * g + 1]), cp_ref[g, 3]))

    for a in range(S5_GROUPS // GPV):
        for v in range(lc // GPV):
            for r0 in range(0, nr, rbk):
                xs = [uf_ref[a * GPV + k, r0:r0 + rbk, v * LANES:(v + 1) * LANES] for k in range(GPV)]
                for k, y_ in enumerate(block_transpose(xs)):
                    y_refs[a][pl.ds((v * GPV + k) + lc * r0, rbk, stride=lc), :] = y_

    y = jnp.concatenate([y0_ref[...], y1_ref[...]], axis=1)
    u = jnp.concatenate([u0_ref[...], u1_ref[...]], axis=1)
    zz = _gelu_tanh(y + d_ref[...] * u)
    o_ref[...] = _bf(zz * _sigmoid(_dot(_bf(zz), gw_ref[...]) + gb_ref[...]))


def _s5(su, tg, bp, cp, la, lb, d, gw, gb, *, B, TT, tc):
    NT, W = su.shape
    bb = S5_BB
    nct = TT // S5_CHUNK
    nr = bb * nct
    kern = functools.partial(_s5_kernel, bb=bb, ncc=tc // S5_CHUNK, nct=nct)
    full = lambda a: pl.BlockSpec(a.shape, lambda i: (0,) * a.ndim)
    return pl.pallas_call(
        kern,
        grid=(B // bb,),
        in_specs=[pl.BlockSpec((bb * TT, LANES), lambda i: (i, 0)),
                  pl.BlockSpec((bb * TT, LANES), lambda i: (i, 1)),
                  full(tg), full(bp), full(cp), full(la), full(lb), full(d), full(gw), full(gb)],
        out_specs=pl.BlockSpec((bb * TT, W), lambda i: (i, 0)),
        out_shape=jax.ShapeDtypeStruct((NT, W), BF16),
        scratch_shapes=[pltpu.VMEM((S5_GROUPS, nr, W), F32), pltpu.VMEM((2 * S5_GROUPS, nr, LANES), F32),
                        pltpu.VMEM((2 * S5_GROUPS, nr, LANES), F32), pltpu.VMEM((2 * S5_GROUPS, nr, LANES), F32),
                        pltpu.VMEM((bb * TT, LANES), F32), pltpu.VMEM((bb * TT, LANES), F32)],
        compiler_params=_cparams(("parallel",)),
        name="s5",
    )(su, su, tg, bp, cp, la, lb, d, gw, gb)


def _s5_matrices(lam_re, lam_im, log_dt, b_re, b_im, c_re, c_im):
    Lc = S5_CHUNK
    lam = lax.complex(lam_re.astype(F32), lam_im.astype(F32))
    dt = jnp.exp(log_dt.astype(F32))[..., None]
    lam_dt = lam * dt
    lam_bar = jnp.exp(lam_dt)
    b_bar = ((lam_bar - 1.0) / lam)[..., None] * lax.complex(b_re.astype(F32), b_im.astype(F32))
    cc = lax.complex(c_re.astype(F32), c_im.astype(F32))
    ks = jnp.arange(Lc + 1, dtype=F32)
    pw = jnp.exp(lam_dt[..., None, :] * ks[:, None])
    kern = jnp.real(jnp.einsum('dghp,dgkp,dgpi->dgkhi', cc, pw[:, :, :Lc], b_bar))
    i_idx = jnp.arange(Lc)[:, None]
    j_idx = jnp.arange(Lc)[None, :]
    lag = i_idx - j_idx
    kf = kern[0][:, jnp.clip(lag, 0, Lc - 1)]
    kb = kern[1][:, jnp.clip(-lag, 0, Lc - 1)]
    tmat = (jnp.where((lag >= 0)[None, :, :, None, None], kf, 0.0)
            + jnp.where((lag <= 0)[None, :, :, None, None], kb, 0.0))
    tg = tmat.transpose(0, 2, 4, 1, 3).reshape(S5_GROUPS, Lc * S5_GH, Lc * S5_GH)
    pf = pw[0][:, Lc - 1 - jnp.arange(Lc)]
    pb = pw[1][:, jnp.arange(Lc)]
    sf = jnp.einsum('gjp,gph->gjhp', pf, b_bar[0])
    sb = jnp.einsum('gjp,gph->gjhp', pb, b_bar[1])
    bp = jnp.concatenate([jnp.real(sf), jnp.real(sb), jnp.imag(sf), jnp.imag(sb)], axis=-1)
    bp = bp.reshape(S5_GROUPS, Lc * S5_GH, 4 * S5_P)
    qf = jnp.einsum('ghp,gip->gpih', cc[0], pw[0][:, 1 + jnp.arange(Lc)])
    qb = jnp.einsum('ghp,gip->gpih', cc[1], pw[1][:, Lc - jnp.arange(Lc)])
    zq = jnp.zeros(qf.shape, F32)
    cp = jnp.stack([jnp.concatenate([jnp.real(qf), zq], axis=1), jnp.concatenate([-jnp.imag(qf), zq], axis=1),
                    jnp.concatenate([zq, jnp.real(qb)], axis=1), jnp.concatenate([zq, -jnp.imag(qb)], axis=1)],
                   axis=1)
    cp = cp.reshape(S5_GROUPS, 4, 2 * S5_P, Lc * S5_GH)
    lc = pw[:, :, Lc]
    la = jnp.concatenate([jnp.real(lc[0]), jnp.real(lc[1])], axis=-1)
    lb = jnp.concatenate([jnp.imag(lc[0]), jnp.imag(lc[1])], axis=-1)
    return (tg.astype(BF16), bp.astype(BF16), cp.astype(BF16),
            la.reshape(S5_GROUPS, 1, 2 * S5_P), lb.reshape(S5_GROUPS, 1, 2 * S5_P))


def _gelu_tanh(x):
    return 0.5 * x * (1.0 + jnp.tanh(math.sqrt(2.0 / math.pi) * (x + 0.044715 * (x * x * x))))


def _kout_kernel(x_ref, mod_c_ref, mod_b_ref, go_ref, ao_ref, so_ref,
                 wg_ref, wa_ref, ws_ref, g_ref, b_ref, o_ref, *, tm, tc, tpb):
    rm = functools.partial(_row_mod, mod_c_ref, mod_b_ref, _ctx_rows(tm, tc, tpb))
    y = _dot(go_ref[...], wg_ref[...]) + _dot(ao_ref[...], wa_ref[...]) + _dot(so_ref[...], ws_ref[...])
    z = ALPHA * x_ref[...] + rm(5) * y
    o_ref[...] = _layer_norm(z, g_ref[...], b_ref[...])


def _kout(xa, mod, go, ao, so, wg, wa, ws, g, b, *, B, TT, tc):
    NT, D = xa.shape
    tpb = 4
    tm = TT // tpb
    kern = functools.partial(_kout_kernel, tm=tm, tc=tc, tpb=tpb)
    row = lambda w: pl.BlockSpec((tm, w), lambda i: (i, 0))
    full = lambda a: pl.BlockSpec(a.shape, lambda i: (0,) * a.ndim)
    return pl.pallas_call(
        kern,
        grid=(NT // tm,),
        in_specs=[row(D),
                  pl.BlockSpec((None, N_MOD, D), lambda i: (B, 0, 0)),
                  pl.BlockSpec((None, N_MOD, D), lambda i: (i // tpb, 0, 0)),
                  row(GDN_W), row(ATT_W), row(S5_W),
                  full(wg), full(wa), full(ws), full(g), full(b)],
        out_specs=row(D),
        out_shape=jax.ShapeDtypeStruct((NT, D), F32),
        compiler_params=_cparams(("parallel",)),
        name="mixer_out",
    )(xa, mod, mod, go, ao, so, wg, wa, ws, g, b)


def _perm_in_cols():
    off_z = GDN_QKV
    off_b = off_z + GDN_W
    off_a = off_b + 2 * GDN_HEADS
    off_q = off_a + 2 * GDN_HEADS
    off_k = off_q + ATT_W
    off_v = off_k + ATT_KV_HEADS * ATT_DH
    off_u = off_v + ATT_KV_HEADS * ATT_DH
    cols = []
    for p in range(GDN_PAIRS):
        for part in range(3):
            base = part * GDN_W + 2 * p * GDN_DK
            cols += list(range(base, base + 2 * GDN_DK))
    cols += list(range(off_z, off_z + GDN_W))
    for p in range(GDN_PAIRS):
        blk = []
        for off in (off_b, off_a):
            for d in range(2):
                blk += [off + d * GDN_HEADS + 2 * p, off + d * GDN_HEADS + 2 * p + 1]
        cols += blk + [-1] * (LANES - len(blk))
    for j in range(ATT_GROUP):
        for h in (j, j + ATT_GROUP):
            cols += list(range(off_q + h * ATT_DH, off_q + (h + 1) * ATT_DH))
    cols += list(range(off_k, off_k + 2 * ATT_DH))
    cols += list(range(off_v, off_v + 2 * ATT_DH))
    cols += list(range(off_u, off_u + S5_W))
    return np.asarray(cols, dtype=np.int32)


def _gdn_pair_cols():
    cols = []
    for p in range(GDN_PAIRS):
        for part in range(3):
            base = part * GDN_W + 2 * p * GDN_DK
            cols += list(range(base, base + 2 * GDN_DK))
    return np.asarray(cols, dtype=np.int32)


def _att_out_rows():
    rows = []
    for j in range(ATT_GROUP):
        for h in (j, j + ATT_GROUP):
            rows += list(range(h * ATT_DH, (h + 1) * ATT_DH))
    return np.asarray(rows, dtype=np.int32)


def _take(w, idx, axis):
    idx = [int(i) for i in idx]
    parts = []
    i = 0
    while i < len(idx):
        j = i + 1
        if idx[i] < 0:
            while j < len(idx) and idx[j] < 0:
                j += 1
            shape = list(w.shape)
            shape[axis] = j - i
            parts.append(jnp.zeros(shape, w.dtype))
        else:
            while j < len(idx) and idx[j] == idx[j - 1] + 1:
                j += 1
            parts.append(lax.slice_in_dim(w, idx[i], idx[j - 1] + 1, axis=axis))
        i = j
    return jnp.concatenate(parts, axis=axis)


def _rope_tables(tc, t):
    rows = t // GRID_W
    row = jnp.repeat(jnp.arange(rows), GRID_W)
    col = jnp.tile(jnp.arange(GRID_W), rows)
    inv_freq = ROPE_THETA ** (-jnp.arange(ROPE_PAIRS, dtype=F32) / ROPE_PAIRS)
    ang = jnp.stack([row, col], axis=-1).astype(F32)[..., None] * inv_freq
    cos, sin = jnp.cos(ang), jnp.sin(ang)
    zero = jnp.zeros_like(sin)
    cos_h = jnp.stack([cos, cos], axis=2).reshape(t, ATT_DH)
    sa_h = jnp.stack([-sin, zero], axis=2).reshape(t, ATT_DH)
    sb_h = jnp.stack([zero, sin], axis=2).reshape(t, ATT_DH)
    def full(tab, fill):
        tab = jnp.concatenate([jnp.full((tc, ATT_DH), fill, F32), tab], axis=0)
        return jnp.concatenate([tab, tab], axis=1)
    return full(cos_h, 1.0), full(sa_h, 0.0), full(sb_h, 0.0)


def kernel(x, c, ctx, c_ctx, w_ada, b_ada, ln_g, ln_b, ffn_w1, ffn_w3, ffn_w2, w_in, w_out, gdn_conv_w,
           gdn_a_log, gdn_dt_bias, gdn_norm_w, q_norm_w, k_norm_w, s5_lam_re, s5_lam_im, s5_log_dt,
           s5_b_re, s5_b_im, s5_c_re, s5_c_im, s5_d, glu_w, glu_b):
    B, T, D = x.shape
    tc = ctx.shape[1]
    TT = tc + T
    NT = B * TT
    L = w_ada.shape[0]
    kw = dict(B=B, TT=TT, tc=tc)

    R = ((B + 1 + 7) // 8) * 8
    c_all = jnp.concatenate([c, c_ctx[None, :], jnp.zeros((R - B - 1, D), F32)], axis=0)
    mod_all = _modulation(c_all, w_ada, b_ada).reshape(L, R, N_MOD, D)

    in_cols = _perm_in_cols()
    pair_cols = _gdn_pair_cols()
    att_rows = _att_out_rows()
    cos, sa, sb = _rope_tables(tc, T)

    def head_params(v):
        v = v.astype(F32).reshape(2, GDN_PAIRS, 2).transpose(1, 0, 2).reshape(GDN_PAIRS, 4)
        return jnp.pad(v, ((0, 0), (4, LANES - 8)))

    xa = jnp.concatenate([ctx, x], axis=1).reshape(NT, D)
    for l in range(L):
        mod = mod_all[l]
        g_l, b_l = ln_g[l], ln_b[l]
        xa = _ffn(xa, mod, _ffn_pack(ffn_w1[l, 0], ffn_w3[l, 0], ffn_w2[l, 0]), g_l[0:1], b_l[0:1], sub=0, **kw)

        w_in_p = _take(_bf(w_in[l]), in_cols, 1)
        qw = jnp.tile(q_norm_w[l], 2)[None, :]
        kwt = jnp.tile(k_norm_w[l], 2)[None, :]
        gqkv, gz, gba, aq, ak, av, su = _kin(xa, mod, w_in_p, qw, kwt, cos, sa, sb, **kw)

        conv_p = jnp.pad(_take(gdn_conv_w[l], pair_cols, 1), ((0, 8 - CONV_K), (0, 0)))
        conv_p = conv_p.reshape(8, GDN_PAIRS, 3 * LANES).transpose(1, 0, 2)
        hp = jnp.stack([head_params(-jnp.exp(gdn_a_log[l].astype(F32))), head_params(gdn_dt_bias[l])], axis=1)
        hp = jnp.pad(hp, ((0, 0), (0, 6), (0, 0)))
        nw = jnp.tile(gdn_norm_w[l], 2)[None, :]
        go = _gdn(gqkv, gz, gba, conv_p, hp, nw, **kw)

        ao = _attention(aq, ak, av, **kw)

        tg, bp, cp, la, lb = _s5_matrices(s5_lam_re[l], s5_lam_im[l], s5_log_dt[l], s5_b_re[l], s5_b_im[l],
                                          s5_c_re[l], s5_c_im[l])
        so = _s5(su, tg, bp, cp, la, lb, s5_d[l][None, :], _bf(glu_w[l]), glu_b[l][None, :], **kw)

        wo = _bf(w_out[l])
        wg = wo[0:GDN_W]
        wa = _take(wo[GDN_W:GDN_W + ATT_W], att_rows, 0)
        ws = wo[GDN_W + ATT_W:]
        xa = _kout(xa, mod, go, ao, so, wg, wa, ws, g_l[1:2], b_l[1:2], **kw)
        xa = _ffn(xa, mod, _ffn_pack(ffn_w1[l, 1], ffn_w3[l, 1], ffn_w2[l, 1]), g_l[2:3], b_l[2:3], sub=2, **kw)
    return xa.reshape(B, TT, D)[:, tc:, :]
```
